```python
import math
import jax
import jax.numpy as jnp
from jax import lax
import numpy as np

D_MODEL = 1024
BATCH = 8
SEQ = 8192
DEPTH = 4

MEM_LEN = 256
EPS = 1e-6

N_Q_HEADS = 16
N_KV_HEADS = 2
HEAD_DIM = 64
WINDOW = 128
BLOCK = 128
N_BUCKETS = 32
MAX_DISTANCE = 128
N_X_HEADS = 4
X_HEAD_DIM = 256
EXPAND = 2
D_INNER = EXPAND * D_MODEL
SSM_HEAD_DIM = 64
N_SSM_HEADS = D_INNER // SSM_HEAD_DIM
N_SSM_GROUPS = 4
HEADS_PER_GROUP = N_SSM_HEADS // N_SSM_GROUPS
D_STATE = 128
CONV_WIDTH = 4
CHUNK = 128
DT_MIN = 1e-3
DT_MAX = 1e-1
D_FF = 2816
N_EXPERTS = 8
TOP_K = 2

Q_W = N_Q_HEADS * HEAD_DIM
KV_W = N_KV_HEADS * HEAD_DIM
XQ_W = N_X_HEADS * X_HEAD_DIM
SWA_IN_W = Q_W + 2 * KV_W + XQ_W
SWA_CAT_W = Q_W + XQ_W
BC_W = N_SSM_GROUPS * D_STATE
CONV_CH = D_INNER + 2 * BC_W
SSM_IN_W = D_INNER + CONV_CH + N_SSM_HEADS + XQ_W
SSM_CAT_W = D_INNER + XQ_W
N_EVEN = (DEPTH + 1) // 2
N_ODD = DEPTH // 2

kernel_name = 'hybrid_swa_ssd_moe_memory_trunk'


def rmsnorm(x, g):
    xf = x.astype(jnp.float32)
    y = xf * lax.rsqrt(jnp.mean(xf * xf, axis=-1, keepdims=True) + EPS)
    return (y * g.astype(jnp.float32)).astype(x.dtype)


def t5_bucket(dist):
    max_exact = N_BUCKETS // 2
    d = jnp.maximum(dist, 0)
    df = jnp.maximum(d, 1).astype(jnp.float32)
    far = max_exact + (jnp.log(df / max_exact) / math.log(MAX_DISTANCE / max_exact) * (N_BUCKETS - max_exact)).astype(jnp.int32)
    return jnp.where(d < max_exact, d, jnp.minimum(far, N_BUCKETS - 1))


def sliding_window_attention(q, k, v, sinks, rel_bias):
    b, s = q.shape[0], q.shape[1]
    nb = s // BLOCK
    grp = N_Q_HEADS // N_KV_HEADS
    qb = q.reshape(b, nb, BLOCK, N_KV_HEADS, grp, HEAD_DIM).transpose(1, 0, 3, 4, 2, 5)

    def band_keys(t):
        tb = t.reshape(b, nb, BLOCK, N_KV_HEADS, HEAD_DIM)
        prev = jnp.concatenate([jnp.zeros_like(tb[:, :1]), tb[:, :-1]], axis=1)
        return jnp.concatenate([prev, tb], axis=2).transpose(1, 0, 3, 2, 4)

    kw, vw = band_keys(k), band_keys(v)
    qi = jnp.arange(BLOCK)[:, None]
    kj = jnp.arange(2 * BLOCK)[None, :]
    dist = BLOCK + qi - kj
    band = (dist >= 0) & (dist < WINDOW)
    bias = rel_bias[t5_bucket(dist)].astype(jnp.float32).transpose(2, 0, 1).reshape(N_KV_HEADS, grp, BLOCK, 2 * BLOCK)
    sink = sinks.astype(jnp.float32).reshape(N_KV_HEADS, grp, 1, 1)
    scale = HEAD_DIM ** -0.5

    def one_block(args):
        qx, kx, vx, idx = args
        logits = jnp.einsum('bhgqd,bhkd->bhgqk', qx, kx, preferred_element_type=jnp.float32) * scale + bias
        valid = band & (idx * BLOCK - BLOCK + kj >= 0)
        logits = jnp.where(valid, logits, -jnp.inf)
        m = jnp.maximum(jnp.max(logits, axis=-1, keepdims=True), sink)
        p = jnp.exp(logits - m)
        p = p / (jnp.sum(p, axis=-1, keepdims=True) + jnp.exp(sink - m))
        return jnp.einsum('bhgqk,bhkd->bhgqd', p.astype(vx.dtype), vx)

    out = lax.map(one_block, (qb, kw, vw, jnp.arange(nb)))
    return out.transpose(1, 0, 4, 2, 3, 5).reshape(b, s, Q_W)


def memory_cross_attention(q, mem_k, mem_v):
    logits = jnp.einsum('bshd,bmhd->bhsm', q, mem_k, preferred_element_type=jnp.float32) * (X_HEAD_DIM ** -0.5)
    p = jax.nn.softmax(logits, axis=-1)
    out = jnp.einsum('bhsm,bmhd->bshd', p.astype(mem_v.dtype), mem_v)
    return out.reshape(q.shape[0], q.shape[1], XQ_W)


def causal_depthwise_conv(u, w, bias):
    out = lax.conv_general_dilated(u, w[:, None, :].astype(u.dtype), window_strides=(1,), padding=[(CONV_WIDTH - 1, 0)], dimension_numbers=('NWC', 'WIO', 'NWC'), feature_group_count=u.shape[-1])
    return out + bias.astype(u.dtype)


def ssd_chunked_scan(x, dt, a, bmat, cmat):
    b, s = x.shape[0], x.shape[1]
    nc = s // CHUNK

    def chunks(t):
        return t.reshape(b, nc, CHUNK, *t.shape[2:]).swapaxes(0, 1)

    causal = jnp.tril(jnp.ones((CHUNK, CHUNK), dtype=bool))

    def step(state, inp):
        xk, dtk, bk, ck = inp
        acum = jnp.cumsum(dtk * a, axis=1)
        at = acum.transpose(0, 2, 3, 1)
        decay = jnp.exp(jnp.where(causal, at[..., :, None] - at[..., None, :], -jnp.inf))
        cb = jnp.einsum('btgn,bsgn->bgts', ck, bk)
        xdt = xk * dtk[..., None]
        y_diag = jnp.einsum('bgets,bsgep->btgep', cb[:, :, None] * decay, xdt)
        y_off = jnp.einsum('btgn,bgepn->btgep', ck, state) * jnp.exp(acum)[..., None]
        to_end = jnp.exp(acum[:, -1:] - acum)
        new_state = state * jnp.exp(acum[:, -1])[..., None, None] + jnp.einsum('bsgn,bsge,bsgep->bgepn', bk, to_end, xdt)
        return new_state, y_diag + y_off

    state0 = jnp.zeros((b, N_SSM_GROUPS, HEADS_PER_GROUP, SSM_HEAD_DIM, D_STATE), jnp.float32)
    _, ys = lax.scan(step, state0, (chunks(x), chunks(dt), chunks(bmat), chunks(cmat)))
    return ys.swapaxes(0, 1).reshape(b, s, D_INNER)


def swa_mixer(h, mem_k, mem_v, w_in, sinks, w_out, rel_bias):
    b, s = h.shape[0], h.shape[1]
    q, k, v, xq = jnp.split(h @ w_in, [Q_W, Q_W + KV_W, Q_W + 2 * KV_W], axis=-1)
    attn = sliding_window_attention(q.reshape(b, s, N_Q_HEADS, HEAD_DIM), k.reshape(b, s, N_KV_HEADS, HEAD_DIM), v.reshape(b, s, N_KV_HEADS, HEAD_DIM), sinks, rel_bias)
    cross = memory_cross_attention(xq.reshape(b, s, N_X_HEADS, X_HEAD_DIM), mem_k, mem_v)
    return jnp.concatenate([attn, cross], axis=-1) @ w_out


def ssd_mixer(h, mem_k, mem_v, w_in, conv_w, conv_b, dt_bias, a_log, d_skip, g_norm, w_out):
    b, s = h.shape[0], h.shape[1]
    z, xbc, dt_raw, xq = jnp.split(h @ w_in, [D_INNER, D_INNER + CONV_CH, D_INNER + CONV_CH + N_SSM_HEADS], axis=-1)
    xbc = jax.nn.silu(causal_depthwise_conv(xbc, conv_w, conv_b))
    xs, bmat, cmat = jnp.split(xbc.astype(jnp.float32), [D_INNER, D_INNER + BC_W], axis=-1)
    dt = jax.nn.softplus(dt_raw.astype(jnp.float32) + dt_bias.astype(jnp.float32)).reshape(b, s, N_SSM_GROUPS, HEADS_PER_GROUP)
    a = -jnp.exp(a_log.astype(jnp.float32)).reshape(N_SSM_GROUPS, HEADS_PER_GROUP)
    xs = xs.reshape(b, s, N_SSM_GROUPS, HEADS_PER_GROUP, SSM_HEAD_DIM)
    y = ssd_chunked_scan(xs, dt, a, bmat.reshape(b, s, N_SSM_GROUPS, D_STATE), cmat.reshape(b, s, N_SSM_GROUPS, D_STATE))
    y = y + (xs * d_skip.astype(jnp.float32).reshape(N_SSM_GROUPS, HEADS_PER_GROUP, 1)).reshape(b, s, D_INNER)
    y = rmsnorm(y * jax.nn.silu(z.astype(jnp.float32)), g_norm).astype(h.dtype)
    cross = memory_cross_attention(xq.reshape(b, s, N_X_HEADS, X_HEAD_DIM), mem_k, mem_v)
    return jnp.concatenate([y, cross], axis=-1) @ w_out


def swiglu(h, wg, wu, wd):
    return (jax.nn.silu(h @ wg) * (h @ wu)) @ wd


def moe_swiglu(h, w_router, wg, wu, wd):
    b, s, d = h.shape
    logits = jnp.einsum('bsd,de->bse', h, w_router, preferred_element_type=jnp.float32)
    top_val, top_idx = lax.top_k(logits, TOP_K)
    gates = jax.nn.softmax(top_val, axis=-1)
    combine = jnp.sum(jax.nn.one_hot(top_idx, N_EXPERTS, dtype=jnp.float32) * gates[..., None], axis=-2)
    nb = s // BLOCK
    hb = h.reshape(b, nb, BLOCK, d).swapaxes(0, 1)
    cb = combine.reshape(b, nb, BLOCK, N_EXPERTS).swapaxes(0, 1)

    def one_block(args):
        hx, cx = args
        g = jnp.einsum('bqd,edf->bqef', hx, wg)
        u = jnp.einsum('bqd,edf->bqef', hx, wu)
        act = jax.nn.silu(g) * u * cx[..., None].astype(hx.dtype)
        return jnp.einsum('bqef,efd->bqd', act, wd)

    out = lax.map(one_block, (hb, cb))
    return out.swapaxes(0, 1).reshape(b, s, d)


def setup_inputs(seed: int = 0) -> dict:
    key = jax.random.key(seed)
    ks = iter(jax.random.split(key, 40))

    def nrm(shape, scale):
        return jax.random.normal(next(ks), shape, jnp.float32) * scale

    def gain(shape):
        return 1.0 + 0.02 * jax.random.normal(next(ks), shape, jnp.float32)

    inp = {}
    inp['x'] = nrm((BATCH, SEQ, D_MODEL), 1.0)
    inp['mem'] = nrm((BATCH, MEM_LEN, D_MODEL), 1.0)
    inp['g_mix'] = gain((DEPTH, D_MODEL))
    inp['g_ffn'] = gain((DEPTH, D_MODEL))
    inp['g_mem'] = gain((DEPTH, D_MODEL))
    inp['w_mem_kv'] = nrm((DEPTH, D_MODEL, 2 * XQ_W), D_MODEL ** -0.5)
    inp['rel_bias'] = nrm((N_BUCKETS, N_Q_HEADS), 0.5)
    inp['swa_w_in'] = nrm((N_EVEN, D_MODEL, SWA_IN_W), D_MODEL ** -0.5)
    inp['swa_sinks'] = nrm((N_EVEN, N_Q_HEADS), 1.0)
    inp['swa_w_out'] = nrm((N_EVEN, SWA_CAT_W, D_MODEL), SWA_CAT_W ** -0.5)
    inp['ssm_w_in'] = nrm((N_ODD, D_MODEL, SSM_IN_W), D_MODEL ** -0.5)
    inp['ssm_conv_w'] = nrm((N_ODD, CONV_WIDTH, CONV_CH), CONV_WIDTH ** -0.5)
    inp['ssm_conv_b'] = nrm((N_ODD, CONV_CH), 0.02)
    dt0 = jnp.exp(jax.random.uniform(next(ks), (N_ODD, N_SSM_HEADS), jnp.float32, minval=math.log(DT_MIN), maxval=math.log(DT_MAX)))
    inp['ssm_dt_bias'] = dt0 + jnp.log(-jnp.expm1(-dt0))
    inp['ssm_A_log'] = jnp.log(jax.random.uniform(next(ks), (N_ODD, N_SSM_HEADS), jnp.float32, minval=1.0, maxval=16.0))
    inp['ssm_D'] = 1.0 + 0.1 * jax.random.normal(next(ks), (N_ODD, N_SSM_HEADS), jnp.float32)
    inp['ssm_g_norm'] = gain((N_ODD, D_INNER))
    inp['ssm_w_out'] = nrm((N_ODD, SSM_CAT_W, D_MODEL), SSM_CAT_W ** -0.5)
    inp['ffn_w_gate'] = nrm((N_EVEN, D_MODEL, D_FF), D_MODEL ** -0.5)
    inp['ffn_w_up'] = nrm((N_EVEN, D_MODEL, D_FF), D_MODEL ** -0.5)
    inp['ffn_w_down'] = nrm((N_EVEN, D_FF, D_MODEL), D_FF ** -0.5)
    inp['moe_w_router'] = nrm((N_ODD, D_MODEL, N_EXPERTS), D_MODEL ** -0.5)
    inp['moe_w_gate'] = nrm((N_ODD, N_EXPERTS, D_MODEL, D_FF), D_MODEL ** -0.5)
    inp['moe_w_up'] = nrm((N_ODD, N_EXPERTS, D_MODEL, D_FF), D_MODEL ** -0.5)
    inp['moe_w_down'] = nrm((N_ODD, N_EXPERTS, D_FF, D_MODEL), D_FF ** -0.5)
    inp['g_final'] = gain((D_MODEL,))
    return inp


def reference(x, mem, g_mix, g_ffn, g_mem, w_mem_kv, rel_bias, swa_w_in, swa_sinks, swa_w_out, ssm_w_in, ssm_conv_w, ssm_conv_b, ssm_dt_bias, ssm_A_log, ssm_D, ssm_g_norm, ssm_w_out, ffn_w_gate, ffn_w_up, ffn_w_down, moe_w_router, moe_w_gate, moe_w_up, moe_w_down, g_final):
    b, m = mem.shape[0], mem.shape[1]
    for i in range(DEPTH):
        j = i // 2
        mem_kv = rmsnorm(mem, g_mem[i]) @ w_mem_kv[i]
        mem_k = mem_kv[..., :XQ_W].reshape(b, m, N_X_HEADS, X_HEAD_DIM)
        mem_v = mem_kv[..., XQ_W:].reshape(b, m, N_X_HEADS, X_HEAD_DIM)
        h = rmsnorm(x, g_mix[i])
        if i % 2 == 0:
            x = x + swa_mixer(h, mem_k, mem_v, swa_w_in[j], swa_sinks[j], swa_w_out[j], rel_bias)
            x = x + swiglu(rmsnorm(x, g_ffn[i]), ffn_w_gate[j], ffn_w_up[j], ffn_w_down[j])
        else:
            x = x + ssd_mixer(h, mem_k, mem_v, ssm_w_in[j], ssm_conv_w[j], ssm_conv_b[j], ssm_dt_bias[j], ssm_A_log[j], ssm_D[j], ssm_g_norm[j], ssm_w_out[j])
            x = x + moe_swiglu(rmsnorm(x, g_ffn[i]), moe_w_router[j], moe_w_gate[j], moe_w_up[j], moe_w_down[j])
    return rmsnorm(x, g_final)
```

```python
import functools
import math

import numpy as np
import jax
import jax.numpy as jnp
from jax import lax
from jax.experimental import pallas as pl
from jax.experimental.pallas import tpu as pltpu

F32 = jnp.float32
BF16 = jnp.bfloat16

D_MODEL = 1024
DEPTH = 4
MEM_LEN = 256
EPS = 1e-6
N_Q_HEADS = 16
N_KV_HEADS = 2
HEAD_DIM = 64
BLOCK = 128
N_BUCKETS = 32
MAX_DISTANCE = 128
N_X_HEADS = 4
X_HEAD_DIM = 256
D_INNER = 2048
SSM_HEAD_DIM = 64
N_SSM_HEADS = 32
N_SSM_GROUPS = 4
HEADS_PER_GROUP = 8
D_STATE = 128
CONV_WIDTH = 4
CHUNK = 128
D_FF = 2816
N_EXPERTS = 8
Q_W = N_Q_HEADS * HEAD_DIM
KV_W = N_KV_HEADS * HEAD_DIM
XQ_W = N_X_HEADS * X_HEAD_DIM
BC_W = N_SSM_GROUPS * D_STATE
CONV_CH = D_INNER + 2 * BC_W
GROUP_W = HEADS_PER_GROUP * SSM_HEAD_DIM

LANES = 128
VMEM_LIMIT = 56 << 20
NEG_INF = float("-inf")
NT_DIMS = (((1,), (1,)), ((), ()))


def _params(n_axes, vmem=VMEM_LIMIT):
    return pltpu.CompilerParams(dimension_semantics=("arbitrary",) * n_axes, vmem_limit_bytes=vmem)


def _resident(shape):
    nd = len(shape)
    return pl.BlockSpec(shape, lambda *_: (0,) * nd, pipeline_mode=pl.Buffered(1))


def _dot(a, b):
    return jnp.dot(a, b, preferred_element_type=F32)


def _rms(x, g):
    return x * lax.rsqrt(jnp.mean(x * x, axis=-1, keepdims=True) + EPS) * g


def _silu(v):
    return v / (1.0 + jnp.exp(-v))


def _norm_proj_kernel(x_ref, g_ref, *refs, n_out, col_chunk):
    w_refs, o_refs = refs[:n_out], refs[n_out:]
    h = _rms(x_ref[...], g_ref[...]).astype(BF16)
    for w_ref, o_ref in zip(w_refs, o_refs):
        n = w_ref.shape[1]
        for c0 in range(0, n, col_chunk):
            c1 = min(c0 + col_chunk, n)
            o_ref[:, c0:c1] = _dot(h, w_ref[:, c0:c1]).astype(o_ref.dtype)


def _norm_proj(x, g, ws, out_dtypes, tm=512):
    t, d = x.shape
    n_out = len(ws)
    return pl.pallas_call(
        functools.partial(_norm_proj_kernel, n_out=n_out, col_chunk=512),
        grid=(t // tm,),
        in_specs=[pl.BlockSpec((tm, d), lambda i: (i, 0)), _resident((1, d))]
        + [_resident(w.shape) for w in ws],
        out_specs=[pl.BlockSpec((tm, w.shape[1]), lambda i: (i, 0)) for w in ws],
        out_shape=[jax.ShapeDtypeStruct((t, w.shape[1]), dt) for w, dt in zip(ws, out_dtypes)],
        compiler_params=_params(1),
        name="norm_proj",
    )(x, g.reshape(1, d), *ws)


def _bucket_table():
    qi = np.arange(BLOCK)[:, None]
    kj = np.arange(2 * BLOCK)[None, :]
    dist = BLOCK + qi - kj
    max_exact = N_BUCKETS // 2
    d = np.maximum(dist, 0)
    df = np.maximum(d, 1).astype(np.float32)
    far = max_exact + (
        np.log(df / np.float32(max_exact)) / np.float32(math.log(MAX_DISTANCE / max_exact))
        * np.float32(N_BUCKETS - max_exact)
    ).astype(np.int32)
    bucket = np.where(d < max_exact, d, np.minimum(far, N_BUCKETS - 1))
    band = (dist >= 0) & (dist < BLOCK)
    return np.where(band, bucket, -1).astype(np.int32)


def _swa_kernel(relb_ref, sink_ref, bucket_ref, q_ref, kvp_ref, kvc_ref, o_ref, bias_scr):
    first = (pl.program_id(0) == 0) & (pl.program_id(1) == 0)

    @pl.when(first)
    def _():
        bucket = bucket_ref[...]
        for h in range(N_Q_HEADS):
            acc = jnp.full((BLOCK, 2 * BLOCK), NEG_INF, F32)
            for n in range(N_BUCKETS):
                acc = jnp.where(bucket == n, relb_ref[n, h], acc)
            bias_scr[h] = acc

    kk = jnp.concatenate([kvp_ref[:, 0:KV_W], kvc_ref[:, 0:KV_W]], axis=0).astype(F32)
    vv = jnp.concatenate([kvp_ref[:, KV_W:], kvc_ref[:, KV_W:]], axis=0).astype(F32)
    low = lax.broadcasted_iota(jnp.int32, kk.shape, 1) < HEAD_DIM

    def placed(t):
        r = pltpu.roll(t, HEAD_DIM, 1)
        return [[jnp.where(low, t, 0.0).astype(BF16), jnp.where(low, 0.0, r).astype(BF16)],
                [jnp.where(low, r, 0.0).astype(BF16), jnp.where(low, 0.0, t).astype(BF16)]]

    k_var, v_var = placed(kk), placed(vv)
    col = lax.broadcasted_iota(jnp.int32, (BLOCK, 2 * BLOCK), 1)
    key_exists = jnp.logical_or(col >= BLOCK, pl.program_id(1) > 0)

    for pair in range(N_Q_HEADS // 2):
        kvh = (2 * pair) // (N_Q_HEADS // N_KV_HEADS)
        qp = q_ref[:, pair * LANES:(pair + 1) * LANES]
        out = jnp.zeros((BLOCK, LANES), F32)
        for half in range(2):
            h = 2 * pair + half
            s = lax.dot_general(qp, k_var[kvh][half], NT_DIMS, preferred_element_type=F32)
            s = s * (HEAD_DIM ** -0.5) + bias_scr[h]
            s = jnp.where(key_exists, s, NEG_INF)
            sink = sink_ref[h]
            m = jnp.maximum(jnp.max(s, axis=-1, keepdims=True), sink)
            p = jnp.exp(s - m)
            p = p / (jnp.sum(p, axis=-1, keepdims=True) + jnp.exp(sink - m))
            out = out + _dot(p.astype(BF16), v_var[kvh][half])
        o_ref[:, pair * LANES:(pair + 1) * LANES] = out.astype(o_ref.dtype)


def _swa(q, kv, rel_bias, sinks, batch):
    t = q.shape[0]
    nb = t // batch // BLOCK
    bucket = jnp.asarray(_bucket_table())
    smem = pl.BlockSpec(memory_space=pltpu.SMEM)
    return pl.pallas_call(
        _swa_kernel,
        grid=(batch, nb),
        in_specs=[smem, smem, _resident((BLOCK, 2 * BLOCK)),
                  pl.BlockSpec((BLOCK, Q_W), lambda b, i: (b * nb + i, 0)),
                  pl.BlockSpec((BLOCK, 2 * KV_W), lambda b, i: (b * nb + jnp.maximum(i - 1, 0), 0)),
                  pl.BlockSpec((BLOCK, 2 * KV_W), lambda b, i: (b * nb + i, 0))],
        out_specs=pl.BlockSpec((BLOCK, Q_W), lambda b, i: (b * nb + i, 0)),
        out_shape=jax.ShapeDtypeStruct((t, Q_W), BF16),
        scratch_shapes=[pltpu.VMEM((N_Q_HEADS, BLOCK, 2 * BLOCK), F32)],
        compiler_params=_params(2),
        name="swa",
    )(rel_bias, sinks, bucket, q, kv, kv)


def _xattn_kernel(xq_ref, mk_ref, mv_ref, o_ref):
    for h in range(N_X_HEADS):
        sl = slice(h * X_HEAD_DIM, (h + 1) * X_HEAD_DIM)
        s = lax.dot_general(xq_ref[:, sl], mk_ref[:, sl], NT_DIMS, preferred_element_type=F32)
        s = s * (X_HEAD_DIM ** -0.5)
        p = jnp.exp(s - jnp.max(s, axis=-1, keepdims=True))
        p = p / jnp.sum(p, axis=-1, keepdims=True)
        o_ref[:, sl] = _dot(p.astype(BF16), mv_ref[:, sl]).astype(o_ref.dtype)


def _xattn(xq, mem_kv, batch, tm=512):
    t = xq.shape[0]
    nt = t // batch // tm
    return pl.pallas_call(
        _xattn_kernel,
        grid=(batch, nt),
        in_specs=[pl.BlockSpec((tm, XQ_W), lambda b, i: (b * nt + i, 0)),
                  pl.BlockSpec((MEM_LEN, XQ_W), lambda b, i: (b, 0)),
                  pl.BlockSpec((MEM_LEN, XQ_W), lambda b, i: (b, 1))],
        out_specs=pl.BlockSpec((tm, XQ_W), lambda b, i: (b * nt + i, 0)),
        out_shape=jax.ShapeDtypeStruct((t, XQ_W), BF16),
        compiler_params=_params(2),
        name="xattn",
    )(xq, mem_kv, mem_kv)


def _out_proj_kernel(x_ref, a_ref, c_ref, wa_ref, wc_ref, o_ref):
    o_ref[...] = x_ref[...] + _dot(a_ref[...], wa_ref[...]) + _dot(c_ref[...], wc_ref[...])


def _out_proj(x, a, c, wa, wc, tm=512):
    t, d = x.shape
    return pl.pallas_call(
        _out_proj_kernel,
        grid=(t // tm,),
        in_specs=[pl.BlockSpec((tm, d), lambda i: (i, 0)),
                  pl.BlockSpec((tm, a.shape[1]), lambda i: (i, 0)),
                  pl.BlockSpec((tm, c.shape[1]), lambda i: (i, 0)),
                  _resident(wa.shape), _resident(wc.shape)],
        out_specs=pl.BlockSpec((tm, d), lambda i: (i, 0)),
        out_shape=jax.ShapeDtypeStruct((t, d), F32),
        compiler_params=_params(1),
        name="out_proj",
    )(x, a, c, wa, wc)


FF_CHUNK = 256


def _ffn_kernel(x_ref, g_ref, wg_ref, wu_ref, wd_ref, o_ref, act_scr):
    x = x_ref[...]
    h = _rms(x, g_ref[...]).astype(BF16)
    for c0 in range(0, D_FF, FF_CHUNK):
        sl = slice(c0, c0 + FF_CHUNK)
        act_scr[:, sl] = (_silu(_dot(h, wg_ref[:, sl])) * _dot(h, wu_ref[:, sl])).astype(BF16)
    o_ref[...] = x + _dot(act_scr[...], wd_ref[...])


def _ffn(x, g, wg, wu, wd, tm=512):
    t, d = x.shape
    return pl.pallas_call(
        _ffn_kernel,
        grid=(t // tm,),
        in_specs=[pl.BlockSpec((tm, d), lambda i: (i, 0)), _resident((1, d)),
                  _resident(wg.shape), _resident(wu.shape), _resident(wd.shape)],
        out_specs=pl.BlockSpec((tm, d), lambda i: (i, 0)),
        out_shape=jax.ShapeDtypeStruct((t, d), F32),
        scratch_shapes=[pltpu.VMEM((tm, D_FF), BF16)],
        compiler_params=_params(1),
        name="ffn",
    )(x, g.reshape(1, d), wg, wu, wd)


def _router_kernel(x_ref, g_ref, wr_ref, comb_ref):
    h = _rms(x_ref[...], g_ref[...])
    logits = jnp.dot(h, wr_ref[...], preferred_element_type=F32, precision=lax.Precision.HIGHEST)
    lane = lax.broadcasted_iota(jnp.int32, logits.shape, 1)
    lg = jnp.where(lane < N_EXPERTS, logits, NEG_INF)
    m1 = jnp.max(lg, axis=-1, keepdims=True)
    i1 = jnp.min(jnp.where(lg == m1, lane, LANES), axis=-1, keepdims=True)
    lg2 = jnp.where(lane == i1, NEG_INF, lg)
    m2 = jnp.max(lg2, axis=-1, keepdims=True)
    i2 = jnp.min(jnp.where(lg2 == m2, lane, LANES), axis=-1, keepdims=True)
    e = jnp.exp(m2 - m1)
    comb_ref[...] = jnp.where(lane == i1, 1.0 / (1.0 + e), 0.0) + jnp.where(lane == i2, e / (1.0 + e), 0.0)


def _router(x, g, w_router, tm=512):
    t, d = x.shape
    wr = jnp.zeros((d, LANES), F32).at[:, :N_EXPERTS].set(w_router)
    return pl.pallas_call(
        _router_kernel,
        grid=(t // tm,),
        in_specs=[pl.BlockSpec((tm, d), lambda i: (i, 0)), _resident((1, d)), _resident((d, LANES))],
        out_specs=pl.BlockSpec((tm, LANES), lambda i: (i, 0)),
        out_shape=jax.ShapeDtypeStruct((t, LANES), F32),
        compiler_params=_params(1),
        name="router",
    )(x, g.reshape(1, d), wr)


MOE_FF_CHUNK = D_FF // 2


def _moe_kernel(x_ref, g_ref, comb_ref, wg_ref, wu_ref, wd_ref, o_ref, h_scr, acc_scr):
    e, c = pl.program_id(1), pl.program_id(2)

    @pl.when((e == 0) & (c == 0))
    def _():
        h_scr[...] = _rms(x_ref[...], g_ref[...]).astype(BF16)
        acc_scr[...] = jnp.zeros_like(acc_scr)

    h = h_scr[...]
    comb = comb_ref[...]
    lane = lax.broadcasted_iota(jnp.int32, comb.shape, 1)
    gate = jnp.sum(jnp.where(lane == e, comb, 0.0), axis=-1, keepdims=True)
    act = _silu(_dot(h, wg_ref[0])) * _dot(h, wu_ref[0]) * gate
    acc_scr[...] += _dot(act.astype(BF16), wd_ref[0])

    @pl.when((e == N_EXPERTS - 1) & (c == pl.num_programs(2) - 1))
    def _():
        o_ref[...] = x_ref[...] + acc_scr[...]


def _moe(x, g, comb, wg, wu, wd, tm=512):
    t, d = x.shape
    nc = D_FF // MOE_FF_CHUNK
    return pl.pallas_call(
        _moe_kernel,
        grid=(t // tm, N_EXPERTS, nc),
        in_specs=[pl.BlockSpec((tm, d), lambda i, e, c: (i, 0)),
                  pl.BlockSpec((1, d), lambda i, e, c: (0, 0)),
                  pl.BlockSpec((tm, LANES), lambda i, e, c: (i, 0)),
                  pl.BlockSpec((1, d, MOE_FF_CHUNK), lambda i, e, c: (e, 0, c)),
                  pl.BlockSpec((1, d, MOE_FF_CHUNK), lambda i, e, c: (e, 0, c)),
                  pl.BlockSpec((1, MOE_FF_CHUNK, d), lambda i, e, c: (e, c, 0))],
        out_specs=pl.BlockSpec((tm, d), lambda i, e, c: (i, 0)),
        out_shape=jax.ShapeDtypeStruct((t, d), F32),
        scratch_shapes=[pltpu.VMEM((tm, d), BF16), pltpu.VMEM((tm, d), F32)],
        compiler_params=_params(3),
        name="moe",
    )(x, g.reshape(1, d), comb, wg, wu, wd)


CONV_TAIL = 8


def _split_bf16(v, parts):
    out, r = [], v
    for _ in range(parts):
        p = r.astype(BF16)
        out.append(p)
        r = r - p.astype(F32)
    return out


def _ssd_kernel(xbc_ref, z_ref, dt_ref, cw_ref, cb_ref, dtb_ref, alog_ref, dsk_ref, gn_ref, exp_ref,
                o_ref, state_scr, tail_scr, ext_scr, y_scr):
    @pl.when(pl.program_id(1) == 0)
    def _():
        state_scr[...] = jnp.zeros_like(state_scr)
        tail_scr[...] = jnp.zeros_like(tail_scr)

    u = xbc_ref[...].astype(F32)
    ext_scr[0:CONV_TAIL, :] = tail_scr[...]
    ext_scr[CONV_TAIL:, :] = u
    tail_scr[...] = u[CHUNK - CONV_TAIL:, :]

    def conv_silu(c0, c1):
        acc = cb_ref[:, c0:c1]
        for k in range(CONV_WIDTH):
            r0 = CONV_TAIL - (CONV_WIDTH - 1) + k
            acc = acc + cw_ref[k:k + 1, c0:c1] * ext_scr[r0:r0 + CHUNK, c0:c1]
        return _silu(acc)

    dt = dt_ref[...] + dtb_ref[...]
    dt = jnp.maximum(dt, 0.0) + jnp.log(1.0 + jnp.exp(-jnp.abs(dt)))
    da = dt * -jnp.exp(alog_ref[...])
    row = lax.broadcasted_iota(jnp.int32, (CHUNK, CHUNK), 0)
    col = lax.broadcasted_iota(jnp.int32, (CHUNK, CHUNK), 1)
    causal = col <= row
    cs = _dot(causal.astype(BF16), jnp.concatenate(_split_bf16(da, 3), axis=1))
    acum = cs[:, 0:LANES] + cs[:, LANES:2 * LANES] + cs[:, 2 * LANES:]
    acum_t = acum.T
    ea = jnp.exp(acum)
    dtte = dt * jnp.exp(acum[CHUNK - 1:CHUNK, :] - acum)

    def expand(v, g):
        return _dot(jnp.concatenate(_split_bf16(v, 2), axis=1), exp_ref[:, g * GROUP_W:(g + 1) * GROUP_W])

    low = lax.broadcasted_iota(jnp.int32, (CHUNK, LANES), 1) < SSM_HEAD_DIM
    for g in range(N_SSM_GROUPS):
        gsl = slice(g * GROUP_W, (g + 1) * GROUP_W)
        xg = conv_silu(g * GROUP_W, (g + 1) * GROUP_W)
        bg = conv_silu(D_INNER + g * D_STATE, D_INNER + (g + 1) * D_STATE)
        cg = conv_silu(D_INNER + BC_W + g * D_STATE, D_INNER + BC_W + (g + 1) * D_STATE).astype(BF16)
        ea_x = expand(ea, g)
        xdt = (xg * expand(dt, g)).astype(BF16)
        cb = lax.dot_general(cg, bg.astype(BF16), NT_DIMS, preferred_element_type=F32)
        state = state_scr[g]
        y = _dot(cg, state.astype(BF16)) * ea_x + xg * dsk_ref[:, gsl]
        diag = []
        for pair in range(HEADS_PER_GROUP // 2):
            xp = xdt[:, pair * LANES:(pair + 1) * LANES]
            acc = jnp.zeros((CHUNK, LANES), F32)
            for half in range(2):
                h = g * HEADS_PER_GROUP + 2 * pair + half
                seg = acum[:, h:h + 1] - acum_t[h:h + 1, :]
                m = (cb * jnp.exp(jnp.where(causal, seg, NEG_INF))).astype(BF16)
                xm = jnp.where(low if half == 0 else jnp.logical_not(low), xp, jnp.zeros_like(xp))
                acc = acc + _dot(m, xm)
            diag.append(acc)
        y_scr[:, gsl] = y + jnp.concatenate(diag, axis=1)
        w = (xg * expand(dtte, g)).astype(BF16)
        state_scr[g] = state * ea_x[CHUNK - 1:CHUNK, :] + _dot(bg.T.astype(BF16), w)

    z = z_ref[...].astype(F32)
    o_ref[...] = _rms(y_scr[...] * _silu(z), gn_ref[...]).astype(o_ref.dtype)


def _ssd(xbc, z, dt_raw, conv_w, conv_b, dt_bias, a_log, d_skip, g_norm, batch):
    t = xbc.shape[0]
    nc = t // batch // CHUNK

    def lane_pad(v):
        return jnp.zeros((1, LANES), F32).at[0, :N_SSM_HEADS].set(v)

    expand = np.zeros((LANES, D_INNER), np.float32)
    for h in range(N_SSM_HEADS):
        expand[h, h * SSM_HEAD_DIM:(h + 1) * SSM_HEAD_DIM] = 1.0
    expand2 = jnp.asarray(np.concatenate([expand, expand], axis=0), BF16)
    tok = lambda b, c: (b * nc + c, 0)
    return pl.pallas_call(
        _ssd_kernel,
        grid=(batch, nc),
        in_specs=[pl.BlockSpec((CHUNK, CONV_CH), tok), pl.BlockSpec((CHUNK, D_INNER), tok),
                  pl.BlockSpec((CHUNK, LANES), tok),
                  _resident((CONV_WIDTH, CONV_CH)), _resident((1, CONV_CH)),
                  _resident((1, LANES)), _resident((1, LANES)),
                  _resident((1, D_INNER)), _resident((1, D_INNER)), _resident((2 * LANES, D_INNER))],
        out_specs=pl.BlockSpec((CHUNK, D_INNER), tok),
        out_shape=jax.ShapeDtypeStruct((t, D_INNER), BF16),
        scratch_shapes=[pltpu.VMEM((N_SSM_GROUPS, D_STATE, GROUP_W), F32),
                        pltpu.VMEM((CONV_TAIL, CONV_CH), F32),
                        pltpu.VMEM((CONV_TAIL + CHUNK, CONV_CH), F32),
                        pltpu.VMEM((CHUNK, D_INNER), F32)],
        compiler_params=_params(2),
        name="ssd",
    )(xbc, z, dt_raw, conv_w, conv_b.reshape(1, CONV_CH), lane_pad(dt_bias), lane_pad(a_log),
      jnp.repeat(d_skip, SSM_HEAD_DIM).reshape(1, D_INNER), g_norm.reshape(1, D_INNER), expand2)


def _final_norm_kernel(x_ref, g_ref, o_ref):
    o_ref[...] = _rms(x_ref[...], g_ref[...])


def _final_norm(x, g, tm=1024):
    t, d = x.shape
    return pl.pallas_call(
        _final_norm_kernel,
        grid=(t // tm,),
        in_specs=[pl.BlockSpec((tm, d), lambda i: (i, 0)), _resident((1, d))],
        out_specs=pl.BlockSpec((tm, d), lambda i: (i, 0)),
        out_shape=jax.ShapeDtypeStruct((t, d), F32),
        compiler_params=_params(1),
        name="final_norm",
    )(x, g.reshape(1, d))


def kernel(x, mem, g_mix, g_ffn, g_mem, w_mem_kv, rel_bias, swa_w_in, swa_sinks, swa_w_out, ssm_w_in, ssm_conv_w, ssm_conv_b, ssm_dt_bias, ssm_A_log, ssm_D, ssm_g_norm, ssm_w_out, ffn_w_gate, ffn_w_up, ffn_w_down, moe_w_router, moe_w_gate, moe_w_up, moe_w_down, g_final):
    batch, seq, d = x.shape
    xf = x.reshape(batch * seq, d)
    memf = mem.reshape(batch * MEM_LEN, d)
    for i in range(DEPTH):
        j = i // 2
        (mem_kv,) = _norm_proj(memf, g_mem[i], [w_mem_kv[i].astype(BF16)], [BF16])
        if i % 2 == 0:
            w_in = swa_w_in[j].astype(BF16)
            q, kv, xq = _norm_proj(
                xf, g_mix[i], [w_in[:, :Q_W], w_in[:, Q_W:Q_W + 2 * KV_W], w_in[:, Q_W + 2 * KV_W:]],
                [BF16, BF16, BF16])
            attn = _swa(q, kv, rel_bias, swa_sinks[j], batch)
            cross = _xattn(xq, mem_kv, batch)
            w_out = swa_w_out[j].astype(BF16)
            xf = _out_proj(xf, attn, cross, w_out[:Q_W], w_out[Q_W:])
            xf = _ffn(xf, g_ffn[i], ffn_w_gate[j].astype(BF16), ffn_w_up[j].astype(BF16),
                      ffn_w_down[j].astype(BF16))
        else:
            w_in = ssm_w_in[j].astype(BF16)
            o_dt = D_INNER + CONV_CH
            w_dt = jnp.zeros((d, LANES), BF16).at[:, :N_SSM_HEADS].set(w_in[:, o_dt:o_dt + N_SSM_HEADS])
            z, xbc, dt_raw, xq = _norm_proj(
                xf, g_mix[i], [w_in[:, :D_INNER], w_in[:, D_INNER:o_dt], w_dt, w_in[:, o_dt + N_SSM_HEADS:]],
                [BF16, BF16, F32, BF16])
            y = _ssd(xbc, z, dt_raw, ssm_conv_w[j], ssm_conv_b[j], ssm_dt_bias[j], ssm_A_log[j],
                     ssm_D[j], ssm_g_norm[j], batch)
            cross = _xattn(xq, mem_kv, batch)
            w_out = ssm_w_out[j].astype(BF16)
            xf = _out_proj(xf, y, cross, w_out[:D_INNER], w_out[D_INNER:])
            comb = _router(xf, g_ffn[i], moe_w_router[j])
            xf = _moe(xf, g_ffn[i], comb, moe_w_gate[j].astype(BF16), moe_w_up[j].astype(BF16),
                      moe_w_down[j].astype(BF16))
    return _final_norm(xf, g_final).reshape(batch, seq, d)
```

```python
import functools
import math

import numpy as np
import jax
import jax.numpy as jnp
from jax import lax
from jax.experimental import pallas as pl
from jax.experimental.pallas import tpu as pltpu

F32 = jnp.float32
BF16 = jnp.bfloat16

D_MODEL = 1024
DEPTH = 4
MEM_LEN = 256
EPS = 1e-6
N_Q_HEADS = 16
N_KV_HEADS = 2
HEAD_DIM = 64
BLOCK = 128
N_BUCKETS = 32
MAX_DISTANCE = 128
N_X_HEADS = 4
X_HEAD_DIM = 256
D_INNER = 2048
SSM_HEAD_DIM = 64
N_SSM_HEADS = 32
N_SSM_GROUPS = 4
HEADS_PER_GROUP = 8
D_STATE = 128
CONV_WIDTH = 4
CHUNK = 128
D_FF = 2816
N_EXPERTS = 8
Q_W = N_Q_HEADS * HEAD_DIM
KV_W = N_KV_HEADS * HEAD_DIM
XQ_W = N_X_HEADS * X_HEAD_DIM
BC_W = N_SSM_GROUPS * D_STATE
CONV_CH = D_INNER + 2 * BC_W
GROUP_W = HEADS_PER_GROUP * SSM_HEAD_DIM

LANES = 128
VMEM_LIMIT = 56 << 20
NEG_INF = float("-inf")
NT_DIMS = (((1,), (1,)), ((), ()))


def _params(n_axes, vmem=VMEM_LIMIT):
    return pltpu.CompilerParams(dimension_semantics=("arbitrary",) * n_axes, vmem_limit_bytes=vmem)


def _resident(shape):
    nd = len(shape)
    return pl.BlockSpec(shape, lambda *_: (0,) * nd, pipeline_mode=pl.Buffered(1))


def _dot(a, b):
    return jnp.dot(a, b, preferred_element_type=F32)


def _rms(x, g):
    return x * lax.rsqrt(jnp.mean(x * x, axis=-1, keepdims=True) + EPS) * g


def _silu(v):
    return v / (1.0 + jnp.exp(-v))


def _norm_proj_kernel(x_ref, g_ref, *refs, n_out, col_chunk):
    w_refs, o_refs = refs[:n_out], refs[n_out:]
    h = _rms(x_ref[...], g_ref[...]).astype(BF16)
    for w_ref, o_ref in zip(w_refs, o_refs):
        n = w_ref.shape[1]
        for c0 in range(0, n, col_chunk):
            c1 = min(c0 + col_chunk, n)
            o_ref[:, c0:c1] = _dot(h, w_ref[:, c0:c1]).astype(o_ref.dtype)


def _norm_proj(x, g, ws, out_dtypes, tm=512):
    t, d = x.shape
    n_out = len(ws)
    return pl.pallas_call(
        functools.partial(_norm_proj_kernel, n_out=n_out, col_chunk=512),
        grid=(t // tm,),
        in_specs=[pl.BlockSpec((tm, d), lambda i: (i, 0)), _resident((1, d))]
        + [_resident(w.shape) for w in ws],
        out_specs=[pl.BlockSpec((tm, w.shape[1]), lambda i: (i, 0)) for w in ws],
        out_shape=[jax.ShapeDtypeStruct((t, w.shape[1]), dt) for w, dt in zip(ws, out_dtypes)],
        compiler_params=_params(1),
        name="norm_proj",
    )(x, g.reshape(1, d), *ws)


def _bucket_table():
    qi = np.arange(BLOCK)[:, None]
    kj = np.arange(2 * BLOCK)[None, :]
    dist = BLOCK + qi - kj
    max_exact = N_BUCKETS // 2
    d = np.maximum(dist, 0)
    df = np.maximum(d, 1).astype(np.float32)
    far = max_exact + (
        np.log(df / np.float32(max_exact)) / np.float32(math.log(MAX_DISTANCE / max_exact))
        * np.float32(N_BUCKETS - max_exact)
    ).astype(np.int32)
    bucket = np.where(d < max_exact, d, np.minimum(far, N_BUCKETS - 1))
    band = (dist >= 0) & (dist < BLOCK)
    return np.where(band, bucket, -1).astype(np.int32)


def _swa_kernel(relb_ref, sink_ref, bucket_ref, q_ref, kvp_ref, kvc_ref, o_ref, bias_scr):
    first = (pl.program_id(0) == 0) & (pl.program_id(1) == 0)

    @pl.when(first)
    def _():
        bucket = bucket_ref[...]
        for h in range(N_Q_HEADS):
            acc = jnp.full((BLOCK, 2 * BLOCK), NEG_INF, F32)
            for n in range(N_BUCKETS):
                acc = jnp.where(bucket == n, relb_ref[n, h], acc)
            bias_scr[h] = acc

    kk = jnp.concatenate([kvp_ref[:, 0:KV_W], kvc_ref[:, 0:KV_W]], axis=0).astype(F32)
    vv = jnp.concatenate([kvp_ref[:, KV_W:], kvc_ref[:, KV_W:]], axis=0).astype(F32)
    low = lax.broadcasted_iota(jnp.int32, kk.shape, 1) < HEAD_DIM

    def placed(t):
        r = pltpu.roll(t, HEAD_DIM, 1)
        return [[jnp.where(low, t, 0.0).astype(BF16), jnp.where(low, 0.0, r).astype(BF16)],
                [jnp.where(low, r, 0.0).astype(BF16), jnp.where(low, 0.0, t).astype(BF16)]]

    k_var, v_var = placed(kk), placed(vv)
    col = lax.broadcasted_iota(jnp.int32, (BLOCK, 2 * BLOCK), 1)
    key_exists = jnp.logical_or(col >= BLOCK, pl.program_id(1) > 0)

    for pair in range(N_Q_HEADS // 2):
        kvh = (2 * pair) // (N_Q_HEADS // N_KV_HEADS)
        qp = q_ref[:, pair * LANES:(pair + 1) * LANES]
        out = jnp.zeros((BLOCK, LANES), F32)
        for half in range(2):
            h = 2 * pair + half
            s = lax.dot_general(qp, k_var[kvh][half], NT_DIMS, preferred_element_type=F32)
            s = s * (HEAD_DIM ** -0.5) + bias_scr[h]
            s = jnp.where(key_exists, s, NEG_INF)
            sink = sink_ref[h]
            m = jnp.maximum(jnp.max(s, axis=-1, keepdims=True), sink)
            p = jnp.exp(s - m)
            p = p / (jnp.sum(p, axis=-1, keepdims=True) + jnp.exp(sink - m))
            out = out + _dot(p.astype(BF16), v_var[kvh][half])
        o_ref[:, pair * LANES:(pair + 1) * LANES] = out.astype(o_ref.dtype)


def _swa(q, kv, rel_bias, sinks, batch):
    t = q.shape[0]
    nb = t // batch // BLOCK
    bucket = jnp.asarray(_bucket_table())
    smem = pl.BlockSpec(memory_space=pltpu.SMEM)
    return pl.pallas_call(
        _swa_kernel,
        grid=(batch, nb),
        in_specs=[smem, smem, _resident((BLOCK, 2 * BLOCK)),
                  pl.BlockSpec((BLOCK, Q_W), lambda b, i: (b * nb + i, 0)),
                  pl.BlockSpec((BLOCK, 2 * KV_W), lambda b, i: (b * nb + jnp.maximum(i - 1, 0), 0)),
                  pl.BlockSpec((BLOCK, 2 * KV_W), lambda b, i: (b * nb + i, 0))],
        out_specs=pl.BlockSpec((BLOCK, Q_W), lambda b, i: (b * nb + i, 0)),
        out_shape=jax.ShapeDtypeStruct((t, Q_W), BF16),
        scratch_shapes=[pltpu.VMEM((N_Q_HEADS, BLOCK, 2 * BLOCK), F32)],
        compiler_params=_params(2),
        name="swa",
    )(rel_bias, sinks, bucket, q, kv, kv)


def _xattn_kernel(xq_ref, mk_ref, mv_ref, o_ref):
    for h in range(N_X_HEADS):
        sl = slice(h * X_HEAD_DIM, (h + 1) * X_HEAD_DIM)
        s = lax.dot_general(xq_ref[:, sl], mk_ref[:, sl], NT_DIMS, preferred_element_type=F32)
        s = s * (X_HEAD_DIM ** -0.5)
        p = jnp.exp(s - jnp.max(s, axis=-1, keepdims=True))
        p = p / jnp.sum(p, axis=-1, keepdims=True)
        o_ref[:, sl] = _dot(p.astype(BF16), mv_ref[:, sl]).astype(o_ref.dtype)


def _xattn(xq, mem_kv, batch, tm=512):
    t = xq.shape[0]
    nt = t // batch // tm
    return pl.pallas_call(
        _xattn_kernel,
        grid=(batch, nt),
        in_specs=[pl.BlockSpec((tm, XQ_W), lambda b, i: (b * nt + i, 0)),
                  pl.BlockSpec((MEM_LEN, XQ_W), lambda b, i: (b, 0)),
                  pl.BlockSpec((MEM_LEN, XQ_W), lambda b, i: (b, 1))],
        out_specs=pl.BlockSpec((tm, XQ_W), lambda b, i: (b * nt + i, 0)),
        out_shape=jax.ShapeDtypeStruct((t, XQ_W), BF16),
        compiler_params=_params(2),
        name="xattn",
    )(xq, mem_kv, mem_kv)


def _out_proj_kernel(x_ref, a_ref, c_ref, wa_ref, wc_ref, o_ref):
    o_ref[...] = x_ref[...] + _dot(a_ref[...], wa_ref[...]) + _dot(c_ref[...], wc_ref[...])


def _out_proj(x, a, c, wa, wc, tm=512):
    t, d = x.shape
    return pl.pallas_call(
        _out_proj_kernel,
        grid=(t // tm,),
        in_specs=[pl.BlockSpec((tm, d), lambda i: (i, 0)),
                  pl.BlockSpec((tm, a.shape[1]), lambda i: (i, 0)),
                  pl.BlockSpec((tm, c.shape[1]), lambda i: (i, 0)),
                  _resident(wa.shape), _resident(wc.shape)],
        out_specs=pl.BlockSpec((tm, d), lambda i: (i, 0)),
        out_shape=jax.ShapeDtypeStruct((t, d), F32),
        compiler_params=_params(1),
        name="out_proj",
    )(x, a, c, wa, wc)


FF_CHUNK = 256


def _ffn_kernel(x_ref, g_ref, wg_ref, wu_ref, wd_ref, o_ref, act_scr):
    x = x_ref[...]
    h = _rms(x, g_ref[...]).astype(BF16)
    for c0 in range(0, D_FF, FF_CHUNK):
        sl = slice(c0, c0 + FF_CHUNK)
        act_scr[:, sl] = (_silu(_dot(h, wg_ref[:, sl])) * _dot(h, wu_ref[:, sl])).astype(BF16)
    o_ref[...] = x + _dot(act_scr[...], wd_ref[...])


def _ffn(x, g, wg, wu, wd, tm=512):
    t, d = x.shape
    return pl.pallas_call(
        _ffn_kernel,
        grid=(t // tm,),
        in_specs=[pl.BlockSpec((tm, d), lambda i: (i, 0)), _resident((1, d)),
                  _resident(wg.shape), _resident(wu.shape), _resident(wd.shape)],
        out_specs=pl.BlockSpec((tm, d), lambda i: (i, 0)),
        out_shape=jax.ShapeDtypeStruct((t, d), F32),
        scratch_shapes=[pltpu.VMEM((tm, D_FF), BF16)],
        compiler_params=_params(1),
        name="ffn",
    )(x, g.reshape(1, d), wg, wu, wd)


TOP_K = 2
HI16 = 0xFFFF0000
META_EXPERT, META_POS, META_GATE = 0, 2, 4


def _pack_bf16_pairs(h):
    n = h.shape[1] // 2
    u = pltpu.bitcast(h.astype(BF16).astype(F32), jnp.uint32)
    return (u[:, :n] >> 16) | (u[:, n:] & jnp.uint32(HI16))


def _unpack_bf16_pairs(w):
    lo = pltpu.bitcast(w << 16, F32).astype(BF16)
    hi = pltpu.bitcast(w & jnp.uint32(HI16), F32).astype(BF16)
    return jnp.concatenate([lo, hi], axis=1)


def _router_kernel(x_ref, g_ref, wr_ref, meta_ref, cnt_ref, hpk_ref, run_scr):
    @pl.when(pl.program_id(0) == 0)
    def _():
        run_scr[...] = jnp.zeros_like(run_scr)

    h = _rms(x_ref[...], g_ref[...])
    hpk_ref[...] = _pack_bf16_pairs(h)
    logits = jnp.dot(h, wr_ref[...], preferred_element_type=F32, precision=lax.Precision.HIGHEST)
    lane = lax.broadcasted_iota(jnp.int32, logits.shape, 1)
    lg = jnp.where(lane < N_EXPERTS, logits, NEG_INF)
    m1 = jnp.max(lg, axis=-1, keepdims=True)
    i1 = jnp.min(jnp.where(lg == m1, lane, LANES), axis=-1, keepdims=True)
    lg2 = jnp.where(lane == i1, NEG_INF, lg)
    m2 = jnp.max(lg2, axis=-1, keepdims=True)
    i2 = jnp.min(jnp.where(lg2 == m2, lane, LANES), axis=-1, keepdims=True)
    e = jnp.exp(m2 - m1)
    g1, g2 = 1.0 / (1.0 + e), e / (1.0 + e)

    tm = h.shape[0]
    picked = jnp.logical_or(lane == i1, lane == i2)
    onehot = jnp.where(picked, 1.0, 0.0)
    earlier = lax.broadcasted_iota(jnp.int32, (tm, tm), 1) < lax.broadcasted_iota(jnp.int32, (tm, tm), 0)
    before = _dot(earlier.astype(BF16), onehot.astype(BF16)) + run_scr[...]
    pos1 = jnp.sum(jnp.where(lane == i1, before, 0.0), axis=-1, keepdims=True)
    pos2 = jnp.sum(jnp.where(lane == i2, before, 0.0), axis=-1, keepdims=True)
    run_scr[...] += jnp.sum(onehot, axis=0, keepdims=True)
    cnt_ref[...] = run_scr[...]

    def put(k, v):
        return jnp.where(lane == k, v, 0.0)

    meta_ref[...] = (put(META_EXPERT, i1.astype(F32)) + put(META_EXPERT + 1, i2.astype(F32))
                     + put(META_POS, pos1) + put(META_POS + 1, pos2)
                     + put(META_GATE, g1) + put(META_GATE + 1, g2))


def _router(x, g, w_router, tm=512):
    t, d = x.shape
    wr = jnp.zeros((d, LANES), F32).at[:, :N_EXPERTS].set(w_router)
    return pl.pallas_call(
        _router_kernel,
        grid=(t // tm,),
        in_specs=[pl.BlockSpec((tm, d), lambda i: (i, 0)), _resident((1, d)), _resident((d, LANES))],
        out_specs=[pl.BlockSpec((tm, LANES), lambda i: (i, 0)),
                   pl.BlockSpec((1, LANES), lambda i: (0, 0)),
                   pl.BlockSpec((tm, d // 2), lambda i: (i, 0))],
        out_shape=[jax.ShapeDtypeStruct((t, LANES), F32), jax.ShapeDtypeStruct((1, LANES), F32),
                   jax.ShapeDtypeStruct((t, d // 2), jnp.uint32)],
        scratch_shapes=[pltpu.VMEM((1, LANES), F32)],
        compiler_params=_params(1),
        name="router",
    )(x, g.reshape(1, d), wr)


ROW_TILE = 512
TOK_TILE = 512


def _row_copy_all_wait(src_like, dst_like, sem):
    pltpu.make_async_copy(src_like, dst_like, sem).wait()


def _dispatch_kernel(dest_hbm, hpk_ref, xs_in, xs_out, idx_smem, idx_sem, row_sem):
    del xs_in
    tm = hpk_ref.shape[0]
    idx_copy = pltpu.make_async_copy(dest_hbm.at[pl.program_id(0)], idx_smem, idx_sem)
    idx_copy.start()
    idx_copy.wait()

    def issue(t, carry):
        for k in range(TOP_K):
            row = idx_smem[TOP_K * t + k]
            pltpu.make_async_copy(hpk_ref.at[pl.ds(t, 1)], xs_out.at[pl.ds(row, 1)], row_sem).start()
        return carry

    lax.fori_loop(0, tm, issue, 0, unroll=8)
    for _ in range(TOP_K):
        _row_copy_all_wait(hpk_ref, xs_out.at[pl.ds(0, tm)], row_sem)


def _dispatch(dest_tiles, hpk, n_rows):
    t, w = hpk.shape
    tm = TOK_TILE
    any_spec = pl.BlockSpec(memory_space=pl.ANY)
    return pl.pallas_call(
        _dispatch_kernel,
        grid=(t // tm,),
        in_specs=[any_spec, pl.BlockSpec((tm, w), lambda i: (i, 0)), any_spec],
        out_specs=any_spec,
        out_shape=jax.ShapeDtypeStruct((n_rows, w), jnp.uint32),
        scratch_shapes=[pltpu.SMEM((TOP_K * tm,), jnp.int32), pltpu.SemaphoreType.DMA,
                        pltpu.SemaphoreType.DMA],
        input_output_aliases={2: 0},
        compiler_params=_params(1),
        name="dispatch",
    )(dest_tiles, hpk, jnp.zeros((n_rows, w), jnp.uint32))


def _experts_kernel(tile_expert_ref, n_used_ref, xs_ref, wg_ref, wu_ref, wd_ref, y_ref, act_scr):
    del tile_expert_ref

    @pl.when(pl.program_id(0) < n_used_ref[0])
    def _():
        h = _unpack_bf16_pairs(xs_ref[...])
        for c0 in range(0, D_FF, FF_CHUNK):
            sl = slice(c0, c0 + FF_CHUNK)
            act_scr[:, sl] = (_silu(_dot(h, wg_ref[0, :, sl])) * _dot(h, wu_ref[0, :, sl])).astype(BF16)
        y_ref[...] = _dot(act_scr[...], wd_ref[0])

    @pl.when(pl.program_id(0) >= n_used_ref[0])
    def _():
        y_ref[...] = jnp.zeros_like(y_ref)


def _experts(tile_expert, n_used, xs, wg, wu, wd):
    n_rows, w = xs.shape
    d = 2 * w
    tm = ROW_TILE
    wspec = lambda shape: pl.BlockSpec((1,) + shape, lambda i, te, nu: (te[i], 0, 0))
    return pl.pallas_call(
        _experts_kernel,
        grid_spec=pltpu.PrefetchScalarGridSpec(
            num_scalar_prefetch=2,
            grid=(n_rows // tm,),
            in_specs=[pl.BlockSpec((tm, w), lambda i, te, nu: (i, 0)),
                      wspec((d, D_FF)), wspec((d, D_FF)), wspec((D_FF, d))],
            out_specs=pl.BlockSpec((tm, d), lambda i, te, nu: (i, 0)),
            scratch_shapes=[pltpu.VMEM((tm, D_FF), BF16)]),
        out_shape=jax.ShapeDtypeStruct((n_rows, d), F32),
        compiler_params=_params(1),
        name="experts",
    )(tile_expert, n_used, xs, wg, wu, wd)


def _combine_kernel(dest_hbm, x_ref, meta_ref, y_hbm, o_ref, idx_smem, y_buf, idx_sem, row_sem):
    tm = x_ref.shape[0]
    idx_copy = pltpu.make_async_copy(dest_hbm.at[pl.program_id(0)], idx_smem, idx_sem)
    idx_copy.start()
    idx_copy.wait()

    def issue(t, carry):
        for k in range(TOP_K):
            row = idx_smem[TOP_K * t + k]
            pltpu.make_async_copy(y_hbm.at[pl.ds(row, 1)], y_buf.at[k, pl.ds(t, 1)], row_sem).start()
        return carry

    lax.fori_loop(0, tm, issue, 0, unroll=8)
    for k in range(TOP_K):
        _row_copy_all_wait(y_hbm.at[pl.ds(0, tm)], y_buf.at[k], row_sem)
    meta = meta_ref[...]
    out = x_ref[...]
    for k in range(TOP_K):
        out = out + meta[:, META_GATE + k:META_GATE + k + 1] * y_buf[k]
    o_ref[...] = out


def _combine(dest_tiles, x, meta, y):
    t, d = x.shape
    tm = TOK_TILE
    any_spec = pl.BlockSpec(memory_space=pl.ANY)
    return pl.pallas_call(
        _combine_kernel,
        grid=(t // tm,),
        in_specs=[any_spec, pl.BlockSpec((tm, d), lambda i: (i, 0)),
                  pl.BlockSpec((tm, LANES), lambda i: (i, 0)), any_spec],
        out_specs=pl.BlockSpec((tm, d), lambda i: (i, 0)),
        out_shape=jax.ShapeDtypeStruct((t, d), F32),
        scratch_shapes=[pltpu.SMEM((TOP_K * tm,), jnp.int32), pltpu.VMEM((TOP_K, tm, d), F32),
                        pltpu.SemaphoreType.DMA, pltpu.SemaphoreType.DMA],
        compiler_params=_params(1),
        name="combine",
    )(dest_tiles, x, meta, y)


def _moe(x, g, w_router, wg, wu, wd):
    t, d = x.shape
    meta, counts, hpk = _router(x, g, w_router)
    cnt = counts[0, :N_EXPERTS].astype(jnp.int32)
    padded = (cnt + ROW_TILE - 1) // ROW_TILE * ROW_TILE
    ends = jnp.cumsum(padded)
    starts = ends - padded
    expert = meta[:, META_EXPERT:META_EXPERT + TOP_K].astype(jnp.int32)
    pos = meta[:, META_POS:META_POS + TOP_K].astype(jnp.int32)
    dest = pos
    for e in range(N_EXPERTS):
        dest = dest + jnp.where(expert == e, starts[e], 0)
    dest_tiles = dest.reshape(t // TOK_TILE, TOP_K * TOK_TILE)
    n_rows = TOP_K * t + N_EXPERTS * ROW_TILE
    n_tiles = n_rows // ROW_TILE
    n_used = ends[-1] // ROW_TILE
    tile_start = jnp.minimum(jnp.arange(n_tiles), n_used - 1) * ROW_TILE
    tile_expert = jnp.sum(tile_start[:, None] >= ends[None, :], axis=1).astype(jnp.int32)
    xs = _dispatch(dest_tiles, hpk, n_rows)
    y = _experts(tile_expert, n_used.reshape(1).astype(jnp.int32), xs, wg, wu, wd)
    return _combine(dest_tiles, x, meta, y)


CONV_TAIL = 8


def _split_bf16(v, parts):
    out, r = [], v
    for _ in range(parts):
        p = r.astype(BF16)
        out.append(p)
        r = r - p.astype(F32)
    return out


def _ssd_kernel(xbc_ref, z_ref, dt_ref, cw_ref, cb_ref, dtb_ref, alog_ref, dsk_ref, gn_ref, exp_ref,
                o_ref, state_scr, tail_scr, ext_scr, y_scr):
    @pl.when(pl.program_id(1) == 0)
    def _():
        state_scr[...] = jnp.zeros_like(state_scr)
        tail_scr[...] = jnp.zeros_like(tail_scr)

    u = xbc_ref[...].astype(F32)
    ext_scr[0:CONV_TAIL, :] = tail_scr[...]
    ext_scr[CONV_TAIL:, :] = u
    tail_scr[...] = u[CHUNK - CONV_TAIL:, :]

    def conv_silu(c0, c1):
        acc = cb_ref[:, c0:c1]
        for k in range(CONV_WIDTH):
            r0 = CONV_TAIL - (CONV_WIDTH - 1) + k
            acc = acc + cw_ref[k:k + 1, c0:c1] * ext_scr[r0:r0 + CHUNK, c0:c1]
        return _silu(acc)

    dt = dt_ref[...] + dtb_ref[...]
    dt = jnp.maximum(dt, 0.0) + jnp.log(1.0 + jnp.exp(-jnp.abs(dt)))
    da = dt * -jnp.exp(alog_ref[...])
    row = lax.broadcasted_iota(jnp.int32, (CHUNK, CHUNK), 0)
    col = lax.broadcasted_iota(jnp.int32, (CHUNK, CHUNK), 1)
    causal = col <= row
    cs = _dot(causal.astype(BF16), jnp.concatenate(_split_bf16(da, 3), axis=1))
    acum = cs[:, 0:LANES] + cs[:, LANES:2 * LANES] + cs[:, 2 * LANES:]
    acum_t = acum.T
    ea = jnp.exp(acum)
    dtte = dt * jnp.exp(acum[CHUNK - 1:CHUNK, :] - acum)

    def expand(v, g):
        return _dot(jnp.concatenate(_split_bf16(v, 2), axis=1), exp_ref[:, g * GROUP_W:(g + 1) * GROUP_W])

    low = lax.broadcasted_iota(jnp.int32, (CHUNK, LANES), 1) < SSM_HEAD_DIM
    for g in range(N_SSM_GROUPS):
        gsl = slice(g * GROUP_W, (g + 1) * GROUP_W)
        xg = conv_silu(g * GROUP_W, (g + 1) * GROUP_W)
        bg = conv_silu(D_INNER + g * D_STATE, D_INNER + (g + 1) * D_STATE)
        cg = conv_silu(D_INNER + BC_W + g * D_STATE, D_INNER + BC_W + (g + 1) * D_STATE).astype(BF16)
        ea_x = expand(ea, g)
        xdt = (xg * expand(dt, g)).astype(BF16)
        cb = lax.dot_general(cg, bg.astype(BF16), NT_DIMS, preferred_element_type=F32)
        state = state_scr[g]
        y = _dot(cg, state.astype(BF16)) * ea_x + xg * dsk_ref[:, gsl]
        diag = []
        for pair in range(HEADS_PER_GROUP // 2):
            xp = xdt[:, pair * LANES:(pair + 1) * LANES]
            acc = jnp.zeros((CHUNK, LANES), F32)
            for half in range(2):
                h = g * HEADS_PER_GROUP + 2 * pair + half
                seg = acum[:, h:h + 1] - acum_t[h:h + 1, :]
                m = (cb * jnp.exp(jnp.where(causal, seg, NEG_INF))).astype(BF16)
                xm = jnp.where(low if half == 0 else jnp.logical_not(low), xp, jnp.zeros_like(xp))
                acc = acc + _dot(m, xm)
            diag.append(acc)
        y_scr[:, gsl] = y + jnp.concatenate(diag, axis=1)
        w = (xg * expand(dtte, g)).astype(BF16)
        state_scr[g] = state * ea_x[CHUNK - 1:CHUNK, :] + _dot(bg.T.astype(BF16), w)

    z = z_ref[...].astype(F32)
    o_ref[...] = _rms(y_scr[...] * _silu(z), gn_ref[...]).astype(o_ref.dtype)


def _ssd(xbc, z, dt_raw, conv_w, conv_b, dt_bias, a_log, d_skip, g_norm, batch):
    t = xbc.shape[0]
    nc = t // batch // CHUNK

    def lane_pad(v):
        return jnp.zeros((1, LANES), F32).at[0, :N_SSM_HEADS].set(v)

    expand = np.zeros((LANES, D_INNER), np.float32)
    for h in range(N_SSM_HEADS):
        expand[h, h * SSM_HEAD_DIM:(h + 1) * SSM_HEAD_DIM] = 1.0
    expand2 = jnp.asarray(np.concatenate([expand, expand], axis=0), BF16)
    tok = lambda b, c: (b * nc + c, 0)
    return pl.pallas_call(
        _ssd_kernel,
        grid=(batch, nc),
        in_specs=[pl.BlockSpec((CHUNK, CONV_CH), tok), pl.BlockSpec((CHUNK, D_INNER), tok),
                  pl.BlockSpec((CHUNK, LANES), tok),
                  _resident((CONV_WIDTH, CONV_CH)), _resident((1, CONV_CH)),
                  _resident((1, LANES)), _resident((1, LANES)),
                  _resident((1, D_INNER)), _resident((1, D_INNER)), _resident((2 * LANES, D_INNER))],
        out_specs=pl.BlockSpec((CHUNK, D_INNER), tok),
        out_shape=jax.ShapeDtypeStruct((t, D_INNER), BF16),
        scratch_shapes=[pltpu.VMEM((N_SSM_GROUPS, D_STATE, GROUP_W), F32),
                        pltpu.VMEM((CONV_TAIL, CONV_CH), F32),
                        pltpu.VMEM((CONV_TAIL + CHUNK, CONV_CH), F32),
                        pltpu.VMEM((CHUNK, D_INNER), F32)],
        compiler_params=_params(2),
        name="ssd",
    )(xbc, z, dt_raw, conv_w, conv_b.reshape(1, CONV_CH), lane_pad(dt_bias), lane_pad(a_log),
      jnp.repeat(d_skip, SSM_HEAD_DIM).reshape(1, D_INNER), g_norm.reshape(1, D_INNER), expand2)


def _final_norm_kernel(x_ref, g_ref, o_ref):
    o_ref[...] = _rms(x_ref[...], g_ref[...])


def _final_norm(x, g, tm=1024):
    t, d = x.shape
    return pl.pallas_call(
        _final_norm_kernel,
        grid=(t // tm,),
        in_specs=[pl.BlockSpec((tm, d), lambda i: (i, 0)), _resident((1, d))],
        out_specs=pl.BlockSpec((tm, d), lambda i: (i, 0)),
        out_shape=jax.ShapeDtypeStruct((t, d), F32),
        compiler_params=_params(1),
        name="final_norm",
    )(x, g.reshape(1, d))


def kernel(x, mem, g_mix, g_ffn, g_mem, w_mem_kv, rel_bias, swa_w_in, swa_sinks, swa_w_out, ssm_w_in, ssm_conv_w, ssm_conv_b, ssm_dt_bias, ssm_A_log, ssm_D, ssm_g_norm, ssm_w_out, ffn_w_gate, ffn_w_up, ffn_w_down, moe_w_router, moe_w_gate, moe_w_up, moe_w_down, g_final):
    batch, seq, d = x.shape
    xf = x.reshape(batch * seq, d)
    memf = mem.reshape(batch * MEM_LEN, d)
    for i in range(DEPTH):
        j = i // 2
        (mem_kv,) = _norm_proj(memf, g_mem[i], [w_mem_kv[i].astype(BF16)], [BF16])
        if i % 2 == 0:
            w_in = swa_w_in[j].astype(BF16)
            q, kv, xq = _norm_proj(
                xf, g_mix[i], [w_in[:, :Q_W], w_in[:, Q_W:Q_W + 2 * KV_W], w_in[:, Q_W + 2 * KV_W:]],
                [BF16, BF16, BF16])
            attn = _swa(q, kv, rel_bias, swa_sinks[j], batch)
            cross = _xattn(xq, mem_kv, batch)
            w_out = swa_w_out[j].astype(BF16)
            xf = _out_proj(xf, attn, cross, w_out[:Q_W], w_out[Q_W:])
            xf = _ffn(xf, g_ffn[i], ffn_w_gate[j].astype(BF16), ffn_w_up[j].astype(BF16),
                      ffn_w_down[j].astype(BF16))
        else:
            w_in = ssm_w_in[j].astype(BF16)
            o_dt = D_INNER + CONV_CH
            w_dt = jnp.zeros((d, LANES), BF16).at[:, :N_SSM_HEADS].set(w_in[:, o_dt:o_dt + N_SSM_HEADS])
            z, xbc, dt_raw, xq = _norm_proj(
                xf, g_mix[i], [w_in[:, :D_INNER], w_in[:, D_INNER:o_dt], w_dt, w_in[:, o_dt + N_SSM_HEADS:]],
                [BF16, BF16, F32, BF16])
            y = _ssd(xbc, z, dt_raw, ssm_conv_w[j], ssm_conv_b[j], ssm_dt_bias[j], ssm_A_log[j],
                     ssm_D[j], ssm_g_norm[j], batch)
            cross = _xattn(xq, mem_kv, batch)
            w_out = ssm_w_out[j].astype(BF16)
            xf = _out_proj(xf, y, cross, w_out[:D_INNER], w_out[D_INNER:])
            xf = _moe(xf, g_ffn[i], moe_w_router[j], moe_w_gate[j].astype(BF16), moe_w_up[j].astype(BF16),
                      moe_w_down[j].astype(BF16))
    return _final_norm(xf, g_final).reshape(batch, seq, d)
```

```python
import functools
import math

import numpy as np
import jax
import jax.numpy as jnp
from jax import lax
from jax.experimental import pallas as pl
from jax.experimental.pallas import tpu as pltpu

F32 = jnp.float32
BF16 = jnp.bfloat16

D_MODEL = 1024
DEPTH = 4
MEM_LEN = 256
EPS = 1e-6
N_Q_HEADS = 16
N_KV_HEADS = 2
HEAD_DIM = 64
BLOCK = 128
N_BUCKETS = 32
MAX_DISTANCE = 128
N_X_HEADS = 4
X_HEAD_DIM = 256
D_INNER = 2048
SSM_HEAD_DIM = 64
N_SSM_HEADS = 32
N_SSM_GROUPS = 4
HEADS_PER_GROUP = 8
D_STATE = 128
CONV_WIDTH = 4
CHUNK = 128
D_FF = 2816
N_EXPERTS = 8
Q_W = N_Q_HEADS * HEAD_DIM
KV_W = N_KV_HEADS * HEAD_DIM
XQ_W = N_X_HEADS * X_HEAD_DIM
BC_W = N_SSM_GROUPS * D_STATE
CONV_CH = D_INNER + 2 * BC_W
GROUP_W = HEADS_PER_GROUP * SSM_HEAD_DIM

LANES = 128
VMEM_LIMIT = 56 << 20
NEG_INF = float("-inf")
NT_DIMS = (((1,), (1,)), ((), ()))


def _params(n_axes, vmem=VMEM_LIMIT):
    return pltpu.CompilerParams(dimension_semantics=("arbitrary",) * n_axes, vmem_limit_bytes=vmem)


def _resident(shape):
    nd = len(shape)
    return pl.BlockSpec(shape, lambda *_: (0,) * nd, pipeline_mode=pl.Buffered(1))


def _dot(a, b):
    return jnp.dot(a, b, preferred_element_type=F32)


def _rms(x, g):
    return x * lax.rsqrt(jnp.mean(x * x, axis=-1, keepdims=True) + EPS) * g


def _silu(v):
    return v / (1.0 + jnp.exp(-v))


def _norm_proj_kernel(x_ref, g_ref, *refs, n_out, col_chunk):
    w_refs, o_refs = refs[:n_out], refs[n_out:]
    h = _rms(x_ref[...], g_ref[...]).astype(BF16)
    for w_ref, o_ref in zip(w_refs, o_refs):
        n = w_ref.shape[1]
        for c0 in range(0, n, col_chunk):
            c1 = min(c0 + col_chunk, n)
            o_ref[:, c0:c1] = _dot(h, w_ref[:, c0:c1]).astype(o_ref.dtype)


def _norm_proj(x, g, ws, out_dtypes, tm=512):
    t, d = x.shape
    n_out = len(ws)
    return pl.pallas_call(
        functools.partial(_norm_proj_kernel, n_out=n_out, col_chunk=512),
        grid=(t // tm,),
        in_specs=[pl.BlockSpec((tm, d), lambda i: (i, 0)), _resident((1, d))]
        + [_resident(w.shape) for w in ws],
        out_specs=[pl.BlockSpec((tm, w.shape[1]), lambda i: (i, 0)) for w in ws],
        out_shape=[jax.ShapeDtypeStruct((t, w.shape[1]), dt) for w, dt in zip(ws, out_dtypes)],
        compiler_params=_params(1),
        name="norm_proj",
    )(x, g.reshape(1, d), *ws)


def _bucket_table():
    qi = np.arange(BLOCK)[:, None]
    kj = np.arange(2 * BLOCK)[None, :]
    dist = BLOCK + qi - kj
    max_exact = N_BUCKETS // 2
    d = np.maximum(dist, 0)
    df = np.maximum(d, 1).astype(np.float32)
    far = max_exact + (
        np.log(df / np.float32(max_exact)) / np.float32(math.log(MAX_DISTANCE / max_exact))
        * np.float32(N_BUCKETS - max_exact)
    ).astype(np.int32)
    bucket = np.where(d < max_exact, d, np.minimum(far, N_BUCKETS - 1))
    r = np.arange(BLOCK)[:, None]
    c = np.arange(BLOCK)[None, :]
    return np.where(c <= r, bucket[:, BLOCK:], bucket[:, :BLOCK]).astype(np.int32)


PAIRS_PER_KV = N_Q_HEADS // N_KV_HEADS // 2
SLOT_ROWS = PAIRS_PER_KV * BLOCK


def _swa_slot_head(slot):
    kvh, rest = divmod(slot, 2 * PAIRS_PER_KV)
    half, pair = divmod(rest, PAIRS_PER_KV)
    return 2 * (kvh * PAIRS_PER_KV + pair) + half


def _swa_kernel(relb_ref, bucket_ref, sink_ref, q_ref, kvp_ref, kvc_ref, o_ref, bias_scr, s_scr, p_scr):
    first = (pl.program_id(0) == 0) & (pl.program_id(1) == 0)

    @pl.when(first)
    def _():
        bucket = bucket_ref[...]
        for slot in range(N_Q_HEADS):
            h = _swa_slot_head(slot)
            acc = jnp.zeros((BLOCK, BLOCK), F32)
            for n in range(N_BUCKETS):
                acc = jnp.where(bucket == n, relb_ref[n, h], acc)
            bias_scr[slot * BLOCK:(slot + 1) * BLOCK, :] = acc

    kk = jnp.concatenate([kvp_ref[:, 0:KV_W], kvc_ref[:, 0:KV_W]], axis=0).astype(F32)
    vv = jnp.concatenate([kvp_ref[:, KV_W:], kvc_ref[:, KV_W:]], axis=0).astype(F32)
    low = lax.broadcasted_iota(jnp.int32, kk.shape, 1) < HEAD_DIM

    def placed(t):
        r = pltpu.roll(t, HEAD_DIM, 1)
        return [[jnp.where(low, t, 0.0).astype(BF16), jnp.where(low, 0.0, r).astype(BF16)],
                [jnp.where(low, r, 0.0).astype(BF16), jnp.where(low, 0.0, t).astype(BF16)]]

    k_var, v_var = placed(kk), placed(vv)

    def from_current(n_rows):
        r = lax.broadcasted_iota(jnp.int32, (n_rows, BLOCK), 0) & (BLOCK - 1)
        return lax.broadcasted_iota(jnp.int32, (n_rows, BLOCK), 1) <= r

    cur_slot = from_current(SLOT_ROWS)
    for v in range(2 * N_KV_HEADS):
        kvh, half = divmod(v, 2)
        q_stack = jnp.concatenate(
            [q_ref[:, (kvh * PAIRS_PER_KV + j) * LANES:(kvh * PAIRS_PER_KV + j + 1) * LANES]
             for j in range(PAIRS_PER_KV)], axis=0)
        s = lax.dot_general(q_stack, k_var[kvh][half], NT_DIMS, preferred_element_type=F32)
        s_scr[v * SLOT_ROWS:(v + 1) * SLOT_ROWS, :] = jnp.where(cur_slot, s[:, BLOCK:], s[:, :BLOCK])

    cur_all = from_current(N_Q_HEADS * BLOCK)
    key_exists = jnp.logical_or(cur_all, pl.program_id(1) > 0)
    s = s_scr[...] * (HEAD_DIM ** -0.5) + bias_scr[...]
    s = jnp.where(key_exists, s, NEG_INF)
    sink = sink_ref[...]
    m = jnp.maximum(jnp.max(s, axis=-1, keepdims=True), sink)
    p = jnp.exp(s - m)
    p = p / (jnp.sum(p, axis=-1, keepdims=True) + jnp.exp(sink - m))
    p_scr[...] = p.astype(BF16)

    outs = []
    for v in range(2 * N_KV_HEADS):
        kvh, half = divmod(v, 2)
        p = p_scr[v * SLOT_ROWS:(v + 1) * SLOT_ROWS, :]
        zero = jnp.zeros_like(p)
        p_band = jnp.concatenate([jnp.where(cur_slot, zero, p), jnp.where(cur_slot, p, zero)], axis=1)
        outs.append(_dot(p_band, v_var[kvh][half]))
    for kvh in range(N_KV_HEADS):
        for j in range(PAIRS_PER_KV):
            pair = kvh * PAIRS_PER_KV + j
            out = outs[2 * kvh][j * BLOCK:(j + 1) * BLOCK, :] + outs[2 * kvh + 1][j * BLOCK:(j + 1) * BLOCK, :]
            o_ref[:, pair * LANES:(pair + 1) * LANES] = out.astype(o_ref.dtype)


def _swa(q, kv, rel_bias, sinks, batch):
    t = q.shape[0]
    nb = t // batch // BLOCK
    rows = N_Q_HEADS * BLOCK
    bucket = jnp.asarray(_bucket_table())
    slot_sinks = sinks.astype(F32)[np.array([_swa_slot_head(s) for s in range(N_Q_HEADS)])]
    sink_rows = jnp.broadcast_to(jnp.repeat(slot_sinks, BLOCK)[:, None], (rows, LANES))
    smem = pl.BlockSpec(memory_space=pltpu.SMEM)
    return pl.pallas_call(
        _swa_kernel,
        grid=(batch, nb),
        in_specs=[smem, _resident((BLOCK, BLOCK)), _resident((rows, LANES)),
                  pl.BlockSpec((BLOCK, Q_W), lambda b, i: (b * nb + i, 0)),
                  pl.BlockSpec((BLOCK, 2 * KV_W), lambda b, i: (b * nb + jnp.maximum(i - 1, 0), 0)),
                  pl.BlockSpec((BLOCK, 2 * KV_W), lambda b, i: (b * nb + i, 0))],
        out_specs=pl.BlockSpec((BLOCK, Q_W), lambda b, i: (b * nb + i, 0)),
        out_shape=jax.ShapeDtypeStruct((t, Q_W), BF16),
        scratch_shapes=[pltpu.VMEM((rows, BLOCK), F32), pltpu.VMEM((rows, BLOCK), F32),
                        pltpu.VMEM((rows, BLOCK), BF16)],
        compiler_params=_params(2),
        name="swa",
    )(rel_bias, bucket, sink_rows, q, kv, kv)


def _xattn_kernel(xq_ref, mk_ref, mv_ref, o_ref):
    for h in range(N_X_HEADS):
        sl = slice(h * X_HEAD_DIM, (h + 1) * X_HEAD_DIM)
        s = lax.dot_general(xq_ref[:, sl], mk_ref[:, sl], NT_DIMS, preferred_element_type=F32)
        s = s * (X_HEAD_DIM ** -0.5)
        p = jnp.exp(s - jnp.max(s, axis=-1, keepdims=True))
        p = p / jnp.sum(p, axis=-1, keepdims=True)
        o_ref[:, sl] = _dot(p.astype(BF16), mv_ref[:, sl]).astype(o_ref.dtype)


def _xattn(xq, mem_kv, batch, tm=512):
    t = xq.shape[0]
    nt = t // batch // tm
    return pl.pallas_call(
        _xattn_kernel,
        grid=(batch, nt),
        in_specs=[pl.BlockSpec((tm, XQ_W), lambda b, i: (b * nt + i, 0)),
                  pl.BlockSpec((MEM_LEN, XQ_W), lambda b, i: (b, 0)),
                  pl.BlockSpec((MEM_LEN, XQ_W), lambda b, i: (b, 1))],
        out_specs=pl.BlockSpec((tm, XQ_W), lambda b, i: (b * nt + i, 0)),
        out_shape=jax.ShapeDtypeStruct((t, XQ_W), BF16),
        compiler_params=_params(2),
        name="xattn",
    )(xq, mem_kv, mem_kv)


def _out_proj_kernel(x_ref, a_ref, c_ref, wa_ref, wc_ref, o_ref):
    o_ref[...] = x_ref[...] + _dot(a_ref[...], wa_ref[...]) + _dot(c_ref[...], wc_ref[...])


def _out_proj(x, a, c, wa, wc, tm=512):
    t, d = x.shape
    return pl.pallas_call(
        _out_proj_kernel,
        grid=(t // tm,),
        in_specs=[pl.BlockSpec((tm, d), lambda i: (i, 0)),
                  pl.BlockSpec((tm, a.shape[1]), lambda i: (i, 0)),
                  pl.BlockSpec((tm, c.shape[1]), lambda i: (i, 0)),
                  _resident(wa.shape), _resident(wc.shape)],
        out_specs=pl.BlockSpec((tm, d), lambda i: (i, 0)),
        out_shape=jax.ShapeDtypeStruct((t, d), F32),
        compiler_params=_params(1),
        name="out_proj",
    )(x, a, c, wa, wc)


FF_CHUNK = 256


def _ffn_kernel(x_ref, g_ref, wg_ref, wu_ref, wd_ref, o_ref, act_scr):
    x = x_ref[...]
    h = _rms(x, g_ref[...]).astype(BF16)
    for c0 in range(0, D_FF, FF_CHUNK):
        sl = slice(c0, c0 + FF_CHUNK)
        act_scr[:, sl] = (_silu(_dot(h, wg_ref[:, sl])) * _dot(h, wu_ref[:, sl])).astype(BF16)
    o_ref[...] = x + _dot(act_scr[...], wd_ref[...])


def _ffn(x, g, wg, wu, wd, tm=512):
    t, d = x.shape
    return pl.pallas_call(
        _ffn_kernel,
        grid=(t // tm,),
        in_specs=[pl.BlockSpec((tm, d), lambda i: (i, 0)), _resident((1, d)),
                  _resident(wg.shape), _resident(wu.shape), _resident(wd.shape)],
        out_specs=pl.BlockSpec((tm, d), lambda i: (i, 0)),
        out_shape=jax.ShapeDtypeStruct((t, d), F32),
        scratch_shapes=[pltpu.VMEM((tm, D_FF), BF16)],
        compiler_params=_params(1),
        name="ffn",
    )(x, g.reshape(1, d), wg, wu, wd)


TOP_K = 2
HI16 = 0xFFFF0000
META_EXPERT, META_POS, META_GATE = 0, 2, 4


def _pack_bf16_pairs(h):
    n = h.shape[1] // 2
    u = pltpu.bitcast(h.astype(BF16).astype(F32), jnp.uint32)
    return (u[:, :n] >> 16) | (u[:, n:] & jnp.uint32(HI16))


def _unpack_bf16_pairs(w):
    lo = pltpu.bitcast(w << 16, F32).astype(BF16)
    hi = pltpu.bitcast(w & jnp.uint32(HI16), F32).astype(BF16)
    return jnp.concatenate([lo, hi], axis=1)


def _router_kernel(x_ref, g_ref, wr_ref, meta_ref, cnt_ref, hpk_ref, run_scr):
    @pl.when(pl.program_id(0) == 0)
    def _():
        run_scr[...] = jnp.zeros_like(run_scr)

    h = _rms(x_ref[...], g_ref[...])
    hpk_ref[...] = _pack_bf16_pairs(h)
    logits = jnp.dot(h, wr_ref[...], preferred_element_type=F32, precision=lax.Precision.HIGHEST)
    lane = lax.broadcasted_iota(jnp.int32, logits.shape, 1)
    lg = jnp.where(lane < N_EXPERTS, logits, NEG_INF)
    m1 = jnp.max(lg, axis=-1, keepdims=True)
    i1 = jnp.min(jnp.where(lg == m1, lane, LANES), axis=-1, keepdims=True)
    lg2 = jnp.where(lane == i1, NEG_INF, lg)
    m2 = jnp.max(lg2, axis=-1, keepdims=True)
    i2 = jnp.min(jnp.where(lg2 == m2, lane, LANES), axis=-1, keepdims=True)
    e = jnp.exp(m2 - m1)
    g1, g2 = 1.0 / (1.0 + e), e / (1.0 + e)

    tm = h.shape[0]
    picked = jnp.logical_or(lane == i1, lane == i2)
    onehot = jnp.where(picked, 1.0, 0.0)
    earlier = lax.broadcasted_iota(jnp.int32, (tm, tm), 1) < lax.broadcasted_iota(jnp.int32, (tm, tm), 0)
    before = _dot(earlier.astype(BF16), onehot.astype(BF16)) + run_scr[...]
    pos1 = jnp.sum(jnp.where(lane == i1, before, 0.0), axis=-1, keepdims=True)
    pos2 = jnp.sum(jnp.where(lane == i2, before, 0.0), axis=-1, keepdims=True)
    run_scr[...] += jnp.sum(onehot, axis=0, keepdims=True)
    cnt_ref[...] = run_scr[...]

    def put(k, v):
        return jnp.where(lane == k, v, 0.0)

    meta_ref[...] = (put(META_EXPERT, i1.astype(F32)) + put(META_EXPERT + 1, i2.astype(F32))
                     + put(META_POS, pos1) + put(META_POS + 1, pos2)
                     + put(META_GATE, g1) + put(META_GATE + 1, g2))


def _router(x, g, w_router, tm=512):
    t, d = x.shape
    wr = jnp.zeros((d, LANES), F32).at[:, :N_EXPERTS].set(w_router)
    return pl.pallas_call(
        _router_kernel,
        grid=(t // tm,),
        in_specs=[pl.BlockSpec((tm, d), lambda i: (i, 0)), _resident((1, d)), _resident((d, LANES))],
        out_specs=[pl.BlockSpec((tm, LANES), lambda i: (i, 0)),
                   pl.BlockSpec((1, LANES), lambda i: (0, 0)),
                   pl.BlockSpec((tm, d // 2), lambda i: (i, 0))],
        out_shape=[jax.ShapeDtypeStruct((t, LANES), F32), jax.ShapeDtypeStruct((1, LANES), F32),
                   jax.ShapeDtypeStruct((t, d // 2), jnp.uint32)],
        scratch_shapes=[pltpu.VMEM((1, LANES), F32)],
        compiler_params=_params(1),
        name="router",
    )(x, g.reshape(1, d), wr)


ROW_TILE = 512
TOK_TILE = 512


def _row_copy_all_wait(src_like, dst_like, sem):
    pltpu.make_async_copy(src_like, dst_like, sem).wait()


def _dispatch_kernel(dest_hbm, hpk_ref, xs_in, xs_out, idx_smem, idx_sem, row_sem):
    del xs_in
    tm = hpk_ref.shape[0]
    idx_copy = pltpu.make_async_copy(dest_hbm.at[pl.program_id(0)], idx_smem, idx_sem)
    idx_copy.start()
    idx_copy.wait()

    def issue(t, carry):
        for k in range(TOP_K):
            row = idx_smem[TOP_K * t + k]
            pltpu.make_async_copy(hpk_ref.at[pl.ds(t, 1)], xs_out.at[pl.ds(row, 1)], row_sem).start()
        return carry

    lax.fori_loop(0, tm, issue, 0, unroll=8)
    for _ in range(TOP_K):
        _row_copy_all_wait(hpk_ref, xs_out.at[pl.ds(0, tm)], row_sem)


def _dispatch(dest_tiles, hpk, n_rows):
    t, w = hpk.shape
    tm = TOK_TILE
    any_spec = pl.BlockSpec(memory_space=pl.ANY)
    return pl.pallas_call(
        _dispatch_kernel,
        grid=(t // tm,),
        in_specs=[any_spec, pl.BlockSpec((tm, w), lambda i: (i, 0)), any_spec],
        out_specs=any_spec,
        out_shape=jax.ShapeDtypeStruct((n_rows, w), jnp.uint32),
        scratch_shapes=[pltpu.SMEM((TOP_K * tm,), jnp.int32), pltpu.SemaphoreType.DMA,
                        pltpu.SemaphoreType.DMA],
        input_output_aliases={2: 0},
        compiler_params=_params(1),
        name="dispatch",
    )(dest_tiles, hpk, jnp.zeros((n_rows, w), jnp.uint32))


def _experts_kernel(tile_expert_ref, n_used_ref, xs_ref, wg_ref, wu_ref, wd_ref, y_ref, act_scr):
    del tile_expert_ref

    @pl.when(pl.program_id(0) < n_used_ref[0])
    def _():
        h = _unpack_bf16_pairs(xs_ref[...])
        for c0 in range(0, D_FF, FF_CHUNK):
            sl = slice(c0, c0 + FF_CHUNK)
            act_scr[:, sl] = (_silu(_dot(h, wg_ref[0, :, sl])) * _dot(h, wu_ref[0, :, sl])).astype(BF16)
        y_ref[...] = _dot(act_scr[...], wd_ref[0])

    @pl.when(pl.program_id(0) >= n_used_ref[0])
    def _():
        y_ref[...] = jnp.zeros_like(y_ref)


def _experts(tile_expert, n_used, xs, wg, wu, wd):
    n_rows, w = xs.shape
    d = 2 * w
    tm = ROW_TILE
    wspec = lambda shape: pl.BlockSpec((1,) + shape, lambda i, te, nu: (te[i], 0, 0))
    return pl.pallas_call(
        _experts_kernel,
        grid_spec=pltpu.PrefetchScalarGridSpec(
            num_scalar_prefetch=2,
            grid=(n_rows // tm,),
            in_specs=[pl.BlockSpec((tm, w), lambda i, te, nu: (i, 0)),
                      wspec((d, D_FF)), wspec((d, D_FF)), wspec((D_FF, d))],
            out_specs=pl.BlockSpec((tm, d), lambda i, te, nu: (i, 0)),
            scratch_shapes=[pltpu.VMEM((tm, D_FF), BF16)]),
        out_shape=jax.ShapeDtypeStruct((n_rows, d), F32),
        compiler_params=_params(1),
        name="experts",
    )(tile_expert, n_used, xs, wg, wu, wd)


def _combine_kernel(dest_hbm, x_ref, meta_ref, y_hbm, o_ref, idx_smem, y_buf, idx_sem, row_sem):
    tm = x_ref.shape[0]
    idx_copy = pltpu.make_async_copy(dest_hbm.at[pl.program_id(0)], idx_smem, idx_sem)
    idx_copy.start()
    idx_copy.wait()

    def issue(t, carry):
        for k in range(TOP_K):
            row = idx_smem[TOP_K * t + k]
            pltpu.make_async_copy(y_hbm.at[pl.ds(row, 1)], y_buf.at[k, pl.ds(t, 1)], row_sem).start()
        return carry

    lax.fori_loop(0, tm, issue, 0, unroll=8)
    for k in range(TOP_K):
        _row_copy_all_wait(y_hbm.at[pl.ds(0, tm)], y_buf.at[k], row_sem)
    meta = meta_ref[...]
    out = x_ref[...]
    for k in range(TOP_K):
        out = out + meta[:, META_GATE + k:META_GATE + k + 1] * y_buf[k]
    o_ref[...] = out


def _combine(dest_tiles, x, meta, y):
    t, d = x.shape
    tm = TOK_TILE
    any_spec = pl.BlockSpec(memory_space=pl.ANY)
    return pl.pallas_call(
        _combine_kernel,
        grid=(t // tm,),
        in_specs=[any_spec, pl.BlockSpec((tm, d), lambda i: (i, 0)),
                  pl.BlockSpec((tm, LANES), lambda i: (i, 0)), any_spec],
        out_specs=pl.BlockSpec((tm, d), lambda i: (i, 0)),
        out_shape=jax.ShapeDtypeStruct((t, d), F32),
        scratch_shapes=[pltpu.SMEM((TOP_K * tm,), jnp.int32), pltpu.VMEM((TOP_K, tm, d), F32),
                        pltpu.SemaphoreType.DMA, pltpu.SemaphoreType.DMA],
        compiler_params=_params(1),
        name="combine",
    )(dest_tiles, x, meta, y)


def _moe(x, g, w_router, wg, wu, wd):
    t, d = x.shape
    meta, counts, hpk = _router(x, g, w_router)
    cnt = counts[0, :N_EXPERTS].astype(jnp.int32)
    padded = (cnt + ROW_TILE - 1) // ROW_TILE * ROW_TILE
    ends = jnp.cumsum(padded)
    starts = ends - padded
    expert = meta[:, META_EXPERT:META_EXPERT + TOP_K].astype(jnp.int32)
    pos = meta[:, META_POS:META_POS + TOP_K].astype(jnp.int32)
    dest = pos
    for e in range(N_EXPERTS):
        dest = dest + jnp.where(expert == e, starts[e], 0)
    dest_tiles = dest.reshape(t // TOK_TILE, TOP_K * TOK_TILE)
    n_rows = TOP_K * t + N_EXPERTS * ROW_TILE
    n_tiles = n_rows // ROW_TILE
    n_used = ends[-1] // ROW_TILE
    tile_start = jnp.minimum(jnp.arange(n_tiles), n_used - 1) * ROW_TILE
    tile_expert = jnp.sum(tile_start[:, None] >= ends[None, :], axis=1).astype(jnp.int32)
    xs = _dispatch(dest_tiles, hpk, n_rows)
    y = _experts(tile_expert, n_used.reshape(1).astype(jnp.int32), xs, wg, wu, wd)
    return _combine(dest_tiles, x, meta, y)


CONV_TAIL = 16


def _conv_shift_matrix():
    m = np.zeros(((CONV_WIDTH - 1) * CHUNK, CONV_TAIL + CHUNK), np.float32)
    for k in range(CONV_WIDTH - 1):
        for t in range(CHUNK):
            m[k * CHUNK + t, CONV_TAIL + t - (CONV_WIDTH - 1) + k] = 1.0
    return m


def _split_bf16(v, parts):
    out, r = [], v
    for _ in range(parts):
        p = r.astype(BF16)
        out.append(p)
        r = r - p.astype(F32)
    return out


def _ssd_kernel(xbc_ref, z_ref, dt_ref, cw_ref, cb_ref, dtb_ref, alog_ref, dsk_ref, gn_ref, exp_ref, shift_ref,
                o_ref, state_scr, ext_scr, y_scr):
    @pl.when(pl.program_id(1) == 0)
    def _():
        state_scr[...] = jnp.zeros_like(state_scr)
        ext_scr[CHUNK:, :] = jnp.zeros((CONV_TAIL, CONV_CH), BF16)

    ext_scr[0:CONV_TAIL, :] = ext_scr[CHUNK:, :]
    ext_scr[CONV_TAIL:, :] = xbc_ref[...]

    def conv_silu(c0, c1):
        shifted = _dot(shift_ref[...], ext_scr[:, c0:c1])
        acc = cb_ref[:, c0:c1] + cw_ref[CONV_WIDTH - 1:CONV_WIDTH, c0:c1] * ext_scr[CONV_TAIL:, c0:c1].astype(F32)
        for k in range(CONV_WIDTH - 1):
            acc = acc + cw_ref[k:k + 1, c0:c1] * shifted[k * CHUNK:(k + 1) * CHUNK, :]
        return _silu(acc)

    dt = dt_ref[...] + dtb_ref[...]
    dt = jnp.maximum(dt, 0.0) + jnp.log(1.0 + jnp.exp(-jnp.abs(dt)))
    da = dt * -jnp.exp(alog_ref[...])
    row = lax.broadcasted_iota(jnp.int32, (CHUNK, CHUNK), 0)
    col = lax.broadcasted_iota(jnp.int32, (CHUNK, CHUNK), 1)
    causal = col <= row
    cs = _dot(causal.astype(BF16), jnp.concatenate(_split_bf16(da, 3), axis=1))
    acum = cs[:, 0:LANES] + cs[:, LANES:2 * LANES] + cs[:, 2 * LANES:]
    acum_t = acum.T
    ea = jnp.exp(acum)
    dtte = dt * jnp.exp(acum[CHUNK - 1:CHUNK, :] - acum)

    def expand(v, g):
        return _dot(jnp.concatenate(_split_bf16(v, 2), axis=1), exp_ref[:, g * GROUP_W:(g + 1) * GROUP_W])

    low = lax.broadcasted_iota(jnp.int32, (CHUNK, LANES), 1) < SSM_HEAD_DIM
    for g in range(N_SSM_GROUPS):
        gsl = slice(g * GROUP_W, (g + 1) * GROUP_W)
        xg = conv_silu(g * GROUP_W, (g + 1) * GROUP_W)
        bg = conv_silu(D_INNER + g * D_STATE, D_INNER + (g + 1) * D_STATE)
        cg = conv_silu(D_INNER + BC_W + g * D_STATE, D_INNER + BC_W + (g + 1) * D_STATE).astype(BF16)
        ea_x = expand(ea, g)
        xdt = (xg * expand(dt, g)).astype(BF16)
        cb = lax.dot_general(cg, bg.astype(BF16), NT_DIMS, preferred_element_type=F32)
        state = state_scr[g]
        y = _dot(cg, state.astype(BF16)) * ea_x + xg * dsk_ref[:, gsl]
        diag = []
        for pair in range(HEADS_PER_GROUP // 2):
            xp = xdt[:, pair * LANES:(pair + 1) * LANES]
            acc = jnp.zeros((CHUNK, LANES), F32)
            for half in range(2):
                h = g * HEADS_PER_GROUP + 2 * pair + half
                seg = acum[:, h:h + 1] - acum_t[h:h + 1, :]
                m = (cb * jnp.exp(jnp.where(causal, seg, NEG_INF))).astype(BF16)
                xm = jnp.where(low if half == 0 else jnp.logical_not(low), xp, jnp.zeros_like(xp))
                acc = acc + _dot(m, xm)
            diag.append(acc)
        y_scr[:, gsl] = y + jnp.concatenate(diag, axis=1)
        w = (xg * expand(dtte, g)).astype(BF16)
        state_scr[g] = state * ea_x[CHUNK - 1:CHUNK, :] + _dot(bg.T.astype(BF16), w)

    z = z_ref[...].astype(F32)
    o_ref[...] = _rms(y_scr[...] * _silu(z), gn_ref[...]).astype(o_ref.dtype)


def _ssd(xbc, z, dt_raw, conv_w, conv_b, dt_bias, a_log, d_skip, g_norm, batch):
    t = xbc.shape[0]
    nc = t // batch // CHUNK

    def lane_pad(v):
        return jnp.zeros((1, LANES), F32).at[0, :N_SSM_HEADS].set(v)

    expand = np.zeros((LANES, D_INNER), np.float32)
    for h in range(N_SSM_HEADS):
        expand[h, h * SSM_HEAD_DIM:(h + 1) * SSM_HEAD_DIM] = 1.0
    expand2 = jnp.asarray(np.concatenate([expand, expand], axis=0), BF16)
    tok = lambda b, c: (b * nc + c, 0)
    return pl.pallas_call(
        _ssd_kernel,
        grid=(batch, nc),
        in_specs=[pl.BlockSpec((CHUNK, CONV_CH), tok), pl.BlockSpec((CHUNK, D_INNER), tok),
                  pl.BlockSpec((CHUNK, LANES), tok),
                  _resident((CONV_WIDTH, CONV_CH)), _resident((1, CONV_CH)),
                  _resident((1, LANES)), _resident((1, LANES)),
                  _resident((1, D_INNER)), _resident((1, D_INNER)), _resident((2 * LANES, D_INNER)),
                  _resident(((CONV_WIDTH - 1) * CHUNK, CONV_TAIL + CHUNK))],
        out_specs=pl.BlockSpec((CHUNK, D_INNER), tok),
        out_shape=jax.ShapeDtypeStruct((t, D_INNER), BF16),
        scratch_shapes=[pltpu.VMEM((N_SSM_GROUPS, D_STATE, GROUP_W), F32),
                        pltpu.VMEM((CONV_TAIL + CHUNK, CONV_CH), BF16),
                        pltpu.VMEM((CHUNK, D_INNER), F32)],
        compiler_params=_params(2),
        name="ssd",
    )(xbc, z, dt_raw, conv_w, conv_b.reshape(1, CONV_CH), lane_pad(dt_bias), lane_pad(a_log),
      jnp.repeat(d_skip, SSM_HEAD_DIM).reshape(1, D_INNER), g_norm.reshape(1, D_INNER), expand2,
      jnp.asarray(_conv_shift_matrix(), BF16))


def _final_norm_kernel(x_ref, g_ref, o_ref):
    o_ref[...] = _rms(x_ref[...], g_ref[...])


def _final_norm(x, g, tm=1024):
    t, d = x.shape
    return pl.pallas_call(
        _final_norm_kernel,
        grid=(t // tm,),
        in_specs=[pl.BlockSpec((tm, d), lambda i: (i, 0)), _resident((1, d))],
        out_specs=pl.BlockSpec((tm, d), lambda i: (i, 0)),
        out_shape=jax.ShapeDtypeStruct((t, d), F32),
        compiler_params=_params(1),
        name="final_norm",
    )(x, g.reshape(1, d))


def kernel(x, mem, g_mix, g_ffn, g_mem, w_mem_kv, rel_bias, swa_w_in, swa_sinks, swa_w_out, ssm_w_in, ssm_conv_w, ssm_conv_b, ssm_dt_bias, ssm_A_log, ssm_D, ssm_g_norm, ssm_w_out, ffn_w_gate, ffn_w_up, ffn_w_down, moe_w_router, moe_w_gate, moe_w_up, moe_w_down, g_final):
    batch, seq, d = x.shape
    xf = x.reshape(batch * seq, d)
    memf = mem.reshape(batch * MEM_LEN, d)
    for i in range(DEPTH):
        j = i // 2
        (mem_kv,) = _norm_proj(memf, g_mem[i], [w_mem_kv[i].astype(BF16)], [BF16])
        if i % 2 == 0:
            w_in = swa_w_in[j].astype(BF16)
            q, kv, xq = _norm_proj(
                xf, g_mix[i], [w_in[:, :Q_W], w_in[:, Q_W:Q_W + 2 * KV_W], w_in[:, Q_W + 2 * KV_W:]],
                [BF16, BF16, BF16])
            attn = _swa(q, kv, rel_bias, swa_sinks[j], batch)
            cross = _xattn(xq, mem_kv, batch)
            w_out = swa_w_out[j].astype(BF16)
            xf = _out_proj(xf, attn, cross, w_out[:Q_W], w_out[Q_W:])
            xf = _ffn(xf, g_ffn[i], ffn_w_gate[j].astype(BF16), ffn_w_up[j].astype(BF16),
                      ffn_w_down[j].astype(BF16))
        else:
            w_in = ssm_w_in[j].astype(BF16)
            o_dt = D_INNER + CONV_CH
            w_dt = jnp.zeros((d, LANES), BF16).at[:, :N_SSM_HEADS].set(w_in[:, o_dt:o_dt + N_SSM_HEADS])
            z, xbc, dt_raw, xq = _norm_proj(
                xf, g_mix[i], [w_in[:, :D_INNER], w_in[:, D_INNER:o_dt], w_dt, w_in[:, o_dt + N_SSM_HEADS:]],
                [BF16, BF16, F32, BF16])
            y = _ssd(xbc, z, dt_raw, ssm_conv_w[j], ssm_conv_b[j], ssm_dt_bias[j], ssm_A_log[j],
                     ssm_D[j], ssm_g_norm[j], batch)
            cross = _xattn(xq, mem_kv, batch)
            w_out = ssm_w_out[j].astype(BF16)
            xf = _out_proj(xf, y, cross, w_out[:D_INNER], w_out[D_INNER:])
            xf = _moe(xf, g_ffn[i], moe_w_router[j], moe_w_gate[j].astype(BF16), moe_w_up[j].astype(BF16),
                      moe_w_down[j].astype(BF16))
    return _final_norm(xf, g_final).reshape(batch, seq, d)
```

```python
import functools
import math

import numpy as np
import jax
import jax.numpy as jnp
from jax import lax
from jax.experimental import pallas as pl
from jax.experimental.pallas import tpu as pltpu

F32 = jnp.float32
BF16 = jnp.bfloat16

D_MODEL = 1024
DEPTH = 4
MEM_LEN = 256
EPS = 1e-6
N_Q_HEADS = 16
N_KV_HEADS = 2
HEAD_DIM = 64
BLOCK = 128
N_BUCKETS = 32
MAX_DISTANCE = 128
N_X_HEADS = 4
X_HEAD_DIM = 256
D_INNER = 2048
SSM_HEAD_DIM = 64
N_SSM_HEADS = 32
N_SSM_GROUPS = 4
HEADS_PER_GROUP = 8
D_STATE = 128
CONV_WIDTH = 4
CHUNK = 128
D_FF = 2816
N_EXPERTS = 8
Q_W = N_Q_HEADS * HEAD_DIM
KV_W = N_KV_HEADS * HEAD_DIM
XQ_W = N_X_HEADS * X_HEAD_DIM
BC_W = N_SSM_GROUPS * D_STATE
CONV_CH = D_INNER + 2 * BC_W
GROUP_W = HEADS_PER_GROUP * SSM_HEAD_DIM

LANES = 128
VMEM_LIMIT = 56 << 20
NEG_INF = float("-inf")
NT_DIMS = (((1,), (1,)), ((), ()))


def _params(n_axes, vmem=VMEM_LIMIT):
    return pltpu.CompilerParams(dimension_semantics=("arbitrary",) * n_axes, vmem_limit_bytes=vmem)


def _resident(shape):
    nd = len(shape)
    return pl.BlockSpec(shape, lambda *_: (0,) * nd, pipeline_mode=pl.Buffered(1))


def _dot(a, b):
    return jnp.dot(a, b, preferred_element_type=F32)


def _rms(x, g):
    return x * lax.rsqrt(jnp.mean(x * x, axis=-1, keepdims=True) + EPS) * g


def _silu(v):
    return v / (1.0 + jnp.exp(-v))


def _norm_proj_kernel(x_ref, g_ref, *refs, n_out, col_chunk):
    w_refs, o_refs = refs[:n_out], refs[n_out:]
    h = _rms(x_ref[...], g_ref[...]).astype(BF16)
    for w_ref, o_ref in zip(w_refs, o_refs):
        n = w_ref.shape[1]
        for c0 in range(0, n, col_chunk):
            c1 = min(c0 + col_chunk, n)
            o_ref[:, c0:c1] = _dot(h, w_ref[:, c0:c1]).astype(o_ref.dtype)


def _norm_proj(x, g, ws, out_dtypes, tm=512):
    t, d = x.shape
    n_out = len(ws)
    return pl.pallas_call(
        functools.partial(_norm_proj_kernel, n_out=n_out, col_chunk=512),
        grid=(t // tm,),
        in_specs=[pl.BlockSpec((tm, d), lambda i: (i, 0)), _resident((1, d))]
        + [_resident(w.shape) for w in ws],
        out_specs=[pl.BlockSpec((tm, w.shape[1]), lambda i: (i, 0)) for w in ws],
        out_shape=[jax.ShapeDtypeStruct((t, w.shape[1]), dt) for w, dt in zip(ws, out_dtypes)],
        compiler_params=_params(1),
        name="norm_proj",
    )(x, g.reshape(1, d), *ws)


def _bucket_table():
    qi = np.arange(BLOCK)[:, None]
    kj = np.arange(2 * BLOCK)[None, :]
    dist = BLOCK + qi - kj
    max_exact = N_BUCKETS // 2
    d = np.maximum(dist, 0)
    df = np.maximum(d, 1).astype(np.float32)
    far = max_exact + (
        np.log(df / np.float32(max_exact)) / np.float32(math.log(MAX_DISTANCE / max_exact))
        * np.float32(N_BUCKETS - max_exact)
    ).astype(np.int32)
    bucket = np.where(d < max_exact, d, np.minimum(far, N_BUCKETS - 1))
    r = np.arange(BLOCK)[:, None]
    c = np.arange(BLOCK)[None, :]
    return np.where(c <= r, bucket[:, BLOCK:], bucket[:, :BLOCK]).astype(np.int32)


PAIRS_PER_KV = N_Q_HEADS // N_KV_HEADS // 2
SLOT_ROWS = PAIRS_PER_KV * BLOCK


def _swa_slot_head(slot):
    kvh, rest = divmod(slot, 2 * PAIRS_PER_KV)
    half, pair = divmod(rest, PAIRS_PER_KV)
    return 2 * (kvh * PAIRS_PER_KV + pair) + half


def _swa_kernel(relb_ref, bucket_ref, sink_ref, q_ref, kvp_ref, kvc_ref, o_ref, bias_scr, s_scr, p_scr):
    first = (pl.program_id(0) == 0) & (pl.program_id(1) == 0)

    @pl.when(first)
    def _():
        bucket = bucket_ref[...]
        for slot in range(N_Q_HEADS):
            h = _swa_slot_head(slot)
            acc = jnp.zeros((BLOCK, BLOCK), F32)
            for n in range(N_BUCKETS):
                acc = jnp.where(bucket == n, relb_ref[n, h], acc)
            bias_scr[slot * BLOCK:(slot + 1) * BLOCK, :] = acc

    kk = jnp.concatenate([kvp_ref[:, 0:KV_W], kvc_ref[:, 0:KV_W]], axis=0).astype(F32)
    vv = jnp.concatenate([kvp_ref[:, KV_W:], kvc_ref[:, KV_W:]], axis=0).astype(F32)
    low = lax.broadcasted_iota(jnp.int32, kk.shape, 1) < HEAD_DIM

    def placed(t):
        r = pltpu.roll(t, HEAD_DIM, 1)
        return [[jnp.where(low, t, 0.0).astype(BF16), jnp.where(low, 0.0, r).astype(BF16)],
                [jnp.where(low, r, 0.0).astype(BF16), jnp.where(low, 0.0, t).astype(BF16)]]

    k_var, v_var = placed(kk), placed(vv)

    def from_current(n_rows):
        r = lax.broadcasted_iota(jnp.int32, (n_rows, BLOCK), 0) & (BLOCK - 1)
        return lax.broadcasted_iota(jnp.int32, (n_rows, BLOCK), 1) <= r

    cur_slot = from_current(SLOT_ROWS)
    for v in range(2 * N_KV_HEADS):
        kvh, half = divmod(v, 2)
        q_stack = jnp.concatenate(
            [q_ref[:, (kvh * PAIRS_PER_KV + j) * LANES:(kvh * PAIRS_PER_KV + j + 1) * LANES]
             for j in range(PAIRS_PER_KV)], axis=0)
        s = lax.dot_general(q_stack, k_var[kvh][half], NT_DIMS, preferred_element_type=F32)
        s_scr[v * SLOT_ROWS:(v + 1) * SLOT_ROWS, :] = jnp.where(cur_slot, s[:, BLOCK:], s[:, :BLOCK])

    cur_all = from_current(N_Q_HEADS * BLOCK)
    key_exists = jnp.logical_or(cur_all, pl.program_id(1) > 0)
    s = s_scr[...] * (HEAD_DIM ** -0.5) + bias_scr[...]
    s = jnp.where(key_exists, s, NEG_INF)
    sink = sink_ref[...]
    m = jnp.maximum(jnp.max(s, axis=-1, keepdims=True), sink)
    p = jnp.exp(s - m)
    p = p / (jnp.sum(p, axis=-1, keepdims=True) + jnp.exp(sink - m))
    p_scr[...] = p.astype(BF16)

    outs = []
    for v in range(2 * N_KV_HEADS):
        kvh, half = divmod(v, 2)
        p = p_scr[v * SLOT_ROWS:(v + 1) * SLOT_ROWS, :]
        zero = jnp.zeros_like(p)
        p_band = jnp.concatenate([jnp.where(cur_slot, zero, p), jnp.where(cur_slot, p, zero)], axis=1)
        outs.append(_dot(p_band, v_var[kvh][half]))
    for kvh in range(N_KV_HEADS):
        for j in range(PAIRS_PER_KV):
            pair = kvh * PAIRS_PER_KV + j
            out = outs[2 * kvh][j * BLOCK:(j + 1) * BLOCK, :] + outs[2 * kvh + 1][j * BLOCK:(j + 1) * BLOCK, :]
            o_ref[:, pair * LANES:(pair + 1) * LANES] = out.astype(o_ref.dtype)


def _swa(q, kv, rel_bias, sinks, batch):
    t = q.shape[0]
    nb = t // batch // BLOCK
    rows = N_Q_HEADS * BLOCK
    bucket = jnp.asarray(_bucket_table())
    slot_sinks = sinks.astype(F32)[np.array([_swa_slot_head(s) for s in range(N_Q_HEADS)])]
    sink_rows = jnp.broadcast_to(jnp.repeat(slot_sinks, BLOCK)[:, None], (rows, LANES))
    smem = pl.BlockSpec(memory_space=pltpu.SMEM)
    return pl.pallas_call(
        _swa_kernel,
        grid=(batch, nb),
        in_specs=[smem, _resident((BLOCK, BLOCK)), _resident((rows, LANES)),
                  pl.BlockSpec((BLOCK, Q_W), lambda b, i: (b * nb + i, 0)),
                  pl.BlockSpec((BLOCK, 2 * KV_W), lambda b, i: (b * nb + jnp.maximum(i - 1, 0), 0)),
                  pl.BlockSpec((BLOCK, 2 * KV_W), lambda b, i: (b * nb + i, 0))],
        out_specs=pl.BlockSpec((BLOCK, Q_W), lambda b, i: (b * nb + i, 0)),
        out_shape=jax.ShapeDtypeStruct((t, Q_W), BF16),
        scratch_shapes=[pltpu.VMEM((rows, BLOCK), F32), pltpu.VMEM((rows, BLOCK), F32),
                        pltpu.VMEM((rows, BLOCK), BF16)],
        compiler_params=_params(2),
        name="swa",
    )(rel_bias, bucket, sink_rows, q, kv, kv)


TAIL_TILE = 512
FF_CHUNK = 256


def _cross_attention(xq_ref, mk_ref, mv_ref, cross_scr):
    for h in range(N_X_HEADS):
        sl = slice(h * X_HEAD_DIM, (h + 1) * X_HEAD_DIM)
        s = lax.dot_general(xq_ref[:, sl], mk_ref[:, sl], NT_DIMS, preferred_element_type=F32)
        s = s * (X_HEAD_DIM ** -0.5)
        p = jnp.exp(s - jnp.max(s, axis=-1, keepdims=True))
        p = p / jnp.sum(p, axis=-1, keepdims=True)
        cross_scr[:, sl] = _dot(p.astype(BF16), mv_ref[:, sl]).astype(cross_scr.dtype)


def _swiglu_residual(x1, g_ref, wg_ref, wu_ref, wd_ref, act_scr):
    h = _rms(x1, g_ref[...]).astype(BF16)
    for c0 in range(0, D_FF, FF_CHUNK):
        sl = slice(c0, c0 + FF_CHUNK)
        act_scr[:, sl] = (_silu(_dot(h, wg_ref[:, sl])) * _dot(h, wu_ref[:, sl])).astype(BF16)
    return x1 + _dot(act_scr[...], wd_ref[...])


def _tail_specs(batch, t, a_width):
    nt = t // batch // TAIL_TILE
    tok = lambda w: pl.BlockSpec((TAIL_TILE, w), lambda b, i: (b * nt + i, 0))
    mem = lambda half: pl.BlockSpec((MEM_LEN, XQ_W), lambda b, i: (b, half))
    return nt, tok, [tok(D_MODEL), tok(a_width), tok(XQ_W), mem(0), mem(1),
                     _resident((a_width, D_MODEL)), _resident((XQ_W, D_MODEL))]


def _attn_tail_kernel(x_ref, a_ref, xq_ref, mk_ref, mv_ref, wa_ref, wc_ref, g_ref, wg_ref, wu_ref, wd_ref,
                      o_ref, cross_scr, act_scr):
    _cross_attention(xq_ref, mk_ref, mv_ref, cross_scr)
    x1 = x_ref[...] + _dot(a_ref[...], wa_ref[...]) + _dot(cross_scr[...], wc_ref[...])
    o_ref[...] = _swiglu_residual(x1, g_ref, wg_ref, wu_ref, wd_ref, act_scr)


def _attn_tail(x, attn, xq, mem_kv, wa, wc, g, wg, wu, wd, batch):
    t, d = x.shape
    nt, tok, specs = _tail_specs(batch, t, attn.shape[1])
    return pl.pallas_call(
        _attn_tail_kernel,
        grid=(batch, nt),
        in_specs=specs + [_resident((1, d)), _resident(wg.shape), _resident(wu.shape), _resident(wd.shape)],
        out_specs=tok(d),
        out_shape=jax.ShapeDtypeStruct((t, d), F32),
        scratch_shapes=[pltpu.VMEM((TAIL_TILE, XQ_W), BF16), pltpu.VMEM((TAIL_TILE, D_FF), BF16)],
        compiler_params=_params(2),
        name="attn_tail",
    )(x, attn, xq, mem_kv, mem_kv, wa, wc, g.reshape(1, d), wg, wu, wd)


TOP_K = 2
HI16 = 0xFFFF0000
META_EXPERT, META_POS, META_GATE = 0, 2, 4


def _pack_bf16_pairs(h):
    n = h.shape[1] // 2
    u = pltpu.bitcast(h.astype(BF16).astype(F32), jnp.uint32)
    return (u[:, :n] >> 16) | (u[:, n:] & jnp.uint32(HI16))


def _unpack_bf16_pairs(w):
    lo = pltpu.bitcast(w << 16, F32).astype(BF16)
    hi = pltpu.bitcast(w & jnp.uint32(HI16), F32).astype(BF16)
    return jnp.concatenate([lo, hi], axis=1)


def _route(x1, g_ref, wr_ref, meta_ref, meta_t_ref, cnt_ref, hpk_ref, run_scr):
    h = _rms(x1, g_ref[...])
    hpk_ref[...] = _pack_bf16_pairs(h)
    h1, h2 = _split_bf16(h, 2)
    prod = _dot(h1, wr_ref[...]) + _dot(h2, wr_ref[...])
    logits = prod[:, :LANES] + prod[:, LANES:]
    tm = h.shape[0]
    lt = logits.T[:N_EXPERTS, :]
    sub = lax.broadcasted_iota(jnp.int32, lt.shape, 0)
    m1 = jnp.max(lt, axis=0, keepdims=True)
    i1 = jnp.min(jnp.where(lt == m1, sub, N_EXPERTS), axis=0, keepdims=True)
    lt2 = jnp.where(sub == i1, NEG_INF, lt)
    m2 = jnp.max(lt2, axis=0, keepdims=True)
    i2 = jnp.min(jnp.where(lt2 == m2, sub, N_EXPERTS), axis=0, keepdims=True)
    e = jnp.exp(m2 - m1)
    g1, g2 = 1.0 / (1.0 + e), e / (1.0 + e)

    onehot = jnp.where(jnp.logical_or(sub == i1, sub == i2), 1.0, 0.0)
    onehot16 = jnp.concatenate([onehot, jnp.zeros_like(onehot)], axis=0).astype(BF16)
    earlier = lax.broadcasted_iota(jnp.int32, (tm, tm), 0) < lax.broadcasted_iota(jnp.int32, (tm, tm), 1)
    before = _dot(onehot16, earlier.astype(BF16))[:N_EXPERTS, :] + run_scr[:, 0:1]
    pos1 = jnp.sum(jnp.where(sub == i1, before, 0.0), axis=0, keepdims=True)
    pos2 = jnp.sum(jnp.where(sub == i2, before, 0.0), axis=0, keepdims=True)
    run_scr[...] += jnp.sum(onehot, axis=1, keepdims=True)
    cnt_ref[...] = run_scr[...]

    def put(k, v):
        return jnp.where(sub == k, v, 0.0)

    meta_t = (put(META_EXPERT, i1.astype(F32)) + put(META_EXPERT + 1, i2.astype(F32))
              + put(META_POS, pos1) + put(META_POS + 1, pos2)
              + put(META_GATE, g1) + put(META_GATE + 1, g2))
    meta_t_ref[...] = meta_t
    meta_ref[...] = jnp.concatenate([meta_t, jnp.zeros((LANES - N_EXPERTS, tm), F32)], axis=0).T


def _ssd_tail_kernel(x_ref, y_ref, xq_ref, mk_ref, mv_ref, wy_ref, wc_ref, g_ref, wr_ref,
                     x1_ref, meta_ref, meta_t_ref, cnt_ref, hpk_ref, cross_scr, run_scr):
    @pl.when((pl.program_id(0) == 0) & (pl.program_id(1) == 0))
    def _():
        run_scr[...] = jnp.zeros_like(run_scr)

    _cross_attention(xq_ref, mk_ref, mv_ref, cross_scr)
    x1 = x_ref[...] + _dot(y_ref[...], wy_ref[...]) + _dot(cross_scr[...], wc_ref[...])
    x1_ref[...] = x1
    _route(x1, g_ref, wr_ref, meta_ref, meta_t_ref, cnt_ref, hpk_ref, run_scr)


def _ssd_tail(x, y, xq, mem_kv, wy, wc, g, w_router, batch):
    t, d = x.shape
    nt, tok, specs = _tail_specs(batch, t, y.shape[1])
    w1, w2 = _split_bf16(w_router.astype(F32), 2)
    wr = jnp.zeros((d, 2 * LANES), BF16).at[:, :N_EXPERTS].set(w1).at[:, LANES:LANES + N_EXPERTS].set(w2)
    return pl.pallas_call(
        _ssd_tail_kernel,
        grid=(batch, nt),
        in_specs=specs + [_resident((1, d)), _resident((d, 2 * LANES))],
        out_specs=[tok(d), tok(LANES), pl.BlockSpec((N_EXPERTS, TAIL_TILE), lambda b, i: (0, b * nt + i)),
                   pl.BlockSpec((N_EXPERTS, LANES), lambda b, i: (0, 0)), tok(d // 2)],
        out_shape=[jax.ShapeDtypeStruct((t, d), F32), jax.ShapeDtypeStruct((t, LANES), F32),
                   jax.ShapeDtypeStruct((N_EXPERTS, t), F32), jax.ShapeDtypeStruct((N_EXPERTS, LANES), F32),
                   jax.ShapeDtypeStruct((t, d // 2), jnp.uint32)],
        scratch_shapes=[pltpu.VMEM((TAIL_TILE, XQ_W), BF16), pltpu.VMEM((N_EXPERTS, LANES), F32)],
        compiler_params=_params(2),
        name="ssd_tail",
    )(x, y, xq, mem_kv, mem_kv, wy, wc, g.reshape(1, d), wr)


ROW_TILE = 512
TOK_TILE = 512


def _row_copy_all_wait(src_like, dst_like, sem):
    pltpu.make_async_copy(src_like, dst_like, sem).wait()


def _dispatch_kernel(dest_hbm, hpk_ref, xs_in, xs_out, idx_smem, idx_sem, row_sem):
    del xs_in
    tm = hpk_ref.shape[0]
    idx_copy = pltpu.make_async_copy(dest_hbm.at[pl.program_id(0)], idx_smem, idx_sem)
    idx_copy.start()
    idx_copy.wait()

    def issue(t, carry):
        for k in range(TOP_K):
            row = idx_smem[k * tm + t]
            pltpu.make_async_copy(hpk_ref.at[pl.ds(t, 1)], xs_out.at[pl.ds(row, 1)], row_sem).start(priority=k)
        return carry

    lax.fori_loop(0, tm, issue, 0, unroll=8)
    for _ in range(TOP_K):
        _row_copy_all_wait(hpk_ref, xs_out.at[pl.ds(0, tm)], row_sem)


def _dispatch(dest_tiles, hpk, n_rows):
    t, w = hpk.shape
    tm = TOK_TILE
    any_spec = pl.BlockSpec(memory_space=pl.ANY)
    return pl.pallas_call(
        _dispatch_kernel,
        grid=(t // tm,),
        in_specs=[any_spec, pl.BlockSpec((tm, w), lambda i: (i, 0)), any_spec],
        out_specs=any_spec,
        out_shape=jax.ShapeDtypeStruct((n_rows, w), jnp.uint32),
        scratch_shapes=[pltpu.SMEM((TOP_K * tm,), jnp.int32), pltpu.SemaphoreType.DMA,
                        pltpu.SemaphoreType.DMA],
        input_output_aliases={2: 0},
        compiler_params=_params(1),
        name="dispatch",
    )(dest_tiles, hpk, jnp.zeros((n_rows, w), jnp.uint32))


def _experts_kernel(tile_expert_ref, n_used_ref, xs_ref, wg_ref, wu_ref, wd_ref, y_ref, act_scr):
    del tile_expert_ref

    @pl.when(pl.program_id(0) < n_used_ref[0])
    def _():
        h = _unpack_bf16_pairs(xs_ref[...])
        for c0 in range(0, D_FF, FF_CHUNK):
            sl = slice(c0, c0 + FF_CHUNK)
            act_scr[:, sl] = (_silu(_dot(h, wg_ref[0, :, sl])) * _dot(h, wu_ref[0, :, sl])).astype(BF16)
        y_ref[...] = _dot(act_scr[...], wd_ref[0])

    @pl.when(pl.program_id(0) >= n_used_ref[0])
    def _():
        y_ref[...] = jnp.zeros_like(y_ref)


def _experts(tile_expert, n_used, xs, wg, wu, wd):
    n_rows, w = xs.shape
    d = 2 * w
    tm = ROW_TILE
    wspec = lambda shape: pl.BlockSpec((1,) + shape, lambda i, te, nu: (te[i], 0, 0))
    return pl.pallas_call(
        _experts_kernel,
        grid_spec=pltpu.PrefetchScalarGridSpec(
            num_scalar_prefetch=2,
            grid=(n_rows // tm,),
            in_specs=[pl.BlockSpec((tm, w), lambda i, te, nu: (i, 0)),
                      wspec((d, D_FF)), wspec((d, D_FF)), wspec((D_FF, d))],
            out_specs=pl.BlockSpec((tm, d), lambda i, te, nu: (i, 0)),
            scratch_shapes=[pltpu.VMEM((tm, D_FF), BF16)]),
        out_shape=jax.ShapeDtypeStruct((n_rows, d), F32),
        compiler_params=_params(1),
        name="experts",
    )(tile_expert, n_used, xs, wg, wu, wd)


def _combine_kernel(dest_hbm, x_ref, meta_ref, y_hbm, gf_ref, o_ref, idx_smem, y_buf, idx_sem, row_sem, *,
                    final_norm):
    tm = x_ref.shape[0]
    idx_copy = pltpu.make_async_copy(dest_hbm.at[pl.program_id(0)], idx_smem, idx_sem)
    idx_copy.start()
    idx_copy.wait()

    def issue(t, carry):
        for k in range(TOP_K):
            row = idx_smem[k * tm + t]
            pltpu.make_async_copy(y_hbm.at[pl.ds(row, 1)], y_buf.at[k, pl.ds(t, 1)], row_sem).start(priority=k)
        return carry

    lax.fori_loop(0, tm, issue, 0, unroll=8)
    for k in range(TOP_K):
        _row_copy_all_wait(y_hbm.at[pl.ds(0, tm)], y_buf.at[k], row_sem)
    meta = meta_ref[...]
    out = x_ref[...]
    for k in range(TOP_K):
        out = out + meta[:, META_GATE + k:META_GATE + k + 1] * y_buf[k]
    o_ref[...] = _rms(out, gf_ref[...]) if final_norm else out


def _combine(dest_tiles, x, meta, y, g_final):
    t, d = x.shape
    tm = TOK_TILE
    any_spec = pl.BlockSpec(memory_space=pl.ANY)
    final_norm = g_final is not None
    gf = (g_final if final_norm else jnp.ones((d,), F32)).reshape(1, d)
    return pl.pallas_call(
        functools.partial(_combine_kernel, final_norm=final_norm),
        grid=(t // tm,),
        in_specs=[any_spec, pl.BlockSpec((tm, d), lambda i: (i, 0)),
                  pl.BlockSpec((tm, LANES), lambda i: (i, 0)), any_spec, _resident((1, d))],
        out_specs=pl.BlockSpec((tm, d), lambda i: (i, 0)),
        out_shape=jax.ShapeDtypeStruct((t, d), F32),
        scratch_shapes=[pltpu.SMEM((TOP_K * tm,), jnp.int32), pltpu.VMEM((TOP_K, tm, d), F32),
                        pltpu.SemaphoreType.DMA, pltpu.SemaphoreType.DMA],
        compiler_params=_params(1),
        name="combine",
    )(dest_tiles, x, meta, y, gf)


def _moe(x, meta, meta_t, counts, hpk, wg, wu, wd, g_final=None):
    t, d = x.shape
    cnt = counts[:, 0].astype(jnp.int32)
    padded = (cnt + ROW_TILE - 1) // ROW_TILE * ROW_TILE
    ends = jnp.cumsum(padded)
    starts = ends - padded
    expert = meta_t[META_EXPERT:META_EXPERT + TOP_K].astype(jnp.int32)
    dest = meta_t[META_POS:META_POS + TOP_K].astype(jnp.int32)
    for e in range(N_EXPERTS):
        dest = dest + jnp.where(expert == e, starts[e], 0)
    dest_tiles = dest.reshape(TOP_K, t // TOK_TILE, TOK_TILE).transpose(1, 0, 2).reshape(-1, TOP_K * TOK_TILE)
    n_rows = TOP_K * t + N_EXPERTS * ROW_TILE
    n_tiles = n_rows // ROW_TILE
    n_used = ends[-1] // ROW_TILE
    tile_start = jnp.minimum(jnp.arange(n_tiles), n_used - 1) * ROW_TILE
    tile_expert = jnp.sum(tile_start[:, None] >= ends[None, :], axis=1).astype(jnp.int32)
    xs = _dispatch(dest_tiles, hpk, n_rows)
    y = _experts(tile_expert, n_used.reshape(1).astype(jnp.int32), xs, wg, wu, wd)
    return _combine(dest_tiles, x, meta, y, g_final)


CONV_TAIL = 16


def _conv_shift_matrix():
    m = np.zeros(((CONV_WIDTH - 1) * CHUNK, CONV_TAIL + CHUNK), np.float32)
    for k in range(CONV_WIDTH - 1):
        for t in range(CHUNK):
            m[k * CHUNK + t, CONV_TAIL + t - (CONV_WIDTH - 1) + k] = 1.0
    return m


def _split_bf16(v, parts):
    out, r = [], v
    for _ in range(parts):
        p = r.astype(BF16)
        out.append(p)
        r = r - p.astype(F32)
    return out


def _ssd_kernel(xbc_ref, z_ref, dt_ref, cw_ref, cb_ref, dtb_ref, alog_ref, dsk_ref, gn_ref, exp_ref, shift_ref,
                o_ref, state_scr, ext_scr, y_scr):
    @pl.when(pl.program_id(1) == 0)
    def _():
        state_scr[...] = jnp.zeros_like(state_scr)
        ext_scr[CHUNK:, :] = jnp.zeros((CONV_TAIL, CONV_CH), BF16)

    ext_scr[0:CONV_TAIL, :] = ext_scr[CHUNK:, :]
    ext_scr[CONV_TAIL:, :] = xbc_ref[...]

    def conv_silu(c0, c1):
        shifted = _dot(shift_ref[...], ext_scr[:, c0:c1])
        acc = cb_ref[:, c0:c1] + cw_ref[CONV_WIDTH - 1:CONV_WIDTH, c0:c1] * ext_scr[CONV_TAIL:, c0:c1].astype(F32)
        for k in range(CONV_WIDTH - 1):
            acc = acc + cw_ref[k:k + 1, c0:c1] * shifted[k * CHUNK:(k + 1) * CHUNK, :]
        return _silu(acc)

    dt = dt_ref[...] + dtb_ref[...]
    dt = jnp.maximum(dt, 0.0) + jnp.log(1.0 + jnp.exp(-jnp.abs(dt)))
    da = dt * -jnp.exp(alog_ref[...])
    row = lax.broadcasted_iota(jnp.int32, (CHUNK, CHUNK), 0)
    col = lax.broadcasted_iota(jnp.int32, (CHUNK, CHUNK), 1)
    causal = col <= row
    cs = _dot(causal.astype(BF16), jnp.concatenate(_split_bf16(da, 3), axis=1))
    acum = cs[:, 0:LANES] + cs[:, LANES:2 * LANES] + cs[:, 2 * LANES:]
    acum_t = acum.T
    ea = jnp.exp(acum)
    dtte = dt * jnp.exp(acum[CHUNK - 1:CHUNK, :] - acum)

    def expand(v, g):
        return _dot(jnp.concatenate(_split_bf16(v, 2), axis=1), exp_ref[:, g * GROUP_W:(g + 1) * GROUP_W])

    low = lax.broadcasted_iota(jnp.int32, (CHUNK, LANES), 1) < SSM_HEAD_DIM
    for g in range(N_SSM_GROUPS):
        gsl = slice(g * GROUP_W, (g + 1) * GROUP_W)
        xg = conv_silu(g * GROUP_W, (g + 1) * GROUP_W)
        bg = conv_silu(D_INNER + g * D_STATE, D_INNER + (g + 1) * D_STATE)
        cg = conv_silu(D_INNER + BC_W + g * D_STATE, D_INNER + BC_W + (g + 1) * D_STATE).astype(BF16)
        ea_x = expand(ea, g)
        xdt = (xg * expand(dt, g)).astype(BF16)
        cb = lax.dot_general(cg, bg.astype(BF16), NT_DIMS, preferred_element_type=F32)
        state = state_scr[g]
        y = _dot(cg, state.astype(BF16)) * ea_x + xg * dsk_ref[:, gsl]
        diag = []
        for pair in range(HEADS_PER_GROUP // 2):
            xp = xdt[:, pair * LANES:(pair + 1) * LANES]
            acc = jnp.zeros((CHUNK, LANES), F32)
            for half in range(2):
                h = g * HEADS_PER_GROUP + 2 * pair + half
                seg = acum[:, h:h + 1] - acum_t[h:h + 1, :]
                m = (cb * jnp.exp(jnp.where(causal, seg, NEG_INF))).astype(BF16)
                xm = jnp.where(low if half == 0 else jnp.logical_not(low), xp, jnp.zeros_like(xp))
                acc = acc + _dot(m, xm)
            diag.append(acc)
        y_scr[:, gsl] = y + jnp.concatenate(diag, axis=1)
        w = (xg * expand(dtte, g)).astype(BF16)
        state_scr[g] = state * ea_x[CHUNK - 1:CHUNK, :] + _dot(bg.T.astype(BF16), w)

    z = z_ref[...].astype(F32)
    o_ref[...] = _rms(y_scr[...] * _silu(z), gn_ref[...]).astype(o_ref.dtype)


def _ssd(xbc, z, dt_raw, conv_w, conv_b, dt_bias, a_log, d_skip, g_norm, batch):
    t = xbc.shape[0]
    nc = t // batch // CHUNK

    def lane_pad(v):
        return jnp.zeros((1, LANES), F32).at[0, :N_SSM_HEADS].set(v)

    expand = np.zeros((LANES, D_INNER), np.float32)
    for h in range(N_SSM_HEADS):
        expand[h, h * SSM_HEAD_DIM:(h + 1) * SSM_HEAD_DIM] = 1.0
    expand2 = jnp.asarray(np.concatenate([expand, expand], axis=0), BF16)
    tok = lambda b, c: (b * nc + c, 0)
    return pl.pallas_call(
        _ssd_kernel,
        grid=(batch, nc),
        in_specs=[pl.BlockSpec((CHUNK, CONV_CH), tok), pl.BlockSpec((CHUNK, D_INNER), tok),
                  pl.BlockSpec((CHUNK, LANES), tok),
                  _resident((CONV_WIDTH, CONV_CH)), _resident((1, CONV_CH)),
                  _resident((1, LANES)), _resident((1, LANES)),
                  _resident((1, D_INNER)), _resident((1, D_INNER)), _resident((2 * LANES, D_INNER)),
                  _resident(((CONV_WIDTH - 1) * CHUNK, CONV_TAIL + CHUNK))],
        out_specs=pl.BlockSpec((CHUNK, D_INNER), tok),
        out_shape=jax.ShapeDtypeStruct((t, D_INNER), BF16),
        scratch_shapes=[pltpu.VMEM((N_SSM_GROUPS, D_STATE, GROUP_W), F32),
                        pltpu.VMEM((CONV_TAIL + CHUNK, CONV_CH), BF16),
                        pltpu.VMEM((CHUNK, D_INNER), F32)],
        compiler_params=_params(2),
        name="ssd",
    )(xbc, z, dt_raw, conv_w, conv_b.reshape(1, CONV_CH), lane_pad(dt_bias), lane_pad(a_log),
      jnp.repeat(d_skip, SSM_HEAD_DIM).reshape(1, D_INNER), g_norm.reshape(1, D_INNER), expand2,
      jnp.asarray(_conv_shift_matrix(), BF16))


def kernel(x, mem, g_mix, g_ffn, g_mem, w_mem_kv, rel_bias, swa_w_in, swa_sinks, swa_w_out, ssm_w_in, ssm_conv_w, ssm_conv_b, ssm_dt_bias, ssm_A_log, ssm_D, ssm_g_norm, ssm_w_out, ffn_w_gate, ffn_w_up, ffn_w_down, moe_w_router, moe_w_gate, moe_w_up, moe_w_down, g_final):
    batch, seq, d = x.shape
    xf = x.reshape(batch * seq, d)
    memf = mem.reshape(batch * MEM_LEN, d)
    for i in range(DEPTH):
        j = i // 2
        (mem_kv,) = _norm_proj(memf, g_mem[i], [w_mem_kv[i].astype(BF16)], [BF16])
        if i % 2 == 0:
            w_in = swa_w_in[j].astype(BF16)
            q, kv, xq = _norm_proj(
                xf, g_mix[i], [w_in[:, :Q_W], w_in[:, Q_W:Q_W + 2 * KV_W], w_in[:, Q_W + 2 * KV_W:]],
                [BF16, BF16, BF16])
            attn = _swa(q, kv, rel_bias, swa_sinks[j], batch)
            w_out = swa_w_out[j].astype(BF16)
            xf = _attn_tail(xf, attn, xq, mem_kv, w_out[:Q_W], w_out[Q_W:], g_ffn[i],
                            ffn_w_gate[j].astype(BF16), ffn_w_up[j].astype(BF16), ffn_w_down[j].astype(BF16),
                            batch)
        else:
            w_in = ssm_w_in[j].astype(BF16)
            o_dt = D_INNER + CONV_CH
            w_dt = jnp.zeros((d, LANES), BF16).at[:, :N_SSM_HEADS].set(w_in[:, o_dt:o_dt + N_SSM_HEADS])
            z, xbc, dt_raw, xq = _norm_proj(
                xf, g_mix[i], [w_in[:, :D_INNER], w_in[:, D_INNER:o_dt], w_dt, w_in[:, o_dt + N_SSM_HEADS:]],
                [BF16, BF16, F32, BF16])
            y = _ssd(xbc, z, dt_raw, ssm_conv_w[j], ssm_conv_b[j], ssm_dt_bias[j], ssm_A_log[j],
                     ssm_D[j], ssm_g_norm[j], batch)
            w_out = ssm_w_out[j].astype(BF16)
            xf, meta, meta_t, counts, hpk = _ssd_tail(xf, y, xq, mem_kv, w_out[:D_INNER], w_out[D_INNER:],
                                                      g_ffn[i], moe_w_router[j], batch)
            xf = _moe(xf, meta, meta_t, counts, hpk, moe_w_gate[j].astype(BF16), moe_w_up[j].astype(BF16),
                      moe_w_down[j].astype(BF16), g_final if i == DEPTH - 1 else None)
    assert DEPTH % 2 == 0
    return xf.reshape(batch, seq, d)
```

```python
import functools
import math

import numpy as np
import jax
import jax.numpy as jnp
from jax import lax
from jax.experimental import pallas as pl
from jax.experimental.pallas import tpu as pltpu

F32 = jnp.float32
BF16 = jnp.bfloat16

D_MODEL = 1024
DEPTH = 4
MEM_LEN = 256
EPS = 1e-6
N_Q_HEADS = 16
N_KV_HEADS = 2
HEAD_DIM = 64
BLOCK = 128
N_BUCKETS = 32
MAX_DISTANCE = 128
N_X_HEADS = 4
X_HEAD_DIM = 256
D_INNER = 2048
SSM_HEAD_DIM = 64
N_SSM_HEADS = 32
N_SSM_GROUPS = 4
HEADS_PER_GROUP = 8
D_STATE = 128
CONV_WIDTH = 4
CHUNK = 128
D_FF = 2816
N_EXPERTS = 8
Q_W = N_Q_HEADS * HEAD_DIM
KV_W = N_KV_HEADS * HEAD_DIM
XQ_W = N_X_HEADS * X_HEAD_DIM
BC_W = N_SSM_GROUPS * D_STATE
CONV_CH = D_INNER + 2 * BC_W
GROUP_W = HEADS_PER_GROUP * SSM_HEAD_DIM

LANES = 128
VMEM_LIMIT = 56 << 20
NEG_INF = float("-inf")
NT_DIMS = (((1,), (1,)), ((), ()))


def _params(n_axes, vmem=VMEM_LIMIT):
    return pltpu.CompilerParams(dimension_semantics=("arbitrary",) * n_axes, vmem_limit_bytes=vmem)


def _resident(shape):
    nd = len(shape)
    return pl.BlockSpec(shape, lambda *_: (0,) * nd, pipeline_mode=pl.Buffered(1))


def _dot(a, b):
    return jnp.dot(a, b, preferred_element_type=F32)


def _rms(x, g):
    return x * lax.rsqrt(jnp.mean(x * x, axis=-1, keepdims=True) + EPS) * g


def _silu(v):
    return v / (1.0 + jnp.exp(-v))


def _norm_proj_kernel(x_ref, g_ref, *refs, n_out, col_chunk):
    w_refs, o_refs = refs[:n_out], refs[n_out:]
    h = _rms(x_ref[...], g_ref[...]).astype(BF16)
    for w_ref, o_ref in zip(w_refs, o_refs):
        n = w_ref.shape[1]
        for c0 in range(0, n, col_chunk):
            c1 = min(c0 + col_chunk, n)
            o_ref[:, c0:c1] = _dot(h, w_ref[:, c0:c1]).astype(o_ref.dtype)


def _norm_proj(x, g, ws, out_dtypes, tm=512):
    t, d = x.shape
    n_out = len(ws)
    return pl.pallas_call(
        functools.partial(_norm_proj_kernel, n_out=n_out, col_chunk=512),
        grid=(t // tm,),
        in_specs=[pl.BlockSpec((tm, d), lambda i: (i, 0)), _resident((1, d))]
        + [_resident(w.shape) for w in ws],
        out_specs=[pl.BlockSpec((tm, w.shape[1]), lambda i: (i, 0)) for w in ws],
        out_shape=[jax.ShapeDtypeStruct((t, w.shape[1]), dt) for w, dt in zip(ws, out_dtypes)],
        compiler_params=_params(1),
        name="norm_proj",
    )(x, g.reshape(1, d), *ws)


def _bucket_table():
    qi = np.arange(BLOCK)[:, None]
    kj = np.arange(2 * BLOCK)[None, :]
    dist = BLOCK + qi - kj
    max_exact = N_BUCKETS // 2
    d = np.maximum(dist, 0)
    df = np.maximum(d, 1).astype(np.float32)
    far = max_exact + (
        np.log(df / np.float32(max_exact)) / np.float32(math.log(MAX_DISTANCE / max_exact))
        * np.float32(N_BUCKETS - max_exact)
    ).astype(np.int32)
    bucket = np.where(d < max_exact, d, np.minimum(far, N_BUCKETS - 1))
    r = np.arange(BLOCK)[:, None]
    c = np.arange(BLOCK)[None, :]
    return np.where(c <= r, bucket[:, BLOCK:], bucket[:, :BLOCK]).astype(np.int32)


PAIRS_PER_KV = N_Q_HEADS // N_KV_HEADS // 2
SLOT_ROWS = PAIRS_PER_KV * BLOCK


def _swa_slot_head(slot):
    kvh, rest = divmod(slot, 2 * PAIRS_PER_KV)
    half, pair = divmod(rest, PAIRS_PER_KV)
    return 2 * (kvh * PAIRS_PER_KV + pair) + half


def _swa_kernel(relb_ref, bucket_ref, sink_ref, q_ref, kvp_ref, kvc_ref, o_ref, bias_scr, s_scr, p_scr):
    first = (pl.program_id(0) == 0) & (pl.program_id(1) == 0)

    @pl.when(first)
    def _():
        bucket = bucket_ref[...]
        for slot in range(N_Q_HEADS):
            h = _swa_slot_head(slot)
            acc = jnp.zeros((BLOCK, BLOCK), F32)
            for n in range(N_BUCKETS):
                acc = jnp.where(bucket == n, relb_ref[n, h], acc)
            bias_scr[slot * BLOCK:(slot + 1) * BLOCK, :] = acc

    kk = jnp.concatenate([kvp_ref[:, 0:KV_W], kvc_ref[:, 0:KV_W]], axis=0).astype(F32)
    vv = jnp.concatenate([kvp_ref[:, KV_W:], kvc_ref[:, KV_W:]], axis=0).astype(F32)
    low = lax.broadcasted_iota(jnp.int32, kk.shape, 1) < HEAD_DIM

    def placed(t):
        r = pltpu.roll(t, HEAD_DIM, 1)
        return [[jnp.where(low, t, 0.0).astype(BF16), jnp.where(low, 0.0, r).astype(BF16)],
                [jnp.where(low, r, 0.0).astype(BF16), jnp.where(low, 0.0, t).astype(BF16)]]

    k_var, v_var = placed(kk), placed(vv)

    def from_current(n_rows):
        r = lax.broadcasted_iota(jnp.int32, (n_rows, BLOCK), 0) & (BLOCK - 1)
        return lax.broadcasted_iota(jnp.int32, (n_rows, BLOCK), 1) <= r

    cur_slot = from_current(SLOT_ROWS)
    for v in range(2 * N_KV_HEADS):
        kvh, half = divmod(v, 2)
        q_stack = jnp.concatenate(
            [q_ref[:, (kvh * PAIRS_PER_KV + j) * LANES:(kvh * PAIRS_PER_KV + j + 1) * LANES]
             for j in range(PAIRS_PER_KV)], axis=0)
        s = lax.dot_general(q_stack, k_var[kvh][half], NT_DIMS, preferred_element_type=F32)
        s_scr[v * SLOT_ROWS:(v + 1) * SLOT_ROWS, :] = jnp.where(cur_slot, s[:, BLOCK:], s[:, :BLOCK])

    cur_all = from_current(N_Q_HEADS * BLOCK)
    key_exists = jnp.logical_or(cur_all, pl.program_id(1) > 0)
    s = s_scr[...] * (HEAD_DIM ** -0.5) + bias_scr[...]
    s = jnp.where(key_exists, s, NEG_INF)
    sink = sink_ref[...]
    m = jnp.maximum(jnp.max(s, axis=-1, keepdims=True), sink)
    p = jnp.exp(s - m)
    p = p / (jnp.sum(p, axis=-1, keepdims=True) + jnp.exp(sink - m))
    p_scr[...] = p.astype(BF16)

    outs = []
    for v in range(2 * N_KV_HEADS):
        kvh, half = divmod(v, 2)
        p = p_scr[v * SLOT_ROWS:(v + 1) * SLOT_ROWS, :]
        zero = jnp.zeros_like(p)
        p_band = jnp.concatenate([jnp.where(cur_slot, zero, p), jnp.where(cur_slot, p, zero)], axis=1)
        outs.append(_dot(p_band, v_var[kvh][half]))
    for kvh in range(N_KV_HEADS):
        for j in range(PAIRS_PER_KV):
            pair = kvh * PAIRS_PER_KV + j
            out = outs[2 * kvh][j * BLOCK:(j + 1) * BLOCK, :] + outs[2 * kvh + 1][j * BLOCK:(j + 1) * BLOCK, :]
            o_ref[:, pair * LANES:(pair + 1) * LANES] = out.astype(o_ref.dtype)


def _swa(q, kv, rel_bias, sinks, batch):
    t = q.shape[0]
    nb = t // batch // BLOCK
    rows = N_Q_HEADS * BLOCK
    bucket = jnp.asarray(_bucket_table())
    slot_sinks = sinks.astype(F32)[np.array([_swa_slot_head(s) for s in range(N_Q_HEADS)])]
    sink_rows = jnp.broadcast_to(jnp.repeat(slot_sinks, BLOCK)[:, None], (rows, LANES))
    smem = pl.BlockSpec(memory_space=pltpu.SMEM)
    return pl.pallas_call(
        _swa_kernel,
        grid=(batch, nb),
        in_specs=[smem, _resident((BLOCK, BLOCK)), _resident((rows, LANES)),
                  pl.BlockSpec((BLOCK, Q_W), lambda b, i: (b * nb + i, 0)),
                  pl.BlockSpec((BLOCK, 2 * KV_W), lambda b, i: (b * nb + jnp.maximum(i - 1, 0), 0)),
                  pl.BlockSpec((BLOCK, 2 * KV_W), lambda b, i: (b * nb + i, 0))],
        out_specs=pl.BlockSpec((BLOCK, Q_W), lambda b, i: (b * nb + i, 0)),
        out_shape=jax.ShapeDtypeStruct((t, Q_W), BF16),
        scratch_shapes=[pltpu.VMEM((rows, BLOCK), F32), pltpu.VMEM((rows, BLOCK), F32),
                        pltpu.VMEM((rows, BLOCK), BF16)],
        compiler_params=_params(2),
        name="swa",
    )(rel_bias, bucket, sink_rows, q, kv, kv)


TAIL_TILE = 512
FF_CHUNK = 256


def _cross_attention(xq_ref, mk_ref, mv_ref, cross_scr):
    for h in range(N_X_HEADS):
        sl = slice(h * X_HEAD_DIM, (h + 1) * X_HEAD_DIM)
        s = lax.dot_general(xq_ref[:, sl], mk_ref[:, sl], NT_DIMS, preferred_element_type=F32)
        s = s * (X_HEAD_DIM ** -0.5)
        p = jnp.exp(s - jnp.max(s, axis=-1, keepdims=True))
        p = p / jnp.sum(p, axis=-1, keepdims=True)
        cross_scr[:, sl] = _dot(p.astype(BF16), mv_ref[:, sl]).astype(cross_scr.dtype)


def _swiglu_residual(x1, g_ref, wg_ref, wu_ref, wd_ref, act_scr):
    h = _rms(x1, g_ref[...]).astype(BF16)
    for c0 in range(0, D_FF, FF_CHUNK):
        sl = slice(c0, c0 + FF_CHUNK)
        act_scr[:, sl] = (_silu(_dot(h, wg_ref[:, sl])) * _dot(h, wu_ref[:, sl])).astype(BF16)
    return x1 + _dot(act_scr[...], wd_ref[...])


def _tail_specs(batch, t, a_width, tile=TAIL_TILE):
    nt = t // batch // tile
    tok = lambda w: pl.BlockSpec((tile, w), lambda b, i: (b * nt + i, 0))
    mem = lambda half: pl.BlockSpec((MEM_LEN, XQ_W), lambda b, i: (b, half))
    return nt, tok, [tok(D_MODEL), tok(a_width), tok(XQ_W), mem(0), mem(1),
                     _resident((a_width, D_MODEL)), _resident((XQ_W, D_MODEL))]


def _attn_tail_kernel(x_ref, a_ref, xq_ref, mk_ref, mv_ref, wa_ref, wc_ref, g_ref, wg_ref, wu_ref, wd_ref,
                      o_ref, cross_scr, act_scr):
    _cross_attention(xq_ref, mk_ref, mv_ref, cross_scr)
    x1 = x_ref[...] + _dot(a_ref[...], wa_ref[...]) + _dot(cross_scr[...], wc_ref[...])
    o_ref[...] = _swiglu_residual(x1, g_ref, wg_ref, wu_ref, wd_ref, act_scr)


def _attn_tail(x, attn, xq, mem_kv, wa, wc, g, wg, wu, wd, batch):
    t, d = x.shape
    nt, tok, specs = _tail_specs(batch, t, attn.shape[1])
    return pl.pallas_call(
        _attn_tail_kernel,
        grid=(batch, nt),
        in_specs=specs + [_resident((1, d)), _resident(wg.shape), _resident(wu.shape), _resident(wd.shape)],
        out_specs=tok(d),
        out_shape=jax.ShapeDtypeStruct((t, d), F32),
        scratch_shapes=[pltpu.VMEM((TAIL_TILE, XQ_W), BF16), pltpu.VMEM((TAIL_TILE, D_FF), BF16)],
        compiler_params=_params(2),
        name="attn_tail",
    )(x, attn, xq, mem_kv, mem_kv, wa, wc, g.reshape(1, d), wg, wu, wd)


TOP_K = 2
HI16 = 0xFFFF0000
META_EXPERT, META_POS, META_GATE = 0, 2, 4


def _pack_bf16_pairs(h):
    n = h.shape[1] // 2
    u = pltpu.bitcast(h.astype(BF16).astype(F32), jnp.uint32)
    return (u[:, :n] >> 16) | (u[:, n:] & jnp.uint32(HI16))


def _unpack_bf16_pairs(w):
    lo = pltpu.bitcast(w << 16, F32).astype(BF16)
    hi = pltpu.bitcast(w & jnp.uint32(HI16), F32).astype(BF16)
    return jnp.concatenate([lo, hi], axis=1)


def _route(x1, g_ref, wr_ref, meta_ref, meta_t_ref, cnt_ref, hpk_ref, run_scr):
    h = _rms(x1, g_ref[...])
    hpk_ref[...] = _pack_bf16_pairs(h)
    h1, h2 = _split_bf16(h, 2)
    prod = _dot(h1, wr_ref[...]) + _dot(h2, wr_ref[...])
    logits = prod[:, :LANES] + prod[:, LANES:]
    tm = h.shape[0]
    lt = logits.T[:N_EXPERTS, :]
    sub = lax.broadcasted_iota(jnp.int32, lt.shape, 0)
    m1 = jnp.max(lt, axis=0, keepdims=True)
    i1 = jnp.min(jnp.where(lt == m1, sub, N_EXPERTS), axis=0, keepdims=True)
    lt2 = jnp.where(sub == i1, NEG_INF, lt)
    m2 = jnp.max(lt2, axis=0, keepdims=True)
    i2 = jnp.min(jnp.where(lt2 == m2, sub, N_EXPERTS), axis=0, keepdims=True)
    e = jnp.exp(m2 - m1)
    g1, g2 = 1.0 / (1.0 + e), e / (1.0 + e)

    onehot = jnp.where(jnp.logical_or(sub == i1, sub == i2), 1.0, 0.0)
    onehot16 = jnp.concatenate([onehot, jnp.zeros_like(onehot)], axis=0).astype(BF16)
    earlier = lax.broadcasted_iota(jnp.int32, (tm, tm), 0) < lax.broadcasted_iota(jnp.int32, (tm, tm), 1)
    before = _dot(onehot16, earlier.astype(BF16))[:N_EXPERTS, :] + run_scr[:, 0:1]
    pos1 = jnp.sum(jnp.where(sub == i1, before, 0.0), axis=0, keepdims=True)
    pos2 = jnp.sum(jnp.where(sub == i2, before, 0.0), axis=0, keepdims=True)
    run_scr[...] += jnp.sum(onehot, axis=1, keepdims=True)
    cnt_ref[...] = run_scr[...]

    def put(k, v):
        return jnp.where(sub == k, v, 0.0)

    meta_t = (put(META_EXPERT, i1.astype(F32)) + put(META_EXPERT + 1, i2.astype(F32))
              + put(META_POS, pos1) + put(META_POS + 1, pos2)
              + put(META_GATE, g1) + put(META_GATE + 1, g2))
    meta_t_ref[...] = meta_t
    meta_ref[...] = jnp.concatenate([meta_t, jnp.zeros((LANES - N_EXPERTS, tm), F32)], axis=0).T


def _ssd_tail_kernel(x_ref, y_ref, xq_ref, mk_ref, mv_ref, wy_ref, wc_ref, g_ref, wr_ref,
                     x1_ref, meta_ref, meta_t_ref, cnt_ref, hpk_ref, run_scr, *cross_scrs):
    @pl.when((pl.program_id(0) == 0) & (pl.program_id(1) == 0))
    def _():
        run_scr[...] = jnp.zeros_like(run_scr)

    for s in range(x_ref.shape[0] // TAIL_TILE):
        rows = pl.ds(s * TAIL_TILE, TAIL_TILE)
        _cross_attention(xq_ref.at[rows], mk_ref, mv_ref, cross_scrs[s])
        x1 = x_ref[rows, :] + _dot(y_ref[rows, :], wy_ref[...]) + _dot(cross_scrs[s][...], wc_ref[...])
        x1_ref[rows, :] = x1
        _route(x1, g_ref, wr_ref, meta_ref.at[rows], meta_t_ref.at[:, rows], cnt_ref, hpk_ref.at[rows], run_scr)


SSD_TAIL_SUBTILES = 2


def _ssd_tail(x, y, xq, mem_kv, wy, wc, g, w_router, batch):
    t, d = x.shape
    tile = SSD_TAIL_SUBTILES * TAIL_TILE
    nt, tok, specs = _tail_specs(batch, t, y.shape[1], tile)
    w1, w2 = _split_bf16(w_router.astype(F32), 2)
    wr = jnp.zeros((d, 2 * LANES), BF16).at[:, :N_EXPERTS].set(w1).at[:, LANES:LANES + N_EXPERTS].set(w2)
    return pl.pallas_call(
        _ssd_tail_kernel,
        grid=(batch, nt),
        in_specs=specs + [_resident((1, d)), _resident((d, 2 * LANES))],
        out_specs=[tok(d), tok(LANES), pl.BlockSpec((N_EXPERTS, tile), lambda b, i: (0, b * nt + i)),
                   pl.BlockSpec((N_EXPERTS, LANES), lambda b, i: (0, 0)), tok(d // 2)],
        out_shape=[jax.ShapeDtypeStruct((t, d), F32), jax.ShapeDtypeStruct((t, LANES), F32),
                   jax.ShapeDtypeStruct((N_EXPERTS, t), F32), jax.ShapeDtypeStruct((N_EXPERTS, LANES), F32),
                   jax.ShapeDtypeStruct((t, d // 2), jnp.uint32)],
        scratch_shapes=[pltpu.VMEM((N_EXPERTS, LANES), F32)]
        + [pltpu.VMEM((TAIL_TILE, XQ_W), BF16)] * SSD_TAIL_SUBTILES,
        compiler_params=_params(2),
        name="ssd_tail",
    )(x, y, xq, mem_kv, mem_kv, wy, wc, g.reshape(1, d), wr)


ROW_TILE = 512
DISPATCH_TILE = 2048
COMBINE_TILE = 1024


def _row_copy_all_wait(src_like, dst_like, sem):
    pltpu.make_async_copy(src_like, dst_like, sem).wait()


def _dispatch_kernel(dest_hbm, hpk_ref, xs_in, xs_out, idx_smem, idx_sem, row_sem):
    del xs_in
    tm = hpk_ref.shape[0]
    idx_copy = pltpu.make_async_copy(dest_hbm.at[pl.program_id(0)], idx_smem, idx_sem)
    idx_copy.start()
    idx_copy.wait()

    def issue(t, carry):
        for k in range(TOP_K):
            row = idx_smem[k * tm + t]
            pltpu.make_async_copy(hpk_ref.at[pl.ds(t, 1)], xs_out.at[pl.ds(row, 1)], row_sem).start(priority=k)
        return carry

    lax.fori_loop(0, tm, issue, 0, unroll=8)
    for _ in range(TOP_K):
        _row_copy_all_wait(hpk_ref, xs_out.at[pl.ds(0, tm)], row_sem)


def _dispatch(dest_tiles, hpk, n_rows):
    t, w = hpk.shape
    tm = dest_tiles.shape[1] // TOP_K
    any_spec = pl.BlockSpec(memory_space=pl.ANY)
    return pl.pallas_call(
        _dispatch_kernel,
        grid=(t // tm,),
        in_specs=[any_spec, pl.BlockSpec((tm, w), lambda i: (i, 0)), any_spec],
        out_specs=any_spec,
        out_shape=jax.ShapeDtypeStruct((n_rows, w), jnp.uint32),
        scratch_shapes=[pltpu.SMEM((TOP_K * tm,), jnp.int32), pltpu.SemaphoreType.DMA,
                        pltpu.SemaphoreType.DMA],
        input_output_aliases={2: 0},
        compiler_params=_params(1),
        name="dispatch",
    )(dest_tiles, hpk, jnp.zeros((n_rows, w), jnp.uint32))


def _experts_kernel(tile_expert_ref, n_used_ref, xs_ref, wg_ref, wu_ref, wd_ref, y_ref, act_scr):
    del tile_expert_ref

    @pl.when(pl.program_id(0) < n_used_ref[0])
    def _():
        h = _unpack_bf16_pairs(xs_ref[...])
        for c0 in range(0, D_FF, FF_CHUNK):
            sl = slice(c0, c0 + FF_CHUNK)
            act_scr[:, sl] = (_silu(_dot(h, wg_ref[0, :, sl])) * _dot(h, wu_ref[0, :, sl])).astype(BF16)
        y_ref[...] = _dot(act_scr[...], wd_ref[0])

    @pl.when(pl.program_id(0) >= n_used_ref[0])
    def _():
        y_ref[...] = jnp.zeros_like(y_ref)


def _experts(tile_expert, n_used, xs, wg, wu, wd):
    n_rows, w = xs.shape
    d = 2 * w
    tm = ROW_TILE
    wspec = lambda shape: pl.BlockSpec((1,) + shape, lambda i, te, nu: (te[i], 0, 0))
    return pl.pallas_call(
        _experts_kernel,
        grid_spec=pltpu.PrefetchScalarGridSpec(
            num_scalar_prefetch=2,
            grid=(n_rows // tm,),
            in_specs=[pl.BlockSpec((tm, w), lambda i, te, nu: (i, 0)),
                      wspec((d, D_FF)), wspec((d, D_FF)), wspec((D_FF, d))],
            out_specs=pl.BlockSpec((tm, d), lambda i, te, nu: (i, 0)),
            scratch_shapes=[pltpu.VMEM((tm, D_FF), BF16)]),
        out_shape=jax.ShapeDtypeStruct((n_rows, d), F32),
        compiler_params=_params(1),
        name="experts",
    )(tile_expert, n_used, xs, wg, wu, wd)


def _combine_kernel(dest_hbm, x_ref, meta_ref, y_hbm, gf_ref, o_ref, idx_smem, y_buf, idx_sem, row_sem, *,
                    final_norm):
    tm = x_ref.shape[0]
    idx_copy = pltpu.make_async_copy(dest_hbm.at[pl.program_id(0)], idx_smem, idx_sem)
    idx_copy.start()
    idx_copy.wait()

    def issue(t, carry):
        for k in range(TOP_K):
            row = idx_smem[k * tm + t]
            pltpu.make_async_copy(y_hbm.at[pl.ds(row, 1)], y_buf.at[k, pl.ds(t, 1)], row_sem).start(priority=k)
        return carry

    lax.fori_loop(0, tm, issue, 0, unroll=8)
    for k in range(TOP_K):
        _row_copy_all_wait(y_hbm.at[pl.ds(0, tm)], y_buf.at[k], row_sem)
    meta = meta_ref[...]
    out = x_ref[...]
    for k in range(TOP_K):
        out = out + meta[:, META_GATE + k:META_GATE + k + 1] * y_buf[k]
    o_ref[...] = _rms(out, gf_ref[...]) if final_norm else out


def _combine(dest_tiles, x, meta, y, g_final):
    t, d = x.shape
    tm = dest_tiles.shape[1] // TOP_K
    any_spec = pl.BlockSpec(memory_space=pl.ANY)
    final_norm = g_final is not None
    gf = (g_final if final_norm else jnp.ones((d,), F32)).reshape(1, d)
    return pl.pallas_call(
        functools.partial(_combine_kernel, final_norm=final_norm),
        grid=(t // tm,),
        in_specs=[any_spec, pl.BlockSpec((tm, d), lambda i: (i, 0)),
                  pl.BlockSpec((tm, LANES), lambda i: (i, 0)), any_spec, _resident((1, d))],
        out_specs=pl.BlockSpec((tm, d), lambda i: (i, 0)),
        out_shape=jax.ShapeDtypeStruct((t, d), F32),
        scratch_shapes=[pltpu.SMEM((TOP_K * tm,), jnp.int32), pltpu.VMEM((TOP_K, tm, d), F32),
                        pltpu.SemaphoreType.DMA, pltpu.SemaphoreType.DMA],
        compiler_params=_params(1),
        name="combine",
    )(dest_tiles, x, meta, y, gf)


def _moe(x, meta, meta_t, counts, hpk, wg, wu, wd, g_final=None):
    t, d = x.shape
    cnt = counts[:, 0].astype(jnp.int32)
    padded = (cnt + ROW_TILE - 1) // ROW_TILE * ROW_TILE
    ends = jnp.cumsum(padded)
    starts = ends - padded
    expert = meta_t[META_EXPERT:META_EXPERT + TOP_K].astype(jnp.int32)
    dest = meta_t[META_POS:META_POS + TOP_K].astype(jnp.int32)
    for e in range(N_EXPERTS):
        dest = dest + jnp.where(expert == e, starts[e], 0)

    def dest_tiles(tile):
        return dest.reshape(TOP_K, t // tile, tile).transpose(1, 0, 2).reshape(-1, TOP_K * tile)

    n_rows = TOP_K * t + N_EXPERTS * ROW_TILE
    n_tiles = n_rows // ROW_TILE
    n_used = ends[-1] // ROW_TILE
    tile_start = jnp.minimum(jnp.arange(n_tiles), n_used - 1) * ROW_TILE
    tile_expert = jnp.sum(tile_start[:, None] >= ends[None, :], axis=1).astype(jnp.int32)
    xs = _dispatch(dest_tiles(DISPATCH_TILE), hpk, n_rows)
    y = _experts(tile_expert, n_used.reshape(1).astype(jnp.int32), xs, wg, wu, wd)
    return _combine(dest_tiles(COMBINE_TILE), x, meta, y, g_final)


CONV_TAIL = 16


def _conv_shift_matrix():
    m = np.zeros(((CONV_WIDTH - 1) * CHUNK, CONV_TAIL + CHUNK), np.float32)
    for k in range(CONV_WIDTH - 1):
        for t in range(CHUNK):
            m[k * CHUNK + t, CONV_TAIL + t - (CONV_WIDTH - 1) + k] = 1.0
    return m


def _split_bf16(v, parts):
    out, r = [], v
    for _ in range(parts):
        p = r.astype(BF16)
        out.append(p)
        r = r - p.astype(F32)
    return out


def _ssd_kernel(xbc_ref, z_ref, dt_ref, cw_ref, cb_ref, dtb_ref, alog_ref, dsk_ref, gn_ref, exp_ref, shift_ref,
                o_ref, state_scr, *chunk_scr):
    n_sub = xbc_ref.shape[0] // CHUNK
    ext_scrs, y_scrs = chunk_scr[:n_sub], chunk_scr[n_sub:]

    @pl.when(pl.program_id(1) == 0)
    def _():
        state_scr[...] = jnp.zeros_like(state_scr)
        ext_scrs[n_sub - 1][CHUNK:, :] = jnp.zeros((CONV_TAIL, CONV_CH), BF16)

    for c in range(n_sub):
        rows = pl.ds(c * CHUNK, CHUNK)
        _ssd_chunk(xbc_ref.at[rows], z_ref.at[rows], dt_ref.at[rows], cw_ref, cb_ref, dtb_ref, alog_ref, dsk_ref,
                   gn_ref, exp_ref, shift_ref, o_ref.at[rows], state_scr, ext_scrs[c],
                   ext_scrs[(c - 1) % n_sub], y_scrs[c])


def _ssd_chunk(xbc_ref, z_ref, dt_ref, cw_ref, cb_ref, dtb_ref, alog_ref, dsk_ref, gn_ref, exp_ref, shift_ref,
               o_ref, state_scr, ext_scr, prev_ext_scr, y_scr):
    ext_scr[0:CONV_TAIL, :] = prev_ext_scr[CHUNK:, :]
    ext_scr[CONV_TAIL:, :] = xbc_ref[...]

    def conv_silu(c0, c1):
        shifted = _dot(shift_ref[...], ext_scr[:, c0:c1])
        acc = cb_ref[:, c0:c1] + cw_ref[CONV_WIDTH - 1:CONV_WIDTH, c0:c1] * ext_scr[CONV_TAIL:, c0:c1].astype(F32)
        for k in range(CONV_WIDTH - 1):
            acc = acc + cw_ref[k:k + 1, c0:c1] * shifted[k * CHUNK:(k + 1) * CHUNK, :]
        return _silu(acc)

    dt = dt_ref[...] + dtb_ref[...]
    dt = jnp.maximum(dt, 0.0) + jnp.log(1.0 + jnp.exp(-jnp.abs(dt)))
    da = dt * -jnp.exp(alog_ref[...])
    row = lax.broadcasted_iota(jnp.int32, (CHUNK, CHUNK), 0)
    col = lax.broadcasted_iota(jnp.int32, (CHUNK, CHUNK), 1)
    causal = col <= row
    cs = _dot(causal.astype(BF16), jnp.concatenate(_split_bf16(da, 3), axis=1))
    acum = cs[:, 0:LANES] + cs[:, LANES:2 * LANES] + cs[:, 2 * LANES:]
    acum_t = acum.T
    ea = jnp.exp(acum)
    dtte = dt * jnp.exp(acum[CHUNK - 1:CHUNK, :] - acum)

    def expand(v, g):
        return _dot(jnp.concatenate(_split_bf16(v, 2), axis=1), exp_ref[:, g * GROUP_W:(g + 1) * GROUP_W])

    low = lax.broadcasted_iota(jnp.int32, (CHUNK, LANES), 1) < SSM_HEAD_DIM
    for g in range(N_SSM_GROUPS):
        gsl = slice(g * GROUP_W, (g + 1) * GROUP_W)
        xg = conv_silu(g * GROUP_W, (g + 1) * GROUP_W)
        bg = conv_silu(D_INNER + g * D_STATE, D_INNER + (g + 1) * D_STATE)
        cg = conv_silu(D_INNER + BC_W + g * D_STATE, D_INNER + BC_W + (g + 1) * D_STATE).astype(BF16)
        ea_x = expand(ea, g)
        xdt = (xg * expand(dt, g)).astype(BF16)
        cb = lax.dot_general(cg, bg.astype(BF16), NT_DIMS, preferred_element_type=F32)
        state = state_scr[g]
        y = _dot(cg, state.astype(BF16)) * ea_x + xg * dsk_ref[:, gsl]
        diag = []
        for pair in range(HEADS_PER_GROUP // 2):
            xp = xdt[:, pair * LANES:(pair + 1) * LANES]
            acc = jnp.zeros((CHUNK, LANES), F32)
            for half in range(2):
                h = g * HEADS_PER_GROUP + 2 * pair + half
                seg = acum[:, h:h + 1] - acum_t[h:h + 1, :]
                m = (cb * jnp.exp(jnp.where(causal, seg, NEG_INF))).astype(BF16)
                xm = jnp.where(low if half == 0 else jnp.logical_not(low), xp, jnp.zeros_like(xp))
                acc = acc + _dot(m, xm)
            diag.append(acc)
        y_scr[:, gsl] = y + jnp.concatenate(diag, axis=1)
        w = (xg * expand(dtte, g)).astype(BF16)
        state_scr[g] = state * ea_x[CHUNK - 1:CHUNK, :] + _dot(bg.T.astype(BF16), w)

    z = z_ref[...].astype(F32)
    o_ref[...] = _rms(y_scr[...] * _silu(z), gn_ref[...]).astype(o_ref.dtype)


SSD_SUBCHUNKS = 2


def _ssd(xbc, z, dt_raw, conv_w, conv_b, dt_bias, a_log, d_skip, g_norm, batch):
    t = xbc.shape[0]
    step = SSD_SUBCHUNKS * CHUNK
    nc = t // batch // step

    def lane_pad(v):
        return jnp.zeros((1, LANES), F32).at[0, :N_SSM_HEADS].set(v)

    expand = np.zeros((LANES, D_INNER), np.float32)
    for h in range(N_SSM_HEADS):
        expand[h, h * SSM_HEAD_DIM:(h + 1) * SSM_HEAD_DIM] = 1.0
    expand2 = jnp.asarray(np.concatenate([expand, expand], axis=0), BF16)
    tok = lambda b, c: (b * nc + c, 0)
    return pl.pallas_call(
        _ssd_kernel,
        grid=(batch, nc),
        in_specs=[pl.BlockSpec((step, CONV_CH), tok), pl.BlockSpec((step, D_INNER), tok),
                  pl.BlockSpec((step, LANES), tok),
                  _resident((CONV_WIDTH, CONV_CH)), _resident((1, CONV_CH)),
                  _resident((1, LANES)), _resident((1, LANES)),
                  _resident((1, D_INNER)), _resident((1, D_INNER)), _resident((2 * LANES, D_INNER)),
                  _resident(((CONV_WIDTH - 1) * CHUNK, CONV_TAIL + CHUNK))],
        out_specs=pl.BlockSpec((step, D_INNER), tok),
        out_shape=jax.ShapeDtypeStruct((t, D_INNER), BF16),
        scratch_shapes=[pltpu.VMEM((N_SSM_GROUPS, D_STATE, GROUP_W), F32)]
        + [pltpu.VMEM((CONV_TAIL + CHUNK, CONV_CH), BF16)] * SSD_SUBCHUNKS
        + [pltpu.VMEM((CHUNK, D_INNER), F32)] * SSD_SUBCHUNKS,
        compiler_params=_params(2),
        name="ssd",
    )(xbc, z, dt_raw, conv_w, conv_b.reshape(1, CONV_CH), lane_pad(dt_bias), lane_pad(a_log),
      jnp.repeat(d_skip, SSM_HEAD_DIM).reshape(1, D_INNER), g_norm.reshape(1, D_INNER), expand2,
      jnp.asarray(_conv_shift_matrix(), BF16))


def kernel(x, mem, g_mix, g_ffn, g_mem, w_mem_kv, rel_bias, swa_w_in, swa_sinks, swa_w_out, ssm_w_in, ssm_conv_w, ssm_conv_b, ssm_dt_bias, ssm_A_log, ssm_D, ssm_g_norm, ssm_w_out, ffn_w_gate, ffn_w_up, ffn_w_down, moe_w_router, moe_w_gate, moe_w_up, moe_w_down, g_final):
    batch, seq, d = x.shape
    xf = x.reshape(batch * seq, d)
    memf = mem.reshape(batch * MEM_LEN, d)
    for i in range(DEPTH):
        j = i // 2
        (mem_kv,) = _norm_proj(memf, g_mem[i], [w_mem_kv[i].astype(BF16)], [BF16])
        if i % 2 == 0:
            w_in = swa_w_in[j].astype(BF16)
            q, kv, xq = _norm_proj(
                xf, g_mix[i], [w_in[:, :Q_W], w_in[:, Q_W:Q_W + 2 * KV_W], w_in[:, Q_W + 2 * KV_W:]],
                [BF16, BF16, BF16])
            attn = _swa(q, kv, rel_bias, swa_sinks[j], batch)
            w_out = swa_w_out[j].astype(BF16)
            xf = _attn_tail(xf, attn, xq, mem_kv, w_out[:Q_W], w_out[Q_W:], g_ffn[i],
                            ffn_w_gate[j].astype(BF16), ffn_w_up[j].astype(BF16), ffn_w_down[j].astype(BF16),
                            batch)
        else:
            w_in = ssm_w_in[j].astype(BF16)
            o_dt = D_INNER + CONV_CH
            w_dt = jnp.zeros((d, LANES), BF16).at[:, :N_SSM_HEADS].set(w_in[:, o_dt:o_dt + N_SSM_HEADS])
            z, xbc, dt_raw, xq = _norm_proj(
                xf, g_mix[i], [w_in[:, :D_INNER], w_in[:, D_INNER:o_dt], w_dt, w_in[:, o_dt + N_SSM_HEADS:]],
                [BF16, BF16, F32, BF16])
            y = _ssd(xbc, z, dt_raw, ssm_conv_w[j], ssm_conv_b[j], ssm_dt_bias[j], ssm_A_log[j],
                     ssm_D[j], ssm_g_norm[j], batch)
            w_out = ssm_w_out[j].astype(BF16)
            xf, meta, meta_t, counts, hpk = _ssd_tail(xf, y, xq, mem_kv, w_out[:D_INNER], w_out[D_INNER:],
                                                      g_ffn[i], moe_w_router[j], batch)
            xf = _moe(xf, meta, meta_t, counts, hpk, moe_w_gate[j].astype(BF16), moe_w_up[j].astype(BF16),
                      moe_w_down[j].astype(BF16), g_final if i == DEPTH - 1 else None)
    assert DEPTH % 2 == 0
    return xf.reshape(batch, seq, d)
```

```python
import functools
import math

import numpy as np
import jax
import jax.numpy as jnp
from jax import lax
from jax.experimental import pallas as pl
from jax.experimental.pallas import tpu as pltpu

F32 = jnp.float32
BF16 = jnp.bfloat16

D_MODEL = 1024
DEPTH = 4
MEM_LEN = 256
EPS = 1e-6
N_Q_HEADS = 16
N_KV_HEADS = 2
HEAD_DIM = 64
BLOCK = 128
N_BUCKETS = 32
MAX_DISTANCE = 128
N_X_HEADS = 4
X_HEAD_DIM = 256
D_INNER = 2048
SSM_HEAD_DIM = 64
N_SSM_HEADS = 32
N_SSM_GROUPS = 4
HEADS_PER_GROUP = 8
D_STATE = 128
CONV_WIDTH = 4
CHUNK = 128
D_FF = 2816
N_EXPERTS = 8
Q_W = N_Q_HEADS * HEAD_DIM
KV_W = N_KV_HEADS * HEAD_DIM
XQ_W = N_X_HEADS * X_HEAD_DIM
BC_W = N_SSM_GROUPS * D_STATE
CONV_CH = D_INNER + 2 * BC_W
GROUP_W = HEADS_PER_GROUP * SSM_HEAD_DIM

LANES = 128
SUBLANES = 8
VMEM_LIMIT = 56 << 20
NEG_INF = float("-inf")
NT_DIMS = (((1,), (1,)), ((), ()))


def _params(n_axes, vmem=VMEM_LIMIT):
    return pltpu.CompilerParams(dimension_semantics=("arbitrary",) * n_axes, vmem_limit_bytes=vmem)


def _resident(shape):
    nd = len(shape)
    return pl.BlockSpec(shape, lambda *_: (0,) * nd, pipeline_mode=pl.Buffered(1))


def _dot(a, b):
    return jnp.dot(a, b, preferred_element_type=F32)


def _rms(x, g):
    return x * lax.rsqrt(jnp.mean(x * x, axis=-1, keepdims=True) + EPS) * g


def _silu(v):
    return v / (1.0 + jnp.exp(-v))


def _norm_proj_kernel(x_ref, g_ref, *refs, n_out, col_chunk):
    w_refs, o_refs = refs[:n_out], refs[n_out:]
    h = _rms(x_ref[...], g_ref[...]).astype(BF16)
    for w_ref, o_ref in zip(w_refs, o_refs):
        n = w_ref.shape[1]
        for c0 in range(0, n, col_chunk):
            c1 = min(c0 + col_chunk, n)
            o_ref[:, c0:c1] = _dot(h, w_ref[:, c0:c1]).astype(o_ref.dtype)


def _norm_proj(x, g, ws, out_dtypes, tm=512):
    t, d = x.shape
    n_out = len(ws)
    return pl.pallas_call(
        functools.partial(_norm_proj_kernel, n_out=n_out, col_chunk=512),
        grid=(t // tm,),
        in_specs=[pl.BlockSpec((tm, d), lambda i: (i, 0)), _resident((1, d))]
        + [_resident(w.shape) for w in ws],
        out_specs=[pl.BlockSpec((tm, w.shape[1]), lambda i: (i, 0)) for w in ws],
        out_shape=[jax.ShapeDtypeStruct((t, w.shape[1]), dt) for w, dt in zip(ws, out_dtypes)],
        compiler_params=_params(1),
        name="norm_proj",
    )(x, g.reshape(1, d), *ws)


def _bucket_table():
    qi = np.arange(BLOCK)[:, None]
    kj = np.arange(2 * BLOCK)[None, :]
    dist = BLOCK + qi - kj
    max_exact = N_BUCKETS // 2
    d = np.maximum(dist, 0)
    df = np.maximum(d, 1).astype(np.float32)
    far = max_exact + (
        np.log(df / np.float32(max_exact)) / np.float32(math.log(MAX_DISTANCE / max_exact))
        * np.float32(N_BUCKETS - max_exact)
    ).astype(np.int32)
    bucket = np.where(d < max_exact, d, np.minimum(far, N_BUCKETS - 1))
    r = np.arange(BLOCK)[:, None]
    c = np.arange(BLOCK)[None, :]
    return np.where(c <= r, bucket[:, BLOCK:], bucket[:, :BLOCK]).astype(np.int32)


PAIRS_PER_KV = N_Q_HEADS // N_KV_HEADS // 2
SLOT_ROWS = PAIRS_PER_KV * BLOCK


def _swa_slot_head(slot):
    kvh, rest = divmod(slot, 2 * PAIRS_PER_KV)
    half, pair = divmod(rest, PAIRS_PER_KV)
    return 2 * (kvh * PAIRS_PER_KV + pair) + half


def _swa_kernel(relb_ref, bucket_ref, sink_ref, q_ref, kvp_ref, kvc_ref, o_ref, bias_scr, s_scr, p_scr):
    first = (pl.program_id(0) == 0) & (pl.program_id(1) == 0)

    def from_current(n_rows):
        r = lax.broadcasted_iota(jnp.int32, (n_rows, BLOCK), 0) & (BLOCK - 1)
        return lax.broadcasted_iota(jnp.int32, (n_rows, BLOCK), 1) <= r

    @pl.when(first)
    def _():
        bucket = bucket_ref[...]
        for slot in range(N_Q_HEADS):
            h = _swa_slot_head(slot)
            acc = jnp.zeros((BLOCK, BLOCK), F32)
            for n in range(N_BUCKETS):
                acc = jnp.where(bucket == n, relb_ref[n, h], acc)
            rows = slice(slot * BLOCK, (slot + 1) * BLOCK)
            bias_scr[1, rows, :] = acc
            bias_scr[0, rows, :] = jnp.where(from_current(BLOCK), acc, NEG_INF)

    kk = jnp.concatenate([kvp_ref[:, 0:KV_W], kvc_ref[:, 0:KV_W]], axis=0).astype(F32)
    vv = jnp.concatenate([kvp_ref[:, KV_W:], kvc_ref[:, KV_W:]], axis=0).astype(F32)
    low = lax.broadcasted_iota(jnp.int32, kk.shape, 1) < HEAD_DIM

    def placed(t):
        r = pltpu.roll(t, HEAD_DIM, 1)
        return [[jnp.where(low, t, 0.0).astype(BF16), jnp.where(low, 0.0, r).astype(BF16)],
                [jnp.where(low, r, 0.0).astype(BF16), jnp.where(low, 0.0, t).astype(BF16)]]

    k_var, v_var = placed(kk), placed(vv)

    cur_slot = from_current(SLOT_ROWS)
    for v in range(2 * N_KV_HEADS):
        kvh, half = divmod(v, 2)
        q_stack = jnp.concatenate(
            [q_ref[:, (kvh * PAIRS_PER_KV + j) * LANES:(kvh * PAIRS_PER_KV + j + 1) * LANES]
             for j in range(PAIRS_PER_KV)], axis=0)
        s = lax.dot_general(q_stack, k_var[kvh][half], NT_DIMS, preferred_element_type=F32)
        s_scr[v * SLOT_ROWS:(v + 1) * SLOT_ROWS, :] = jnp.where(cur_slot, s[:, BLOCK:], s[:, :BLOCK])

    s = s_scr[...] + bias_scr[jnp.minimum(pl.program_id(1), 1)]
    sink = sink_ref[...]
    m = jnp.maximum(jnp.max(s, axis=-1, keepdims=True), sink)
    p = jnp.exp(s - m)
    p = p / (jnp.sum(p, axis=-1, keepdims=True) + jnp.exp(sink - m))
    p_scr[...] = p.astype(BF16)

    outs = []
    for v in range(2 * N_KV_HEADS):
        kvh, half = divmod(v, 2)
        p = p_scr[v * SLOT_ROWS:(v + 1) * SLOT_ROWS, :]
        zero = jnp.zeros_like(p)
        p_band = jnp.concatenate([jnp.where(cur_slot, zero, p), jnp.where(cur_slot, p, zero)], axis=1)
        outs.append(_dot(p_band, v_var[kvh][half]))
    for kvh in range(N_KV_HEADS):
        for j in range(PAIRS_PER_KV):
            pair = kvh * PAIRS_PER_KV + j
            out = outs[2 * kvh][j * BLOCK:(j + 1) * BLOCK, :] + outs[2 * kvh + 1][j * BLOCK:(j + 1) * BLOCK, :]
            o_ref[:, pair * LANES:(pair + 1) * LANES] = out.astype(o_ref.dtype)


def _swa(q, kv, rel_bias, sinks, batch):
    t = q.shape[0]
    nb = t // batch // BLOCK
    rows = N_Q_HEADS * BLOCK
    bucket = jnp.asarray(_bucket_table())
    slot_sinks = sinks.astype(F32)[np.array([_swa_slot_head(s) for s in range(N_Q_HEADS)])]
    sink_rows = jnp.broadcast_to(jnp.repeat(slot_sinks, BLOCK)[:, None], (rows, LANES))
    smem = pl.BlockSpec(memory_space=pltpu.SMEM)
    return pl.pallas_call(
        _swa_kernel,
        grid=(batch, nb),
        in_specs=[smem, _resident((BLOCK, BLOCK)), _resident((rows, LANES)),
                  pl.BlockSpec((BLOCK, Q_W), lambda b, i: (b * nb + i, 0)),
                  pl.BlockSpec((BLOCK, 2 * KV_W), lambda b, i: (b * nb + jnp.maximum(i - 1, 0), 0)),
                  pl.BlockSpec((BLOCK, 2 * KV_W), lambda b, i: (b * nb + i, 0))],
        out_specs=pl.BlockSpec((BLOCK, Q_W), lambda b, i: (b * nb + i, 0)),
        out_shape=jax.ShapeDtypeStruct((t, Q_W), BF16),
        scratch_shapes=[pltpu.VMEM((2, rows, BLOCK), F32), pltpu.VMEM((rows, BLOCK), F32),
                        pltpu.VMEM((rows, BLOCK), BF16)],
        compiler_params=_params(2),
        name="swa",
    )(rel_bias, bucket, sink_rows, q, kv, kv)


TAIL_TILE = 512
FF_CHUNK = 256


def _cross_attention(xq_ref, mk_ref, mv_ref, cross_scr):
    for h in range(N_X_HEADS):
        sl = slice(h * X_HEAD_DIM, (h + 1) * X_HEAD_DIM)
        s = lax.dot_general(xq_ref[:, sl], mk_ref[:, sl], NT_DIMS, preferred_element_type=F32)
        s = s * (X_HEAD_DIM ** -0.5)
        p = jnp.exp(s - jnp.max(s, axis=-1, keepdims=True))
        p = p / jnp.sum(p, axis=-1, keepdims=True)
        cross_scr[:, sl] = _dot(p.astype(BF16), mv_ref[:, sl]).astype(cross_scr.dtype)


def _swiglu_residual(x1, g_ref, wg_ref, wu_ref, wd_ref, act_scr):
    h = _rms(x1, g_ref[...]).astype(BF16)
    for c0 in range(0, D_FF, FF_CHUNK):
        sl = slice(c0, c0 + FF_CHUNK)
        act_scr[:, sl] = (_silu(_dot(h, wg_ref[:, sl])) * _dot(h, wu_ref[:, sl])).astype(BF16)
    return x1 + _dot(act_scr[...], wd_ref[...])


def _tail_specs(batch, t, a_width, tile=TAIL_TILE):
    nt = t // batch // tile
    tok = lambda w: pl.BlockSpec((tile, w), lambda b, i: (b * nt + i, 0))
    mem = lambda half: pl.BlockSpec((MEM_LEN, XQ_W), lambda b, i: (b, half))
    return nt, tok, [tok(D_MODEL), tok(a_width), tok(XQ_W), mem(0), mem(1),
                     _resident((a_width, D_MODEL)), _resident((XQ_W, D_MODEL))]


def _attn_tail_kernel(x_ref, a_ref, xq_ref, mk_ref, mv_ref, wa_ref, wc_ref, g_ref, wg_ref, wu_ref, wd_ref,
                      o_ref, cross_scr, act_scr):
    _cross_attention(xq_ref, mk_ref, mv_ref, cross_scr)
    x1 = x_ref[...] + _dot(a_ref[...], wa_ref[...]) + _dot(cross_scr[...], wc_ref[...])
    o_ref[...] = _swiglu_residual(x1, g_ref, wg_ref, wu_ref, wd_ref, act_scr)


def _attn_tail(x, attn, xq, mem_kv, wa, wc, g, wg, wu, wd, batch):
    t, d = x.shape
    nt, tok, specs = _tail_specs(batch, t, attn.shape[1])
    return pl.pallas_call(
        _attn_tail_kernel,
        grid=(batch, nt),
        in_specs=specs + [_resident((1, d)), _resident(wg.shape), _resident(wu.shape), _resident(wd.shape)],
        out_specs=tok(d),
        out_shape=jax.ShapeDtypeStruct((t, d), F32),
        scratch_shapes=[pltpu.VMEM((TAIL_TILE, XQ_W), BF16), pltpu.VMEM((TAIL_TILE, D_FF), BF16)],
        compiler_params=_params(2),
        name="attn_tail",
    )(x, attn, xq, mem_kv, mem_kv, wa, wc, g.reshape(1, d), wg, wu, wd)


TOP_K = 2
HI16 = 0xFFFF0000
META_EXPERT, META_POS, META_GATE = 0, 2, 4


def _pack_bf16_pairs(h):
    n = h.shape[1] // 2
    u = pltpu.bitcast(h.astype(BF16).astype(F32), jnp.uint32)
    return (u[:, :n] >> 16) | (u[:, n:] & jnp.uint32(HI16))


def _unpack_bf16_pairs(w):
    lo = pltpu.bitcast(w << 16, F32).astype(BF16)
    hi = pltpu.bitcast(w & jnp.uint32(HI16), F32).astype(BF16)
    return jnp.concatenate([lo, hi], axis=1)


def _route(x1, g_ref, wr_ref, meta_ref, meta_t_ref, cnt_ref, hpk_ref, run_scr):
    h = _rms(x1, g_ref[...])
    hpk_ref[...] = _pack_bf16_pairs(h)
    h1, h2 = _split_bf16(h, 2)
    prod = _dot(h1, wr_ref[...]) + _dot(h2, wr_ref[...])
    logits = prod[:, :LANES] + prod[:, LANES:]
    tm = h.shape[0]
    lt = logits.T[:N_EXPERTS, :]
    sub = lax.broadcasted_iota(jnp.int32, lt.shape, 0)
    m1 = jnp.max(lt, axis=0, keepdims=True)
    i1 = jnp.min(jnp.where(lt == m1, sub, N_EXPERTS), axis=0, keepdims=True)
    lt2 = jnp.where(sub == i1, NEG_INF, lt)
    m2 = jnp.max(lt2, axis=0, keepdims=True)
    i2 = jnp.min(jnp.where(lt2 == m2, sub, N_EXPERTS), axis=0, keepdims=True)
    e = jnp.exp(m2 - m1)
    g1, g2 = 1.0 / (1.0 + e), e / (1.0 + e)

    onehot = jnp.where(jnp.logical_or(sub == i1, sub == i2), 1.0, 0.0)
    onehot16 = jnp.concatenate([onehot, jnp.zeros_like(onehot)], axis=0).astype(BF16)
    earlier = lax.broadcasted_iota(jnp.int32, (tm, tm), 0) < lax.broadcasted_iota(jnp.int32, (tm, tm), 1)
    before = _dot(onehot16, earlier.astype(BF16))[:N_EXPERTS, :] + run_scr[:, 0:1]
    pos1 = jnp.sum(jnp.where(sub == i1, before, 0.0), axis=0, keepdims=True)
    pos2 = jnp.sum(jnp.where(sub == i2, before, 0.0), axis=0, keepdims=True)
    run_scr[...] += jnp.sum(onehot, axis=1, keepdims=True)
    cnt_ref[...] = run_scr[...]

    def put(k, v):
        return jnp.where(sub == k, v, 0.0)

    meta_t = (put(META_EXPERT, i1.astype(F32)) + put(META_EXPERT + 1, i2.astype(F32))
              + put(META_POS, pos1) + put(META_POS + 1, pos2)
              + put(META_GATE, g1) + put(META_GATE + 1, g2))
    meta_t_ref[...] = meta_t
    meta_ref[...] = jnp.concatenate([meta_t, jnp.zeros((LANES - N_EXPERTS, tm), F32)], axis=0).T


def _ssd_tail_kernel(x_ref, y_ref, xq_ref, mk_ref, mv_ref, wy_ref, wc_ref, g_ref, wr_ref,
                     x1_ref, meta_ref, meta_t_ref, cnt_ref, hpk_ref, run_scr, *cross_scrs):
    @pl.when((pl.program_id(0) == 0) & (pl.program_id(1) == 0))
    def _():
        run_scr[...] = jnp.zeros_like(run_scr)

    for s in range(x_ref.shape[0] // TAIL_TILE):
        rows = pl.ds(s * TAIL_TILE, TAIL_TILE)
        _cross_attention(xq_ref.at[rows], mk_ref, mv_ref, cross_scrs[s])
        x1 = x_ref[rows, :] + _dot(y_ref[rows, :], wy_ref[...]) + _dot(cross_scrs[s][...], wc_ref[...])
        x1_ref[rows, :] = x1
        _route(x1, g_ref, wr_ref, meta_ref.at[rows], meta_t_ref.at[:, rows], cnt_ref, hpk_ref.at[rows], run_scr)


SSD_TAIL_SUBTILES = 2


def _ssd_tail(x, y, xq, mem_kv, wy, wc, g, w_router, batch):
    t, d = x.shape
    tile = SSD_TAIL_SUBTILES * TAIL_TILE
    nt, tok, specs = _tail_specs(batch, t, y.shape[1], tile)
    w1, w2 = _split_bf16(w_router.astype(F32), 2)
    wr = jnp.zeros((d, 2 * LANES), BF16).at[:, :N_EXPERTS].set(w1).at[:, LANES:LANES + N_EXPERTS].set(w2)
    return pl.pallas_call(
        _ssd_tail_kernel,
        grid=(batch, nt),
        in_specs=specs + [_resident((1, d)), _resident((d, 2 * LANES))],
        out_specs=[tok(d), tok(LANES), pl.BlockSpec((N_EXPERTS, tile), lambda b, i: (0, b * nt + i)),
                   pl.BlockSpec((N_EXPERTS, LANES), lambda b, i: (0, 0)), tok(d // 2)],
        out_shape=[jax.ShapeDtypeStruct((t, d), F32), jax.ShapeDtypeStruct((t, LANES), F32),
                   jax.ShapeDtypeStruct((N_EXPERTS, t), F32), jax.ShapeDtypeStruct((N_EXPERTS, LANES), F32),
                   jax.ShapeDtypeStruct((t, d // 2), jnp.uint32)],
        scratch_shapes=[pltpu.VMEM((N_EXPERTS, LANES), F32)]
        + [pltpu.VMEM((TAIL_TILE, XQ_W), BF16)] * SSD_TAIL_SUBTILES,
        compiler_params=_params(2),
        name="ssd_tail",
    )(x, y, xq, mem_kv, mem_kv, wy, wc, g.reshape(1, d), wr)


ROW_TILE = 512
DISPATCH_TILE = 2048
COMBINE_TILE = 1024


def _row_copy_all_wait(src_like, dst_like, sem):
    pltpu.make_async_copy(src_like, dst_like, sem).wait()


def _dispatch_kernel(pad_start_ref, dest_hbm, hpk_ref, xs_out, idx_smem, zero_buf, idx_sem, row_sem, fill_sem):
    tm = hpk_ref.shape[0]

    @pl.when(pl.program_id(0) == 0)
    def _():
        zero_buf[...] = jnp.zeros_like(zero_buf)
        fill_rows = zero_buf.shape[0]
        starts = [pad_start_ref[e] // SUBLANES * SUBLANES for e in range(N_EXPERTS)]
        last = xs_out.shape[0] - fill_rows
        n_tail = -(-(N_EXPERTS + 1) * ROW_TILE // fill_rows)
        starts += [jnp.minimum(pad_start_ref[N_EXPERTS] + k * fill_rows, last) for k in range(n_tail)]
        for s in starts:
            fill = pltpu.make_async_copy(zero_buf, xs_out.at[pl.ds(pl.multiple_of(s, SUBLANES), fill_rows)], fill_sem)
            fill.start()
            fill.wait()

    idx_copy = pltpu.make_async_copy(dest_hbm.at[pl.program_id(0)], idx_smem, idx_sem)
    idx_copy.start()
    idx_copy.wait()

    def issue(t, carry):
        for k in range(TOP_K):
            row = idx_smem[k * tm + t]
            pltpu.make_async_copy(hpk_ref.at[pl.ds(t, 1)], xs_out.at[pl.ds(row, 1)], row_sem).start(priority=k)
        return carry

    lax.fori_loop(0, tm, issue, 0, unroll=8)
    for _ in range(TOP_K):
        _row_copy_all_wait(hpk_ref, xs_out.at[pl.ds(0, tm)], row_sem)


def _dispatch(pad_start, dest_tiles, hpk, n_rows):
    t, w = hpk.shape
    tm = dest_tiles.shape[1] // TOP_K
    any_spec = pl.BlockSpec(memory_space=pl.ANY)
    return pl.pallas_call(
        _dispatch_kernel,
        grid_spec=pltpu.PrefetchScalarGridSpec(
            num_scalar_prefetch=1,
            grid=(t // tm,),
            in_specs=[any_spec, pl.BlockSpec((tm, w), lambda i, ps: (i, 0))],
            out_specs=any_spec,
            scratch_shapes=[pltpu.SMEM((TOP_K * tm,), jnp.int32), pltpu.VMEM((ROW_TILE + SUBLANES, w), jnp.uint32),
                            pltpu.SemaphoreType.DMA, pltpu.SemaphoreType.DMA, pltpu.SemaphoreType.DMA]),
        out_shape=jax.ShapeDtypeStruct((n_rows, w), jnp.uint32),
        compiler_params=_params(1),
        name="dispatch",
    )(pad_start, dest_tiles, hpk)


def _experts_kernel(tile_expert_ref, n_used_ref, xs_ref, wg_ref, wu_ref, wd_ref, y_ref, act_scr):
    del tile_expert_ref

    @pl.when(pl.program_id(0) < n_used_ref[0])
    def _():
        h = _unpack_bf16_pairs(xs_ref[...])
        for c0 in range(0, D_FF, FF_CHUNK):
            sl = slice(c0, c0 + FF_CHUNK)
            act_scr[:, sl] = (_silu(_dot(h, wg_ref[:, sl])) * _dot(h, wu_ref[:, sl])).astype(BF16)
        y_ref[...] = _dot(act_scr[...], wd_ref[...])

    @pl.when(pl.program_id(0) >= n_used_ref[0])
    def _():
        y_ref[...] = jnp.zeros_like(y_ref)


def _experts(tile_expert, n_used, xs, wg, wu, wd, layer):
    n_rows, w = xs.shape
    d = 2 * w
    tm = ROW_TILE
    wspec = lambda shape: pl.BlockSpec((None, None) + shape, lambda i, te, nu: (layer, te[i], 0, 0))
    return pl.pallas_call(
        _experts_kernel,
        grid_spec=pltpu.PrefetchScalarGridSpec(
            num_scalar_prefetch=2,
            grid=(n_rows // tm,),
            in_specs=[pl.BlockSpec((tm, w), lambda i, te, nu: (jnp.minimum(i, nu[0] - 1), 0)),
                      wspec((d, D_FF)), wspec((d, D_FF)), wspec((D_FF, d))],
            out_specs=pl.BlockSpec((tm, d), lambda i, te, nu: (i, 0)),
            scratch_shapes=[pltpu.VMEM((tm, D_FF), BF16)]),
        out_shape=jax.ShapeDtypeStruct((n_rows, d), F32),
        compiler_params=_params(1),
        name="experts",
    )(tile_expert, n_used, xs, wg, wu, wd)


def _combine_kernel(dest_hbm, x_ref, meta_ref, y_hbm, gf_ref, o_ref, idx_smem, y_buf, idx_sem, row_sem, *,
                    final_norm):
    tm = x_ref.shape[0]
    idx_copy = pltpu.make_async_copy(dest_hbm.at[pl.program_id(0)], idx_smem, idx_sem)
    idx_copy.start()
    idx_copy.wait()

    def issue(t, carry):
        for k in range(TOP_K):
            row = idx_smem[k * tm + t]
            pltpu.make_async_copy(y_hbm.at[pl.ds(row, 1)], y_buf.at[k, pl.ds(t, 1)], row_sem).start(priority=k)
        return carry

    lax.fori_loop(0, tm, issue, 0, unroll=8)
    for k in range(TOP_K):
        _row_copy_all_wait(y_hbm.at[pl.ds(0, tm)], y_buf.at[k], row_sem)
    meta = meta_ref[...]
    out = x_ref[...]
    for k in range(TOP_K):
        out = out + meta[:, META_GATE + k:META_GATE + k + 1] * y_buf[k]
    o_ref[...] = _rms(out, gf_ref[...]) if final_norm else out


def _combine(dest_tiles, x, meta, y, g_final):
    t, d = x.shape
    tm = dest_tiles.shape[1] // TOP_K
    any_spec = pl.BlockSpec(memory_space=pl.ANY)
    final_norm = g_final is not None
    gf = (g_final if final_norm else jnp.ones((d,), F32)).reshape(1, d)
    return pl.pallas_call(
        functools.partial(_combine_kernel, final_norm=final_norm),
        grid=(t // tm,),
        in_specs=[any_spec, pl.BlockSpec((tm, d), lambda i: (i, 0)),
                  pl.BlockSpec((tm, LANES), lambda i: (i, 0)), any_spec, _resident((1, d))],
        out_specs=pl.BlockSpec((tm, d), lambda i: (i, 0)),
        out_shape=jax.ShapeDtypeStruct((t, d), F32),
        scratch_shapes=[pltpu.SMEM((TOP_K * tm,), jnp.int32), pltpu.VMEM((TOP_K, tm, d), F32),
                        pltpu.SemaphoreType.DMA, pltpu.SemaphoreType.DMA],
        compiler_params=_params(1),
        name="combine",
    )(dest_tiles, x, meta, y, gf)


def _moe(x, meta, meta_t, counts, hpk, wg, wu, wd, layer, g_final=None):
    t, d = x.shape
    cnt = counts[:, 0].astype(jnp.int32)
    padded = (cnt + ROW_TILE - 1) // ROW_TILE * ROW_TILE
    ends = jnp.cumsum(padded)
    starts = ends - padded
    expert = meta_t[META_EXPERT:META_EXPERT + TOP_K].astype(jnp.int32)
    dest = meta_t[META_POS:META_POS + TOP_K].astype(jnp.int32)
    for e in range(N_EXPERTS):
        dest = dest + jnp.where(expert == e, starts[e], 0)

    def dest_tiles(tile):
        return dest.reshape(TOP_K, t // tile, tile).transpose(1, 0, 2).reshape(-1, TOP_K * tile)

    n_rows = TOP_K * t + (N_EXPERTS + 1) * ROW_TILE
    n_tiles = n_rows // ROW_TILE
    n_used = ends[-1] // ROW_TILE
    tile_start = jnp.minimum(jnp.arange(n_tiles), n_used - 1) * ROW_TILE
    tile_expert = jnp.sum(tile_start[:, None] >= ends[None, :], axis=1).astype(jnp.int32)
    pad_start = jnp.concatenate([starts + cnt, ends[-1:]])
    xs = _dispatch(pad_start, dest_tiles(DISPATCH_TILE), hpk, n_rows)
    y = _experts(tile_expert, n_used.reshape(1).astype(jnp.int32), xs, wg, wu, wd, layer)
    return _combine(dest_tiles(COMBINE_TILE), x, meta, y, g_final)


CONV_TAIL = 16


def _conv_shift_matrix():
    m = np.zeros(((CONV_WIDTH - 1) * CHUNK, CONV_TAIL + CHUNK), np.float32)
    for k in range(CONV_WIDTH - 1):
        for t in range(CHUNK):
            m[k * CHUNK + t, CONV_TAIL + t - (CONV_WIDTH - 1) + k] = 1.0
    return m


def _split_bf16(v, parts):
    out, r = [], v
    for _ in range(parts):
        p = r.astype(BF16)
        out.append(p)
        r = r - p.astype(F32)
    return out


def _ssd_kernel(xbc_ref, z_ref, dt_ref, cw_ref, cb_ref, dtb_ref, alog_ref, dsk_ref, gn_ref, exp_ref, shift_ref,
                o_ref, state_scr, *chunk_scr):
    n_sub = xbc_ref.shape[0] // CHUNK
    ext_scrs, y_scrs = chunk_scr[:n_sub], chunk_scr[n_sub:]

    @pl.when(pl.program_id(1) == 0)
    def _():
        state_scr[...] = jnp.zeros_like(state_scr)
        ext_scrs[n_sub - 1][CHUNK:, :] = jnp.zeros((CONV_TAIL, CONV_CH), BF16)

    for c in range(n_sub):
        rows = pl.ds(c * CHUNK, CHUNK)
        _ssd_chunk(xbc_ref.at[rows], z_ref.at[rows], dt_ref.at[rows], cw_ref, cb_ref, dtb_ref, alog_ref, dsk_ref,
                   gn_ref, exp_ref, shift_ref, o_ref.at[rows], state_scr, ext_scrs[c],
                   ext_scrs[(c - 1) % n_sub], y_scrs[c])


def _ssd_chunk(xbc_ref, z_ref, dt_ref, cw_ref, cb_ref, dtb_ref, alog_ref, dsk_ref, gn_ref, exp_ref, shift_ref,
               o_ref, state_scr, ext_scr, prev_ext_scr, y_scr):
    ext_scr[0:CONV_TAIL, :] = prev_ext_scr[CHUNK:, :]
    ext_scr[CONV_TAIL:, :] = xbc_ref[...]

    def conv_silu(c0, c1):
        shifted = _dot(shift_ref[...], ext_scr[:, c0:c1])
        acc = cb_ref[:, c0:c1] + cw_ref[CONV_WIDTH - 1:CONV_WIDTH, c0:c1] * ext_scr[CONV_TAIL:, c0:c1].astype(F32)
        for k in range(CONV_WIDTH - 1):
            acc = acc + cw_ref[k:k + 1, c0:c1] * shifted[k * CHUNK:(k + 1) * CHUNK, :]
        return _silu(acc)

    dt = dt_ref[...] + dtb_ref[...]
    dt = jnp.maximum(dt, 0.0) + jnp.log(1.0 + jnp.exp(-jnp.abs(dt)))
    da = dt * -jnp.exp(alog_ref[...])
    row = lax.broadcasted_iota(jnp.int32, (CHUNK, CHUNK), 0)
    col = lax.broadcasted_iota(jnp.int32, (CHUNK, CHUNK), 1)
    causal = col <= row
    cs = _dot(causal.astype(BF16), jnp.concatenate(_split_bf16(da, 3), axis=1))
    acum = cs[:, 0:LANES] + cs[:, LANES:2 * LANES] + cs[:, 2 * LANES:]
    acum_t = acum.T
    ea = jnp.exp(acum)
    dtte = dt * jnp.exp(acum[CHUNK - 1:CHUNK, :] - acum)

    def expand(v, g):
        return _dot(jnp.concatenate(_split_bf16(v, 2), axis=1), exp_ref[:, g * GROUP_W:(g + 1) * GROUP_W])

    low = lax.broadcasted_iota(jnp.int32, (CHUNK, LANES), 1) < SSM_HEAD_DIM
    for g in range(N_SSM_GROUPS):
        gsl = slice(g * GROUP_W, (g + 1) * GROUP_W)
        xg = conv_silu(g * GROUP_W, (g + 1) * GROUP_W)
        bg = conv_silu(D_INNER + g * D_STATE, D_INNER + (g + 1) * D_STATE)
        cg = conv_silu(D_INNER + BC_W + g * D_STATE, D_INNER + BC_W + (g + 1) * D_STATE).astype(BF16)
        ea_x = expand(ea, g)
        xdt = (xg * expand(dt, g)).astype(BF16)
        cb = lax.dot_general(cg, bg.astype(BF16), NT_DIMS, preferred_element_type=F32)
        state = state_scr[g]
        y = _dot(cg, state.astype(BF16)) * ea_x + xg * dsk_ref[:, gsl]
        diag = []
        for pair in range(HEADS_PER_GROUP // 2):
            xp = xdt[:, pair * LANES:(pair + 1) * LANES]
            acc = jnp.zeros((CHUNK, LANES), F32)
            for half in range(2):
                h = g * HEADS_PER_GROUP + 2 * pair + half
                seg = acum[:, h:h + 1] - acum_t[h:h + 1, :]
                m = (cb * jnp.exp(jnp.where(causal, seg, NEG_INF))).astype(BF16)
                xm = jnp.where(low if half == 0 else jnp.logical_not(low), xp, jnp.zeros_like(xp))
                acc = acc + _dot(m, xm)
            diag.append(acc)
        y_scr[:, gsl] = y + jnp.concatenate(diag, axis=1)
        w = (xg * expand(dtte, g)).astype(BF16)
        state_scr[g] = state * ea_x[CHUNK - 1:CHUNK, :] + _dot(bg.T.astype(BF16), w)

    z = z_ref[...].astype(F32)
    o_ref[...] = _rms(y_scr[...] * _silu(z), gn_ref[...]).astype(o_ref.dtype)


SSD_SUBCHUNKS = 2


def _ssd(xbc, z, dt_raw, conv_w, conv_b, dt_bias, a_log, d_skip, g_norm, batch):
    t = xbc.shape[0]
    step = SSD_SUBCHUNKS * CHUNK
    nc = t // batch // step

    def lane_pad(v):
        return jnp.zeros((1, LANES), F32).at[0, :N_SSM_HEADS].set(v)

    expand = np.zeros((LANES, D_INNER), np.float32)
    for h in range(N_SSM_HEADS):
        expand[h, h * SSM_HEAD_DIM:(h + 1) * SSM_HEAD_DIM] = 1.0
    expand2 = jnp.asarray(np.concatenate([expand, expand], axis=0), BF16)
    tok = lambda b, c: (b * nc + c, 0)
    return pl.pallas_call(
        _ssd_kernel,
        grid=(batch, nc),
        in_specs=[pl.BlockSpec((step, CONV_CH), tok), pl.BlockSpec((step, D_INNER), tok),
                  pl.BlockSpec((step, LANES), tok),
                  _resident((CONV_WIDTH, CONV_CH)), _resident((1, CONV_CH)),
                  _resident((1, LANES)), _resident((1, LANES)),
                  _resident((1, D_INNER)), _resident((1, D_INNER)), _resident((2 * LANES, D_INNER)),
                  _resident(((CONV_WIDTH - 1) * CHUNK, CONV_TAIL + CHUNK))],
        out_specs=pl.BlockSpec((step, D_INNER), tok),
        out_shape=jax.ShapeDtypeStruct((t, D_INNER), BF16),
        scratch_shapes=[pltpu.VMEM((N_SSM_GROUPS, D_STATE, GROUP_W), F32)]
        + [pltpu.VMEM((CONV_TAIL + CHUNK, CONV_CH), BF16)] * SSD_SUBCHUNKS
        + [pltpu.VMEM((CHUNK, D_INNER), F32)] * SSD_SUBCHUNKS,
        compiler_params=_params(2),
        name="ssd",
    )(xbc, z, dt_raw, conv_w, conv_b.reshape(1, CONV_CH), lane_pad(dt_bias), lane_pad(a_log),
      jnp.repeat(d_skip, SSM_HEAD_DIM).reshape(1, D_INNER), g_norm.reshape(1, D_INNER), expand2,
      jnp.asarray(_conv_shift_matrix(), BF16))


def kernel(x, mem, g_mix, g_ffn, g_mem, w_mem_kv, rel_bias, swa_w_in, swa_sinks, swa_w_out, ssm_w_in, ssm_conv_w, ssm_conv_b, ssm_dt_bias, ssm_A_log, ssm_D, ssm_g_norm, ssm_w_out, ffn_w_gate, ffn_w_up, ffn_w_down, moe_w_router, moe_w_gate, moe_w_up, moe_w_down, g_final):
    batch, seq, d = x.shape
    xf = x.reshape(batch * seq, d)
    memf = mem.reshape(batch * MEM_LEN, d)
    moe_wg, moe_wu, moe_wd = moe_w_gate.astype(BF16), moe_w_up.astype(BF16), moe_w_down.astype(BF16)
    for i in range(DEPTH):
        j = i // 2
        (mem_kv,) = _norm_proj(memf, g_mem[i], [w_mem_kv[i].astype(BF16)], [BF16])
        if i % 2 == 0:
            w_in = swa_w_in[j]
            w_q = (w_in[:, :Q_W] * HEAD_DIM ** -0.5).astype(BF16)
            q, kv, xq = _norm_proj(
                xf, g_mix[i], [w_q, w_in[:, Q_W:Q_W + 2 * KV_W].astype(BF16), w_in[:, Q_W + 2 * KV_W:].astype(BF16)],
                [BF16, BF16, BF16])
            attn = _swa(q, kv, rel_bias, swa_sinks[j], batch)
            w_out = swa_w_out[j].astype(BF16)
            xf = _attn_tail(xf, attn, xq, mem_kv, w_out[:Q_W], w_out[Q_W:], g_ffn[i],
                            ffn_w_gate[j].astype(BF16), ffn_w_up[j].astype(BF16), ffn_w_down[j].astype(BF16),
                            batch)
        else:
            w_in = ssm_w_in[j].astype(BF16)
            o_dt = D_INNER + CONV_CH
            w_dt = jnp.zeros((d, LANES), BF16).at[:, :N_SSM_HEADS].set(w_in[:, o_dt:o_dt + N_SSM_HEADS])
            z, xbc, dt_raw, xq = _norm_proj(
                xf, g_mix[i], [w_in[:, :D_INNER], w_in[:, D_INNER:o_dt], w_dt, w_in[:, o_dt + N_SSM_HEADS:]],
                [BF16, BF16, F32, BF16])
            y = _ssd(xbc, z, dt_raw, ssm_conv_w[j], ssm_conv_b[j], ssm_dt_bias[j], ssm_A_log[j],
                     ssm_D[j], ssm_g_norm[j], batch)
            w_out = ssm_w_out[j].astype(BF16)
            xf, meta, meta_t, counts, hpk = _ssd_tail(xf, y, xq, mem_kv, w_out[:D_INNER], w_out[D_INNER:],
                                                      g_ffn[i], moe_w_router[j], batch)
            xf = _moe(xf, meta, meta_t, counts, hpk, moe_wg, moe_wu, moe_wd, j,
                      g_final if i == DEPTH - 1 else None)
    assert DEPTH % 2 == 0
    return xf.reshape(batch, seq, d)
```

```python
import functools
import math

import numpy as np
import jax
import jax.numpy as jnp
from jax import lax
from jax.experimental import pallas as pl
from jax.experimental.pallas import tpu as pltpu

F32 = jnp.float32
BF16 = jnp.bfloat16

D_MODEL = 1024
DEPTH = 4
MEM_LEN = 256
EPS = 1e-6
N_Q_HEADS = 16
N_KV_HEADS = 2
HEAD_DIM = 64
BLOCK = 128
N_BUCKETS = 32
MAX_DISTANCE = 128
N_X_HEADS = 4
X_HEAD_DIM = 256
D_INNER = 2048
SSM_HEAD_DIM = 64
N_SSM_HEADS = 32
N_SSM_GROUPS = 4
HEADS_PER_GROUP = 8
D_STATE = 128
CONV_WIDTH = 4
CHUNK = 128
D_FF = 2816
N_EXPERTS = 8
Q_W = N_Q_HEADS * HEAD_DIM
KV_W = N_KV_HEADS * HEAD_DIM
XQ_W = N_X_HEADS * X_HEAD_DIM
BC_W = N_SSM_GROUPS * D_STATE
CONV_CH = D_INNER + 2 * BC_W
GROUP_W = HEADS_PER_GROUP * SSM_HEAD_DIM

LANES = 128
SUBLANES = 8
VMEM_LIMIT = 56 << 20
NEG_INF = float("-inf")
NT_DIMS = (((1,), (1,)), ((), ()))


def _params(n_axes, vmem=VMEM_LIMIT):
    return pltpu.CompilerParams(dimension_semantics=("arbitrary",) * n_axes, vmem_limit_bytes=vmem)


def _resident(shape):
    nd = len(shape)
    return pl.BlockSpec(shape, lambda *_: (0,) * nd, pipeline_mode=pl.Buffered(1))


def _dot(a, b):
    return jnp.dot(a, b, preferred_element_type=F32)


def _rms(x, g):
    return x * lax.rsqrt(jnp.mean(x * x, axis=-1, keepdims=True) + EPS) * g


def _silu(v):
    return v / (1.0 + jnp.exp(-v))


def _norm_proj_kernel(x_ref, g_ref, *refs, n_out, col_chunk):
    w_refs, o_refs = refs[:n_out], refs[n_out:]
    h = _rms(x_ref[...], g_ref[...]).astype(BF16)
    for w_ref, o_ref in zip(w_refs, o_refs):
        n = w_ref.shape[1]
        for c0 in range(0, n, col_chunk):
            c1 = min(c0 + col_chunk, n)
            o_ref[:, c0:c1] = _dot(h, w_ref[:, c0:c1]).astype(o_ref.dtype)


def _norm_proj(x, g, ws, out_dtypes, tm=512):
    t, d = x.shape
    n_out = len(ws)
    return pl.pallas_call(
        functools.partial(_norm_proj_kernel, n_out=n_out, col_chunk=512),
        grid=(t // tm,),
        in_specs=[pl.BlockSpec((tm, d), lambda i: (i, 0)), _resident((1, d))]
        + [_resident(w.shape) for w in ws],
        out_specs=[pl.BlockSpec((tm, w.shape[1]), lambda i: (i, 0)) for w in ws],
        out_shape=[jax.ShapeDtypeStruct((t, w.shape[1]), dt) for w, dt in zip(ws, out_dtypes)],
        compiler_params=_params(1),
        name="norm_proj",
    )(x, g.reshape(1, d), *ws)


def _bucket_table():
    qi = np.arange(BLOCK)[:, None]
    kj = np.arange(2 * BLOCK)[None, :]
    dist = BLOCK + qi - kj
    max_exact = N_BUCKETS // 2
    d = np.maximum(dist, 0)
    df = np.maximum(d, 1).astype(np.float32)
    far = max_exact + (
        np.log(df / np.float32(max_exact)) / np.float32(math.log(MAX_DISTANCE / max_exact))
        * np.float32(N_BUCKETS - max_exact)
    ).astype(np.int32)
    bucket = np.where(d < max_exact, d, np.minimum(far, N_BUCKETS - 1))
    r = np.arange(BLOCK)[:, None]
    c = np.arange(BLOCK)[None, :]
    return np.where(c <= r, bucket[:, BLOCK:], bucket[:, :BLOCK]).astype(np.int32)


PAIRS_PER_KV = N_Q_HEADS // N_KV_HEADS // 2
SLOT_ROWS = PAIRS_PER_KV * BLOCK


def _swa_slot_head(slot):
    kvh, rest = divmod(slot, 2 * PAIRS_PER_KV)
    half, pair = divmod(rest, PAIRS_PER_KV)
    return 2 * (kvh * PAIRS_PER_KV + pair) + half


def _swa_kernel(relb_ref, bucket_ref, sink_ref, q_ref, kvp_ref, kvc_ref, o_ref, bias_scr, s_scr, p_scr):
    first = (pl.program_id(0) == 0) & (pl.program_id(1) == 0)

    def from_current(n_rows):
        r = lax.broadcasted_iota(jnp.int32, (n_rows, BLOCK), 0) & (BLOCK - 1)
        return lax.broadcasted_iota(jnp.int32, (n_rows, BLOCK), 1) <= r

    @pl.when(first)
    def _():
        bucket = bucket_ref[...]
        for slot in range(N_Q_HEADS):
            h = _swa_slot_head(slot)
            acc = jnp.zeros((BLOCK, BLOCK), F32)
            for n in range(N_BUCKETS):
                acc = jnp.where(bucket == n, relb_ref[n, h], acc)
            rows = slice(slot * BLOCK, (slot + 1) * BLOCK)
            bias_scr[1, rows, :] = acc
            bias_scr[0, rows, :] = jnp.where(from_current(BLOCK), acc, NEG_INF)

    kk = jnp.concatenate([kvp_ref[:, 0:KV_W], kvc_ref[:, 0:KV_W]], axis=0).astype(F32)
    vv = jnp.concatenate([kvp_ref[:, KV_W:], kvc_ref[:, KV_W:]], axis=0).astype(F32)
    low = lax.broadcasted_iota(jnp.int32, kk.shape, 1) < HEAD_DIM

    def placed(t):
        r = pltpu.roll(t, HEAD_DIM, 1)
        return [[jnp.where(low, t, 0.0).astype(BF16), jnp.where(low, 0.0, r).astype(BF16)],
                [jnp.where(low, r, 0.0).astype(BF16), jnp.where(low, 0.0, t).astype(BF16)]]

    k_var, v_var = placed(kk), placed(vv)

    cur_slot = from_current(SLOT_ROWS)
    for v in range(2 * N_KV_HEADS):
        kvh, half = divmod(v, 2)
        q_stack = jnp.concatenate(
            [q_ref[:, (kvh * PAIRS_PER_KV + j) * LANES:(kvh * PAIRS_PER_KV + j + 1) * LANES]
             for j in range(PAIRS_PER_KV)], axis=0)
        s = lax.dot_general(q_stack, k_var[kvh][half], NT_DIMS, preferred_element_type=F32)
        s_scr[v * SLOT_ROWS:(v + 1) * SLOT_ROWS, :] = jnp.where(cur_slot, s[:, BLOCK:], s[:, :BLOCK])

    s = s_scr[...] + bias_scr[jnp.minimum(pl.program_id(1), 1)]
    sink = sink_ref[...]
    m = jnp.maximum(jnp.max(s, axis=-1, keepdims=True), sink)
    p = jnp.exp(s - m)
    p = p / (jnp.sum(p, axis=-1, keepdims=True) + jnp.exp(sink - m))
    p_scr[...] = p.astype(BF16)

    outs = []
    for v in range(2 * N_KV_HEADS):
        kvh, half = divmod(v, 2)
        p = p_scr[v * SLOT_ROWS:(v + 1) * SLOT_ROWS, :]
        zero = jnp.zeros_like(p)
        p_band = jnp.concatenate([jnp.where(cur_slot, zero, p), jnp.where(cur_slot, p, zero)], axis=1)
        outs.append(_dot(p_band, v_var[kvh][half]))
    for kvh in range(N_KV_HEADS):
        for j in range(PAIRS_PER_KV):
            pair = kvh * PAIRS_PER_KV + j
            out = outs[2 * kvh][j * BLOCK:(j + 1) * BLOCK, :] + outs[2 * kvh + 1][j * BLOCK:(j + 1) * BLOCK, :]
            o_ref[:, pair * LANES:(pair + 1) * LANES] = out.astype(o_ref.dtype)


def _swa(q, kv, rel_bias, sinks, batch):
    t = q.shape[0]
    nb = t // batch // BLOCK
    rows = N_Q_HEADS * BLOCK
    bucket = jnp.asarray(_bucket_table())
    slot_sinks = sinks.astype(F32)[np.array([_swa_slot_head(s) for s in range(N_Q_HEADS)])]
    sink_rows = jnp.broadcast_to(jnp.repeat(slot_sinks, BLOCK)[:, None], (rows, LANES))
    smem = pl.BlockSpec(memory_space=pltpu.SMEM)
    return pl.pallas_call(
        _swa_kernel,
        grid=(batch, nb),
        in_specs=[smem, _resident((BLOCK, BLOCK)), _resident((rows, LANES)),
                  pl.BlockSpec((BLOCK, Q_W), lambda b, i: (b * nb + i, 0)),
                  pl.BlockSpec((BLOCK, 2 * KV_W), lambda b, i: (b * nb + jnp.maximum(i - 1, 0), 0)),
                  pl.BlockSpec((BLOCK, 2 * KV_W), lambda b, i: (b * nb + i, 0))],
        out_specs=pl.BlockSpec((BLOCK, Q_W), lambda b, i: (b * nb + i, 0)),
        out_shape=jax.ShapeDtypeStruct((t, Q_W), BF16),
        scratch_shapes=[pltpu.VMEM((2, rows, BLOCK), F32), pltpu.VMEM((rows, BLOCK), F32),
                        pltpu.VMEM((rows, BLOCK), BF16)],
        compiler_params=_params(2),
        name="swa",
    )(rel_bias, bucket, sink_rows, q, kv, kv)


TAIL_TILE = 512
FF_CHUNK = 256


def _cross_attention(xq_ref, mk_ref, mv_ref, cross_scr):
    for h in range(N_X_HEADS):
        sl = slice(h * X_HEAD_DIM, (h + 1) * X_HEAD_DIM)
        s = lax.dot_general(xq_ref[:, sl], mk_ref[:, sl], NT_DIMS, preferred_element_type=F32)
        s = s * (X_HEAD_DIM ** -0.5)
        p = jnp.exp(s - jnp.max(s, axis=-1, keepdims=True))
        p = p / jnp.sum(p, axis=-1, keepdims=True)
        cross_scr[:, sl] = _dot(p.astype(BF16), mv_ref[:, sl]).astype(cross_scr.dtype)


def _swiglu_residual(x1, g_ref, wg_ref, wu_ref, wd_ref, act_scr):
    h = _rms(x1, g_ref[...]).astype(BF16)
    for c0 in range(0, D_FF, FF_CHUNK):
        sl = slice(c0, c0 + FF_CHUNK)
        act_scr[:, sl] = (_silu(_dot(h, wg_ref[:, sl])) * _dot(h, wu_ref[:, sl])).astype(BF16)
    return x1 + _dot(act_scr[...], wd_ref[...])


def _tail_specs(batch, t, a_width, tile=TAIL_TILE):
    nt = t // batch // tile
    tok = lambda w: pl.BlockSpec((tile, w), lambda b, i: (b * nt + i, 0))
    mem = lambda half: pl.BlockSpec((MEM_LEN, XQ_W), lambda b, i: (b, half))
    return nt, tok, [tok(D_MODEL), tok(a_width), tok(XQ_W), mem(0), mem(1),
                     _resident((a_width, D_MODEL)), _resident((XQ_W, D_MODEL))]


def _attn_tail_kernel(x_ref, a_ref, xq_ref, mk_ref, mv_ref, wa_ref, wc_ref, g_ref, wg_ref, wu_ref, wd_ref,
                      o_ref, cross_scr, act_scr):
    _cross_attention(xq_ref, mk_ref, mv_ref, cross_scr)
    x1 = x_ref[...] + _dot(a_ref[...], wa_ref[...]) + _dot(cross_scr[...], wc_ref[...])
    o_ref[...] = _swiglu_residual(x1, g_ref, wg_ref, wu_ref, wd_ref, act_scr)


def _attn_tail(x, attn, xq, mem_kv, wa, wc, g, wg, wu, wd, batch):
    t, d = x.shape
    nt, tok, specs = _tail_specs(batch, t, attn.shape[1])
    return pl.pallas_call(
        _attn_tail_kernel,
        grid=(batch, nt),
        in_specs=specs + [_resident((1, d)), _resident(wg.shape), _resident(wu.shape), _resident(wd.shape)],
        out_specs=tok(d),
        out_shape=jax.ShapeDtypeStruct((t, d), F32),
        scratch_shapes=[pltpu.VMEM((TAIL_TILE, XQ_W), BF16), pltpu.VMEM((TAIL_TILE, D_FF), BF16)],
        compiler_params=_params(2),
        name="attn_tail",
    )(x, attn, xq, mem_kv, mem_kv, wa, wc, g.reshape(1, d), wg, wu, wd)


TOP_K = 2
HI16 = 0xFFFF0000
META_EXPERT, META_POS, META_GATE = 0, 2, 4


def _pack_bf16_pairs(h):
    n = h.shape[1] // 2
    u = pltpu.bitcast(h.astype(BF16).astype(F32), jnp.uint32)
    return (u[:, :n] >> 16) | (u[:, n:] & jnp.uint32(HI16))


def _unpack_bf16_pairs(w):
    lo = pltpu.bitcast(w << 16, F32).astype(BF16)
    hi = pltpu.bitcast(w & jnp.uint32(HI16), F32).astype(BF16)
    return jnp.concatenate([lo, hi], axis=1)


def _route(x1, g_ref, wr_ref, meta_ref, meta_t_ref, cnt_ref, hpk_ref, run_scr):
    h = _rms(x1, g_ref[...])
    hpk_ref[...] = _pack_bf16_pairs(h)
    h1, h2 = _split_bf16(h, 2)
    prod = _dot(h1, wr_ref[...]) + _dot(h2, wr_ref[...])
    logits = prod[:, :LANES] + prod[:, LANES:]
    tm = h.shape[0]
    lt = logits.T[:N_EXPERTS, :]
    sub = lax.broadcasted_iota(jnp.int32, lt.shape, 0)
    m1 = jnp.max(lt, axis=0, keepdims=True)
    i1 = jnp.min(jnp.where(lt == m1, sub, N_EXPERTS), axis=0, keepdims=True)
    lt2 = jnp.where(sub == i1, NEG_INF, lt)
    m2 = jnp.max(lt2, axis=0, keepdims=True)
    i2 = jnp.min(jnp.where(lt2 == m2, sub, N_EXPERTS), axis=0, keepdims=True)
    e = jnp.exp(m2 - m1)
    g1, g2 = 1.0 / (1.0 + e), e / (1.0 + e)

    onehot = jnp.where(jnp.logical_or(sub == i1, sub == i2), 1.0, 0.0)
    onehot16 = jnp.concatenate([onehot, jnp.zeros_like(onehot)], axis=0).astype(BF16)
    earlier = lax.broadcasted_iota(jnp.int32, (tm, tm), 0) < lax.broadcasted_iota(jnp.int32, (tm, tm), 1)
    before = _dot(onehot16, earlier.astype(BF16))[:N_EXPERTS, :] + run_scr[:, 0:1]
    pos1 = jnp.sum(jnp.where(sub == i1, before, 0.0), axis=0, keepdims=True)
    pos2 = jnp.sum(jnp.where(sub == i2, before, 0.0), axis=0, keepdims=True)
    run_scr[...] += jnp.sum(onehot, axis=1, keepdims=True)
    cnt_ref[...] = run_scr[...]

    def put(k, v):
        return jnp.where(sub == k, v, 0.0)

    meta_t = (put(META_EXPERT, i1.astype(F32)) + put(META_EXPERT + 1, i2.astype(F32))
              + put(META_POS, pos1) + put(META_POS + 1, pos2)
              + put(META_GATE, g1) + put(META_GATE + 1, g2))
    meta_t_ref[...] = meta_t
    meta_ref[...] = jnp.concatenate([meta_t, jnp.zeros((LANES - N_EXPERTS, tm), F32)], axis=0).T


def _ssd_tail_kernel(x_ref, y_ref, xq_ref, mk_ref, mv_ref, wy_ref, wc_ref, g_ref, wr_ref,
                     x1_ref, meta_ref, meta_t_ref, cnt_ref, hpk_ref, run_scr, *cross_scrs):
    @pl.when((pl.program_id(0) == 0) & (pl.program_id(1) == 0))
    def _():
        run_scr[...] = jnp.zeros_like(run_scr)

    for s in range(x_ref.shape[0] // TAIL_TILE):
        rows = pl.ds(s * TAIL_TILE, TAIL_TILE)
        _cross_attention(xq_ref.at[rows], mk_ref, mv_ref, cross_scrs[s])
        x1 = x_ref[rows, :] + _dot(y_ref[rows, :], wy_ref[...]) + _dot(cross_scrs[s][...], wc_ref[...])
        x1_ref[rows, :] = x1
        _route(x1, g_ref, wr_ref, meta_ref.at[rows], meta_t_ref.at[:, rows], cnt_ref, hpk_ref.at[rows], run_scr)


SSD_TAIL_SUBTILES = 2


def _ssd_tail(x, y, xq, mem_kv, wy, wc, g, w_router, batch):
    t, d = x.shape
    tile = SSD_TAIL_SUBTILES * TAIL_TILE
    nt, tok, specs = _tail_specs(batch, t, y.shape[1], tile)
    w1, w2 = _split_bf16(w_router.astype(F32), 2)
    wr = jnp.zeros((d, 2 * LANES), BF16).at[:, :N_EXPERTS].set(w1).at[:, LANES:LANES + N_EXPERTS].set(w2)
    return pl.pallas_call(
        _ssd_tail_kernel,
        grid=(batch, nt),
        in_specs=specs + [_resident((1, d)), _resident((d, 2 * LANES))],
        out_specs=[tok(d), tok(LANES), pl.BlockSpec((N_EXPERTS, tile), lambda b, i: (0, b * nt + i)),
                   pl.BlockSpec((N_EXPERTS, LANES), lambda b, i: (0, 0)), tok(d // 2)],
        out_shape=[jax.ShapeDtypeStruct((t, d), F32), jax.ShapeDtypeStruct((t, LANES), F32),
                   jax.ShapeDtypeStruct((N_EXPERTS, t), F32), jax.ShapeDtypeStruct((N_EXPERTS, LANES), F32),
                   jax.ShapeDtypeStruct((t, d // 2), jnp.uint32)],
        scratch_shapes=[pltpu.VMEM((N_EXPERTS, LANES), F32)]
        + [pltpu.VMEM((TAIL_TILE, XQ_W), BF16)] * SSD_TAIL_SUBTILES,
        compiler_params=_params(2),
        name="ssd_tail",
    )(x, y, xq, mem_kv, mem_kv, wy, wc, g.reshape(1, d), wr)


ROW_TILE = 512
DISPATCH_TILE = 2048
COMBINE_TILE = 1024


def _row_copy_all_wait(src_like, dst_like, sem):
    pltpu.make_async_copy(src_like, dst_like, sem).wait()


def _dispatch_kernel(pad_start_ref, dest_hbm, hpk_ref, xs_out, idx_smem, zero_buf, idx_sem, row_sem, fill_sem):
    tm = hpk_ref.shape[0]

    @pl.when(pl.program_id(0) == 0)
    def _():
        zero_buf[...] = jnp.zeros_like(zero_buf)
        fill_rows = zero_buf.shape[0]
        starts = [pad_start_ref[e] // SUBLANES * SUBLANES for e in range(N_EXPERTS)]
        last = xs_out.shape[0] - fill_rows
        n_tail = -(-(N_EXPERTS + 1) * ROW_TILE // fill_rows)
        starts += [jnp.minimum(pad_start_ref[N_EXPERTS] + k * fill_rows, last) for k in range(n_tail)]
        for s in starts:
            fill = pltpu.make_async_copy(zero_buf, xs_out.at[pl.ds(pl.multiple_of(s, SUBLANES), fill_rows)], fill_sem)
            fill.start()
            fill.wait()

    idx_copy = pltpu.make_async_copy(dest_hbm.at[pl.program_id(0)], idx_smem, idx_sem)
    idx_copy.start()
    idx_copy.wait()

    def issue(t, carry):
        for k in range(TOP_K):
            row = idx_smem[k * tm + t]
            pltpu.make_async_copy(hpk_ref.at[pl.ds(t, 1)], xs_out.at[pl.ds(row, 1)], row_sem).start(priority=k)
        return carry

    lax.fori_loop(0, tm, issue, 0, unroll=8)
    for _ in range(TOP_K):
        _row_copy_all_wait(hpk_ref, xs_out.at[pl.ds(0, tm)], row_sem)


def _dispatch(pad_start, dest_tiles, hpk, n_rows):
    t, w = hpk.shape
    tm = dest_tiles.shape[1] // TOP_K
    any_spec = pl.BlockSpec(memory_space=pl.ANY)
    return pl.pallas_call(
        _dispatch_kernel,
        grid_spec=pltpu.PrefetchScalarGridSpec(
            num_scalar_prefetch=1,
            grid=(t // tm,),
            in_specs=[any_spec, pl.BlockSpec((tm, w), lambda i, ps: (i, 0))],
            out_specs=any_spec,
            scratch_shapes=[pltpu.SMEM((TOP_K * tm,), jnp.int32), pltpu.VMEM((ROW_TILE + SUBLANES, w), jnp.uint32),
                            pltpu.SemaphoreType.DMA, pltpu.SemaphoreType.DMA, pltpu.SemaphoreType.DMA]),
        out_shape=jax.ShapeDtypeStruct((n_rows, w), jnp.uint32),
        compiler_params=_params(1),
        name="dispatch",
    )(pad_start, dest_tiles, hpk)


def _experts_kernel(tile_expert_ref, n_used_ref, xs_ref, wg_ref, wu_ref, wd_ref, y_ref, act_scr):
    del tile_expert_ref

    @pl.when(pl.program_id(0) < n_used_ref[0])
    def _():
        h = _unpack_bf16_pairs(xs_ref[...])
        for c0 in range(0, D_FF, FF_CHUNK):
            sl = slice(c0, c0 + FF_CHUNK)
            act_scr[:, sl] = (_silu(_dot(h, wg_ref[:, sl])) * _dot(h, wu_ref[:, sl])).astype(BF16)
        y_ref[...] = _dot(act_scr[...], wd_ref[...])

    @pl.when(pl.program_id(0) >= n_used_ref[0])
    def _():
        y_ref[...] = jnp.zeros_like(y_ref)


def _experts(tile_expert, n_used, xs, wg, wu, wd, layer):
    n_rows, w = xs.shape
    d = 2 * w
    tm = ROW_TILE
    wspec = lambda shape: pl.BlockSpec((None, None) + shape, lambda i, te, nu: (layer, te[i], 0, 0))
    return pl.pallas_call(
        _experts_kernel,
        grid_spec=pltpu.PrefetchScalarGridSpec(
            num_scalar_prefetch=2,
            grid=(n_rows // tm,),
            in_specs=[pl.BlockSpec((tm, w), lambda i, te, nu: (jnp.minimum(i, nu[0] - 1), 0)),
                      wspec((d, D_FF)), wspec((d, D_FF)), wspec((D_FF, d))],
            out_specs=pl.BlockSpec((tm, d), lambda i, te, nu: (i, 0)),
            scratch_shapes=[pltpu.VMEM((tm, D_FF), BF16)]),
        out_shape=jax.ShapeDtypeStruct((n_rows, d), F32),
        compiler_params=_params(1),
        name="experts",
    )(tile_expert, n_used, xs, wg, wu, wd)


def _combine_kernel(dest_hbm, x_ref, meta_ref, y_hbm, gf_ref, o_ref, idx_smem, y_buf, idx_sem, row_sem, *,
                    final_norm):
    tm = x_ref.shape[0]
    i, n = pl.program_id(0), pl.num_programs(0)
    slot = i % 2

    def idx_copy(tile, s):
        return pltpu.make_async_copy(dest_hbm.at[tile], idx_smem.at[s], idx_sem.at[s])

    def issue_rows(s):
        def issue(t, carry):
            for k in range(TOP_K):
                row = idx_smem[s, k * tm + t]
                pltpu.make_async_copy(y_hbm.at[pl.ds(row, 1)], y_buf.at[s, k, pl.ds(t, 1)],
                                      row_sem.at[s]).start(priority=k)
            return carry

        lax.fori_loop(0, tm, issue, 0, unroll=8)

    @pl.when(i == 0)
    def _():
        first = idx_copy(0, 0)
        first.start()
        first.wait()
        issue_rows(0)

        @pl.when(n > 1)
        def _():
            idx_copy(1, 1).start()

    @pl.when(i + 1 < n)
    def _():
        idx_copy(i + 1, 1 - slot).wait()
        issue_rows(1 - slot)

    @pl.when(i + 2 < n)
    def _():
        idx_copy(i + 2, slot).start()

    for k in range(TOP_K):
        _row_copy_all_wait(y_hbm.at[pl.ds(0, tm)], y_buf.at[slot, k], row_sem.at[slot])
    meta = meta_ref[...]
    out = x_ref[...]
    for k in range(TOP_K):
        out = out + meta[:, META_GATE + k:META_GATE + k + 1] * y_buf[slot, k]
    o_ref[...] = _rms(out, gf_ref[...]) if final_norm else out


def _combine(dest_tiles, x, meta, y, g_final):
    t, d = x.shape
    tm = dest_tiles.shape[1] // TOP_K
    any_spec = pl.BlockSpec(memory_space=pl.ANY)
    final_norm = g_final is not None
    gf = (g_final if final_norm else jnp.ones((d,), F32)).reshape(1, d)
    return pl.pallas_call(
        functools.partial(_combine_kernel, final_norm=final_norm),
        grid=(t // tm,),
        in_specs=[any_spec, pl.BlockSpec((tm, d), lambda i: (i, 0)),
                  pl.BlockSpec((tm, LANES), lambda i: (i, 0)), any_spec, _resident((1, d))],
        out_specs=pl.BlockSpec((tm, d), lambda i: (i, 0)),
        out_shape=jax.ShapeDtypeStruct((t, d), F32),
        scratch_shapes=[pltpu.SMEM((2, TOP_K * tm), jnp.int32), pltpu.VMEM((2, TOP_K, tm, d), F32),
                        pltpu.SemaphoreType.DMA((2,)), pltpu.SemaphoreType.DMA((2,))],
        compiler_params=_params(1),
        name="combine",
    )(dest_tiles, x, meta, y, gf)


def _moe(x, meta, meta_t, counts, hpk, wg, wu, wd, layer, g_final=None):
    t, d = x.shape
    cnt = counts[:, 0].astype(jnp.int32)
    padded = (cnt + ROW_TILE - 1) // ROW_TILE * ROW_TILE
    ends = jnp.cumsum(padded)
    starts = ends - padded
    expert = meta_t[META_EXPERT:META_EXPERT + TOP_K].astype(jnp.int32)
    dest = meta_t[META_POS:META_POS + TOP_K].astype(jnp.int32)
    for e in range(N_EXPERTS):
        dest = dest + jnp.where(expert == e, starts[e], 0)

    def dest_tiles(tile):
        return dest.reshape(TOP_K, t // tile, tile).transpose(1, 0, 2).reshape(-1, TOP_K * tile)

    n_rows = TOP_K * t + (N_EXPERTS + 1) * ROW_TILE
    n_tiles = n_rows // ROW_TILE
    n_used = ends[-1] // ROW_TILE
    tile_start = jnp.minimum(jnp.arange(n_tiles), n_used - 1) * ROW_TILE
    tile_expert = jnp.sum(tile_start[:, None] >= ends[None, :], axis=1).astype(jnp.int32)
    pad_start = jnp.concatenate([starts + cnt, ends[-1:]])
    xs = _dispatch(pad_start, dest_tiles(DISPATCH_TILE), hpk, n_rows)
    y = _experts(tile_expert, n_used.reshape(1).astype(jnp.int32), xs, wg, wu, wd, layer)
    return _combine(dest_tiles(COMBINE_TILE), x, meta, y, g_final)


CONV_TAIL = 16


def _conv_shift_matrix():
    m = np.zeros(((CONV_WIDTH - 1) * CHUNK, CONV_TAIL + CHUNK), np.float32)
    for k in range(CONV_WIDTH - 1):
        for t in range(CHUNK):
            m[k * CHUNK + t, CONV_TAIL + t - (CONV_WIDTH - 1) + k] = 1.0
    return m


def _split_bf16(v, parts):
    out, r = [], v
    for _ in range(parts):
        p = r.astype(BF16)
        out.append(p)
        r = r - p.astype(F32)
    return out


def _ssd_kernel(xbc_ref, z_ref, dt_ref, cw_ref, cb_ref, dtb_ref, alog_ref, dsk_ref, gn_ref, exp_ref, shift_ref,
                o_ref, state_scr, *chunk_scr):
    n_sub = xbc_ref.shape[0] // CHUNK
    ext_scrs, y_scrs = chunk_scr[:n_sub], chunk_scr[n_sub:]

    @pl.when(pl.program_id(1) == 0)
    def _():
        state_scr[...] = jnp.zeros_like(state_scr)
        ext_scrs[n_sub - 1][CHUNK:, :] = jnp.zeros((CONV_TAIL, CONV_CH), BF16)

    for c in range(n_sub):
        rows = pl.ds(c * CHUNK, CHUNK)
        _ssd_chunk(xbc_ref.at[rows], z_ref.at[rows], dt_ref.at[rows], cw_ref, cb_ref, dtb_ref, alog_ref, dsk_ref,
                   gn_ref, exp_ref, shift_ref, o_ref.at[rows], state_scr, ext_scrs[c],
                   ext_scrs[(c - 1) % n_sub], y_scrs[c])


def _ssd_chunk(xbc_ref, z_ref, dt_ref, cw_ref, cb_ref, dtb_ref, alog_ref, dsk_ref, gn_ref, exp_ref, shift_ref,
               o_ref, state_scr, ext_scr, prev_ext_scr, y_scr):
    ext_scr[0:CONV_TAIL, :] = prev_ext_scr[CHUNK:, :]
    ext_scr[CONV_TAIL:, :] = xbc_ref[...]

    def conv_silu(c0, c1):
        shifted = _dot(shift_ref[...], ext_scr[:, c0:c1])
        acc = cb_ref[:, c0:c1] + cw_ref[CONV_WIDTH - 1:CONV_WIDTH, c0:c1] * ext_scr[CONV_TAIL:, c0:c1].astype(F32)
        for k in range(CONV_WIDTH - 1):
            acc = acc + cw_ref[k:k + 1, c0:c1] * shifted[k * CHUNK:(k + 1) * CHUNK, :]
        return _silu(acc)

    dt = dt_ref[...] + dtb_ref[...]
    dt = jnp.maximum(dt, 0.0) + jnp.log(1.0 + jnp.exp(-jnp.abs(dt)))
    da = dt * -jnp.exp(alog_ref[...])
    row = lax.broadcasted_iota(jnp.int32, (CHUNK, CHUNK), 0)
    col = lax.broadcasted_iota(jnp.int32, (CHUNK, CHUNK), 1)
    causal = col <= row
    cs = _dot(causal.astype(BF16), jnp.concatenate(_split_bf16(da, 3), axis=1))
    acum = cs[:, 0:LANES] + cs[:, LANES:2 * LANES] + cs[:, 2 * LANES:]
    acum_t = acum.T
    ea = jnp.exp(acum)
    dtte = dt * jnp.exp(acum[CHUNK - 1:CHUNK, :] - acum)

    def expand(v, g):
        return _dot(jnp.concatenate(_split_bf16(v, 2), axis=1), exp_ref[:, g * GROUP_W:(g + 1) * GROUP_W])

    low = lax.broadcasted_iota(jnp.int32, (CHUNK, LANES), 1) < SSM_HEAD_DIM
    for g in range(N_SSM_GROUPS):
        gsl = slice(g * GROUP_W, (g + 1) * GROUP_W)
        xg = conv_silu(g * GROUP_W, (g + 1) * GROUP_W)
        bg = conv_silu(D_INNER + g * D_STATE, D_INNER + (g + 1) * D_STATE)
        cg = conv_silu(D_INNER + BC_W + g * D_STATE, D_INNER + BC_W + (g + 1) * D_STATE).astype(BF16)
        ea_x = expand(ea, g)
        xdt = (xg * expand(dt, g)).astype(BF16)
        cb = lax.dot_general(cg, bg.astype(BF16), NT_DIMS, preferred_element_type=F32)
        state = state_scr[g]
        y = _dot(cg, state.astype(BF16)) * ea_x + xg * dsk_ref[:, gsl]
        diag = []
        for pair in range(HEADS_PER_GROUP // 2):
            xp = xdt[:, pair * LANES:(pair + 1) * LANES]
            acc = jnp.zeros((CHUNK, LANES), F32)
            for half in range(2):
                h = g * HEADS_PER_GROUP + 2 * pair + half
                seg = acum[:, h:h + 1] - acum_t[h:h + 1, :]
                m = (cb * jnp.exp(jnp.where(causal, seg, NEG_INF))).astype(BF16)
                xm = jnp.where(low if half == 0 else jnp.logical_not(low), xp, jnp.zeros_like(xp))
                acc = acc + _dot(m, xm)
            diag.append(acc)
        y_scr[:, gsl] = y + jnp.concatenate(diag, axis=1)
        w = (xg * expand(dtte, g)).astype(BF16)
        state_scr[g] = state * ea_x[CHUNK - 1:CHUNK, :] + _dot(bg.T.astype(BF16), w)

    z = z_ref[...].astype(F32)
    o_ref[...] = _rms(y_scr[...] * _silu(z), gn_ref[...]).astype(o_ref.dtype)


SSD_SUBCHUNKS = 2


def _ssd(xbc, z, dt_raw, conv_w, conv_b, dt_bias, a_log, d_skip, g_norm, batch):
    t = xbc.shape[0]
    step = SSD_SUBCHUNKS * CHUNK
    nc = t // batch // step

    def lane_pad(v):
        return jnp.zeros((1, LANES), F32).at[0, :N_SSM_HEADS].set(v)

    expand = np.zeros((LANES, D_INNER), np.float32)
    for h in range(N_SSM_HEADS):
        expand[h, h * SSM_HEAD_DIM:(h + 1) * SSM_HEAD_DIM] = 1.0
    expand2 = jnp.asarray(np.concatenate([expand, expand], axis=0), BF16)
    tok = lambda b, c: (b * nc + c, 0)
    return pl.pallas_call(
        _ssd_kernel,
        grid=(batch, nc),
        in_specs=[pl.BlockSpec((step, CONV_CH), tok), pl.BlockSpec((step, D_INNER), tok),
                  pl.BlockSpec((step, LANES), tok),
                  _resident((CONV_WIDTH, CONV_CH)), _resident((1, CONV_CH)),
                  _resident((1, LANES)), _resident((1, LANES)),
                  _resident((1, D_INNER)), _resident((1, D_INNER)), _resident((2 * LANES, D_INNER)),
                  _resident(((CONV_WIDTH - 1) * CHUNK, CONV_TAIL + CHUNK))],
        out_specs=pl.BlockSpec((step, D_INNER), tok),
        out_shape=jax.ShapeDtypeStruct((t, D_INNER), BF16),
        scratch_shapes=[pltpu.VMEM((N_SSM_GROUPS, D_STATE, GROUP_W), F32)]
        + [pltpu.VMEM((CONV_TAIL + CHUNK, CONV_CH), BF16)] * SSD_SUBCHUNKS
        + [pltpu.VMEM((CHUNK, D_INNER), F32)] * SSD_SUBCHUNKS,
        compiler_params=_params(2),
        name="ssd",
    )(xbc, z, dt_raw, conv_w, conv_b.reshape(1, CONV_CH), lane_pad(dt_bias), lane_pad(a_log),
      jnp.repeat(d_skip, SSM_HEAD_DIM).reshape(1, D_INNER), g_norm.reshape(1, D_INNER), expand2,
      jnp.asarray(_conv_shift_matrix(), BF16))


def kernel(x, mem, g_mix, g_ffn, g_mem, w_mem_kv, rel_bias, swa_w_in, swa_sinks, swa_w_out, ssm_w_in, ssm_conv_w, ssm_conv_b, ssm_dt_bias, ssm_A_log, ssm_D, ssm_g_norm, ssm_w_out, ffn_w_gate, ffn_w_up, ffn_w_down, moe_w_router, moe_w_gate, moe_w_up, moe_w_down, g_final):
    batch, seq, d = x.shape
    xf = x.reshape(batch * seq, d)
    memf = mem.reshape(batch * MEM_LEN, d)
    moe_wg, moe_wu, moe_wd = moe_w_gate.astype(BF16), moe_w_up.astype(BF16), moe_w_down.astype(BF16)
    for i in range(DEPTH):
        j = i // 2
        (mem_kv,) = _norm_proj(memf, g_mem[i], [w_mem_kv[i].astype(BF16)], [BF16])
        if i % 2 == 0:
            w_in = swa_w_in[j]
            w_q = (w_in[:, :Q_W] * HEAD_DIM ** -0.5).astype(BF16)
            q, kv, xq = _norm_proj(
                xf, g_mix[i], [w_q, w_in[:, Q_W:Q_W + 2 * KV_W].astype(BF16), w_in[:, Q_W + 2 * KV_W:].astype(BF16)],
                [BF16, BF16, BF16])
            attn = _swa(q, kv, rel_bias, swa_sinks[j], batch)
            w_out = swa_w_out[j].astype(BF16)
            xf = _attn_tail(xf, attn, xq, mem_kv, w_out[:Q_W], w_out[Q_W:], g_ffn[i],
                            ffn_w_gate[j].astype(BF16), ffn_w_up[j].astype(BF16), ffn_w_down[j].astype(BF16),
                            batch)
        else:
            w_in = ssm_w_in[j].astype(BF16)
            o_dt = D_INNER + CONV_CH
            w_dt = jnp.zeros((d, LANES), BF16).at[:, :N_SSM_HEADS].set(w_in[:, o_dt:o_dt + N_SSM_HEADS])
            z, xbc, dt_raw, xq = _norm_proj(
                xf, g_mix[i], [w_in[:, :D_INNER], w_in[:, D_INNER:o_dt], w_dt, w_in[:, o_dt + N_SSM_HEADS:]],
                [BF16, BF16, F32, BF16])
            y = _ssd(xbc, z, dt_raw, ssm_conv_w[j], ssm_conv_b[j], ssm_dt_bias[j], ssm_A_log[j],
                     ssm_D[j], ssm_g_norm[j], batch)
            w_out = ssm_w_out[j].astype(BF16)
            xf, meta, meta_t, counts, hpk = _ssd_tail(xf, y, xq, mem_kv, w_out[:D_INNER], w_out[D_INNER:],
                                                      g_ffn[i], moe_w_router[j], batch)
            xf = _moe(xf, meta, meta_t, counts, hpk, moe_wg, moe_wu, moe_wd, j,
                      g_final if i == DEPTH - 1 else None)
    assert DEPTH % 2 == 0
    return xf.reshape(batch, seq, d)
```

```python
import functools
import math

import numpy as np
import jax
import jax.numpy as jnp
from jax import lax
from jax.experimental import pallas as pl
from jax.experimental.pallas import tpu as pltpu

F32 = jnp.float32
BF16 = jnp.bfloat16

D_MODEL = 1024
DEPTH = 4
MEM_LEN = 256
EPS = 1e-6
N_Q_HEADS = 16
N_KV_HEADS = 2
HEAD_DIM = 64
BLOCK = 128
N_BUCKETS = 32
MAX_DISTANCE = 128
N_X_HEADS = 4
X_HEAD_DIM = 256
D_INNER = 2048
SSM_HEAD_DIM = 64
N_SSM_HEADS = 32
N_SSM_GROUPS = 4
HEADS_PER_GROUP = 8
D_STATE = 128
CONV_WIDTH = 4
CHUNK = 128
D_FF = 2816
N_EXPERTS = 8
Q_W = N_Q_HEADS * HEAD_DIM
KV_W = N_KV_HEADS * HEAD_DIM
XQ_W = N_X_HEADS * X_HEAD_DIM
BC_W = N_SSM_GROUPS * D_STATE
CONV_CH = D_INNER + 2 * BC_W
GROUP_W = HEADS_PER_GROUP * SSM_HEAD_DIM

LANES = 128
SUBLANES = 8
VMEM_LIMIT = 56 << 20
NEG_INF = float("-inf")
NT_DIMS = (((1,), (1,)), ((), ()))


def _params(n_axes, vmem=VMEM_LIMIT):
    return pltpu.CompilerParams(dimension_semantics=("arbitrary",) * n_axes, vmem_limit_bytes=vmem)


def _resident(shape):
    nd = len(shape)
    return pl.BlockSpec(shape, lambda *_: (0,) * nd, pipeline_mode=pl.Buffered(1))


def _dot(a, b):
    return jnp.dot(a, b, preferred_element_type=F32)


def _rms(x, g):
    return x * lax.rsqrt(jnp.mean(x * x, axis=-1, keepdims=True) + EPS) * g


def _silu(v):
    return v / (1.0 + jnp.exp(-v))


def _norm_proj_kernel(x_ref, g_ref, *refs, n_out, col_chunk):
    w_refs, o_refs = refs[:n_out], refs[n_out:]
    h = _rms(x_ref[...], g_ref[...]).astype(BF16)
    for w_ref, o_ref in zip(w_refs, o_refs):
        n = w_ref.shape[1]
        for c0 in range(0, n, col_chunk):
            c1 = min(c0 + col_chunk, n)
            o_ref[:, c0:c1] = _dot(h, w_ref[:, c0:c1]).astype(o_ref.dtype)


def _norm_proj(x, g, ws, out_dtypes, tm=512):
    t, d = x.shape
    n_out = len(ws)
    return pl.pallas_call(
        functools.partial(_norm_proj_kernel, n_out=n_out, col_chunk=512),
        grid=(t // tm,),
        in_specs=[pl.BlockSpec((tm, d), lambda i: (i, 0)), _resident((1, d))]
        + [_resident(w.shape) for w in ws],
        out_specs=[pl.BlockSpec((tm, w.shape[1]), lambda i: (i, 0)) for w in ws],
        out_shape=[jax.ShapeDtypeStruct((t, w.shape[1]), dt) for w, dt in zip(ws, out_dtypes)],
        compiler_params=_params(1),
        name="norm_proj",
    )(x, g.reshape(1, d), *ws)


def _bucket_table():
    qi = np.arange(BLOCK)[:, None]
    kj = np.arange(2 * BLOCK)[None, :]
    dist = BLOCK + qi - kj
    max_exact = N_BUCKETS // 2
    d = np.maximum(dist, 0)
    df = np.maximum(d, 1).astype(np.float32)
    far = max_exact + (
        np.log(df / np.float32(max_exact)) / np.float32(math.log(MAX_DISTANCE / max_exact))
        * np.float32(N_BUCKETS - max_exact)
    ).astype(np.int32)
    bucket = np.where(d < max_exact, d, np.minimum(far, N_BUCKETS - 1))
    r = np.arange(BLOCK)[:, None]
    c = np.arange(BLOCK)[None, :]
    return np.where(c <= r, bucket[:, BLOCK:], bucket[:, :BLOCK]).astype(np.int32)


PAIRS_PER_KV = N_Q_HEADS // N_KV_HEADS // 2
SLOT_ROWS = PAIRS_PER_KV * BLOCK


def _swa_slot_head(slot):
    kvh, rest = divmod(slot, 2 * PAIRS_PER_KV)
    half, pair = divmod(rest, PAIRS_PER_KV)
    return 2 * (kvh * PAIRS_PER_KV + pair) + half


def _swa_kernel(relb_ref, bucket_ref, sink_ref, q_ref, kvp_ref, kvc_ref, o_ref, bias_scr, s_scr, p_scr):
    first = (pl.program_id(0) == 0) & (pl.program_id(1) == 0)

    def from_current(n_rows):
        r = lax.broadcasted_iota(jnp.int32, (n_rows, BLOCK), 0) & (BLOCK - 1)
        return lax.broadcasted_iota(jnp.int32, (n_rows, BLOCK), 1) <= r

    @pl.when(first)
    def _():
        bucket = bucket_ref[...]
        for slot in range(N_Q_HEADS):
            h = _swa_slot_head(slot)
            acc = jnp.zeros((BLOCK, BLOCK), F32)
            for n in range(N_BUCKETS):
                acc = jnp.where(bucket == n, relb_ref[n, h], acc)
            rows = slice(slot * BLOCK, (slot + 1) * BLOCK)
            bias_scr[1, rows, :] = acc
            bias_scr[0, rows, :] = jnp.where(from_current(BLOCK), acc, NEG_INF)

    kk = jnp.concatenate([kvp_ref[:, 0:KV_W], kvc_ref[:, 0:KV_W]], axis=0).astype(F32)
    vv = jnp.concatenate([kvp_ref[:, KV_W:], kvc_ref[:, KV_W:]], axis=0).astype(F32)
    low = lax.broadcasted_iota(jnp.int32, kk.shape, 1) < HEAD_DIM

    def placed(t):
        r = pltpu.roll(t, HEAD_DIM, 1)
        return [[jnp.where(low, t, 0.0).astype(BF16), jnp.where(low, 0.0, r).astype(BF16)],
                [jnp.where(low, r, 0.0).astype(BF16), jnp.where(low, 0.0, t).astype(BF16)]]

    k_var, v_var = placed(kk), placed(vv)

    cur_slot = from_current(SLOT_ROWS)
    for v in range(2 * N_KV_HEADS):
        kvh, half = divmod(v, 2)
        q_stack = jnp.concatenate(
            [q_ref[:, (kvh * PAIRS_PER_KV + j) * LANES:(kvh * PAIRS_PER_KV + j + 1) * LANES]
             for j in range(PAIRS_PER_KV)], axis=0)
        s = lax.dot_general(q_stack, k_var[kvh][half], NT_DIMS, preferred_element_type=F32)
        s_scr[v * SLOT_ROWS:(v + 1) * SLOT_ROWS, :] = jnp.where(cur_slot, s[:, BLOCK:], s[:, :BLOCK])

    s = s_scr[...] + bias_scr[jnp.minimum(pl.program_id(1), 1)]
    sink = sink_ref[...]
    m = jnp.maximum(jnp.max(s, axis=-1, keepdims=True), sink)
    p = jnp.exp(s - m)
    p = p / (jnp.sum(p, axis=-1, keepdims=True) + jnp.exp(sink - m))
    p_scr[...] = p.astype(BF16)

    outs = []
    for v in range(2 * N_KV_HEADS):
        kvh, half = divmod(v, 2)
        p = p_scr[v * SLOT_ROWS:(v + 1) * SLOT_ROWS, :]
        zero = jnp.zeros_like(p)
        p_band = jnp.concatenate([jnp.where(cur_slot, zero, p), jnp.where(cur_slot, p, zero)], axis=1)
        outs.append(_dot(p_band, v_var[kvh][half]))
    for kvh in range(N_KV_HEADS):
        for j in range(PAIRS_PER_KV):
            pair = kvh * PAIRS_PER_KV + j
            out = outs[2 * kvh][j * BLOCK:(j + 1) * BLOCK, :] + outs[2 * kvh + 1][j * BLOCK:(j + 1) * BLOCK, :]
            o_ref[:, pair * LANES:(pair + 1) * LANES] = out.astype(o_ref.dtype)


def _swa(q, kv, rel_bias, sinks, batch):
    t = q.shape[0]
    nb = t // batch // BLOCK
    rows = N_Q_HEADS * BLOCK
    bucket = jnp.asarray(_bucket_table())
    slot_sinks = sinks.astype(F32)[np.array([_swa_slot_head(s) for s in range(N_Q_HEADS)])]
    sink_rows = jnp.broadcast_to(jnp.repeat(slot_sinks, BLOCK)[:, None], (rows, LANES))
    smem = pl.BlockSpec(memory_space=pltpu.SMEM)
    return pl.pallas_call(
        _swa_kernel,
        grid=(batch, nb),
        in_specs=[smem, _resident((BLOCK, BLOCK)), _resident((rows, LANES)),
                  pl.BlockSpec((BLOCK, Q_W), lambda b, i: (b * nb + i, 0)),
                  pl.BlockSpec((BLOCK, 2 * KV_W), lambda b, i: (b * nb + jnp.maximum(i - 1, 0), 0)),
                  pl.BlockSpec((BLOCK, 2 * KV_W), lambda b, i: (b * nb + i, 0))],
        out_specs=pl.BlockSpec((BLOCK, Q_W), lambda b, i: (b * nb + i, 0)),
        out_shape=jax.ShapeDtypeStruct((t, Q_W), BF16),
        scratch_shapes=[pltpu.VMEM((2, rows, BLOCK), F32), pltpu.VMEM((rows, BLOCK), F32),
                        pltpu.VMEM((rows, BLOCK), BF16)],
        compiler_params=_params(2),
        name="swa",
    )(rel_bias, bucket, sink_rows, q, kv, kv)


TAIL_TILE = 512
FF_CHUNK = 256


def _cross_attention(xq_ref, mk_ref, mv_ref, cross_scr):
    for h in range(N_X_HEADS):
        sl = slice(h * X_HEAD_DIM, (h + 1) * X_HEAD_DIM)
        s = lax.dot_general(xq_ref[:, sl], mk_ref[:, sl], NT_DIMS, preferred_element_type=F32)
        s = s * (X_HEAD_DIM ** -0.5)
        p = jnp.exp(s - jnp.max(s, axis=-1, keepdims=True))
        p = p / jnp.sum(p, axis=-1, keepdims=True)
        cross_scr[:, sl] = _dot(p.astype(BF16), mv_ref[:, sl]).astype(cross_scr.dtype)


def _swiglu_residual(x1, g_ref, wg_ref, wu_ref, wd_ref, act_scr):
    h = _rms(x1, g_ref[...]).astype(BF16)
    for c0 in range(0, D_FF, FF_CHUNK):
        sl = slice(c0, c0 + FF_CHUNK)
        act_scr[:, sl] = (_silu(_dot(h, wg_ref[:, sl])) * _dot(h, wu_ref[:, sl])).astype(BF16)
    return x1 + _dot(act_scr[...], wd_ref[...])


def _tail_specs(batch, t, a_width, tile=TAIL_TILE):
    nt = t // batch // tile
    tok = lambda w: pl.BlockSpec((tile, w), lambda b, i: (b * nt + i, 0))
    mem = lambda half: pl.BlockSpec((MEM_LEN, XQ_W), lambda b, i: (b, half))
    return nt, tok, [tok(D_MODEL), tok(a_width), tok(XQ_W), mem(0), mem(1),
                     _resident((a_width, D_MODEL)), _resident((XQ_W, D_MODEL))]


def _attn_tail_kernel(x_ref, a_ref, xq_ref, mk_ref, mv_ref, wa_ref, wc_ref, g_ref, wg_ref, wu_ref, wd_ref,
                      o_ref, cross_scr, act_scr):
    _cross_attention(xq_ref, mk_ref, mv_ref, cross_scr)
    x1 = x_ref[...] + _dot(a_ref[...], wa_ref[...]) + _dot(cross_scr[...], wc_ref[...])
    o_ref[...] = _swiglu_residual(x1, g_ref, wg_ref, wu_ref, wd_ref, act_scr)


def _attn_tail(x, attn, xq, mem_kv, wa, wc, g, wg, wu, wd, batch):
    t, d = x.shape
    nt, tok, specs = _tail_specs(batch, t, attn.shape[1])
    return pl.pallas_call(
        _attn_tail_kernel,
        grid=(batch, nt),
        in_specs=specs + [_resident((1, d)), _resident(wg.shape), _resident(wu.shape), _resident(wd.shape)],
        out_specs=tok(d),
        out_shape=jax.ShapeDtypeStruct((t, d), F32),
        scratch_shapes=[pltpu.VMEM((TAIL_TILE, XQ_W), BF16), pltpu.VMEM((TAIL_TILE, D_FF), BF16)],
        compiler_params=_params(2),
        name="attn_tail",
    )(x, attn, xq, mem_kv, mem_kv, wa, wc, g.reshape(1, d), wg, wu, wd)


TOP_K = 2
HI16 = 0xFFFF0000
META_EXPERT, META_POS, META_GATE = 0, 2, 4


def _pack_bf16_pairs(h):
    n = h.shape[1] // 2
    u = pltpu.bitcast(h.astype(BF16).astype(F32), jnp.uint32)
    return (u[:, :n] >> 16) | (u[:, n:] & jnp.uint32(HI16))


def _unpack_bf16_pairs(w):
    lo = pltpu.bitcast(w << 16, F32).astype(BF16)
    hi = pltpu.bitcast(w & jnp.uint32(HI16), F32).astype(BF16)
    return jnp.concatenate([lo, hi], axis=1)


def _route(x1, g_ref, wr_ref, meta_ref, meta_t_ref, cnt_ref, hpk_ref, run_scr):
    h = _rms(x1, g_ref[...])
    hpk_ref[...] = _pack_bf16_pairs(h)
    tm = h.shape[0]
    prod = _dot(jnp.concatenate(_split_bf16(h, 2), axis=0), wr_ref[...])
    prod = prod[:tm, :] + prod[tm:, :]
    logits = prod[:, :LANES] + prod[:, LANES:]
    lt = logits.T[:N_EXPERTS, :]
    sub = lax.broadcasted_iota(jnp.int32, lt.shape, 0)
    m1 = jnp.max(lt, axis=0, keepdims=True)
    i1 = jnp.min(jnp.where(lt == m1, sub, N_EXPERTS), axis=0, keepdims=True)
    lt2 = jnp.where(sub == i1, NEG_INF, lt)
    m2 = jnp.max(lt2, axis=0, keepdims=True)
    i2 = jnp.min(jnp.where(lt2 == m2, sub, N_EXPERTS), axis=0, keepdims=True)
    e = jnp.exp(m2 - m1)
    g1, g2 = 1.0 / (1.0 + e), e / (1.0 + e)

    onehot = jnp.where(jnp.logical_or(sub == i1, sub == i2), 1.0, 0.0)
    onehot16 = jnp.concatenate([onehot, jnp.zeros_like(onehot)], axis=0).astype(BF16)
    earlier = lax.broadcasted_iota(jnp.int32, (tm, tm), 0) < lax.broadcasted_iota(jnp.int32, (tm, tm), 1)
    before = _dot(onehot16, earlier.astype(BF16))[:N_EXPERTS, :] + run_scr[:, 0:1]
    pos1 = jnp.sum(jnp.where(sub == i1, before, 0.0), axis=0, keepdims=True)
    pos2 = jnp.sum(jnp.where(sub == i2, before, 0.0), axis=0, keepdims=True)
    run_scr[...] += jnp.sum(onehot, axis=1, keepdims=True)
    cnt_ref[...] = run_scr[...]

    def put(k, v):
        return jnp.where(sub == k, v, 0.0)

    meta_t = (put(META_EXPERT, i1.astype(F32)) + put(META_EXPERT + 1, i2.astype(F32))
              + put(META_POS, pos1) + put(META_POS + 1, pos2)
              + put(META_GATE, g1) + put(META_GATE + 1, g2))
    meta_t_ref[...] = meta_t
    meta_ref[...] = jnp.concatenate([meta_t, jnp.zeros((LANES - N_EXPERTS, tm), F32)], axis=0).T


def _ssd_tail_kernel(x_ref, y_ref, xq_ref, mk_ref, mv_ref, wy_ref, wc_ref, g_ref, wr_ref,
                     x1_ref, meta_ref, meta_t_ref, cnt_ref, hpk_ref, run_scr, *cross_scrs):
    @pl.when((pl.program_id(0) == 0) & (pl.program_id(1) == 0))
    def _():
        run_scr[...] = jnp.zeros_like(run_scr)

    for s in range(x_ref.shape[0] // TAIL_TILE):
        rows = pl.ds(s * TAIL_TILE, TAIL_TILE)
        _cross_attention(xq_ref.at[rows], mk_ref, mv_ref, cross_scrs[s])
        x1_ref[rows, :] = x_ref[rows, :] + _dot(y_ref[rows, :], wy_ref[...]) + _dot(cross_scrs[s][...], wc_ref[...])
    _route(x1_ref[...], g_ref, wr_ref, meta_ref, meta_t_ref, cnt_ref, hpk_ref, run_scr)


SSD_TAIL_SUBTILES = 2


def _ssd_tail(x, y, xq, mem_kv, wy, wc, g, w_router, batch):
    t, d = x.shape
    tile = SSD_TAIL_SUBTILES * TAIL_TILE
    nt, tok, specs = _tail_specs(batch, t, y.shape[1], tile)
    w1, w2 = _split_bf16(w_router.astype(F32), 2)
    wr = jnp.zeros((d, 2 * LANES), BF16).at[:, :N_EXPERTS].set(w1).at[:, LANES:LANES + N_EXPERTS].set(w2)
    return pl.pallas_call(
        _ssd_tail_kernel,
        grid=(batch, nt),
        in_specs=specs + [_resident((1, d)), _resident((d, 2 * LANES))],
        out_specs=[tok(d), tok(LANES), pl.BlockSpec((N_EXPERTS, tile), lambda b, i: (0, b * nt + i)),
                   pl.BlockSpec((N_EXPERTS, LANES), lambda b, i: (0, 0)), tok(d // 2)],
        out_shape=[jax.ShapeDtypeStruct((t, d), F32), jax.ShapeDtypeStruct((t, LANES), F32),
                   jax.ShapeDtypeStruct((N_EXPERTS, t), F32), jax.ShapeDtypeStruct((N_EXPERTS, LANES), F32),
                   jax.ShapeDtypeStruct((t, d // 2), jnp.uint32)],
        scratch_shapes=[pltpu.VMEM((N_EXPERTS, LANES), F32)]
        + [pltpu.VMEM((TAIL_TILE, XQ_W), BF16)] * SSD_TAIL_SUBTILES,
        compiler_params=_params(2),
        name="ssd_tail",
    )(x, y, xq, mem_kv, mem_kv, wy, wc, g.reshape(1, d), wr)


ROW_TILE = 512
DISPATCH_TILE = 2048
COMBINE_TILE = 1024


def _row_copy_all_wait(src_like, dst_like, sem):
    pltpu.make_async_copy(src_like, dst_like, sem).wait()


def _dispatch_kernel(pad_start_ref, dest_hbm, hpk_ref, xs_out, idx_smem, zero_buf, idx_sem, row_sem, fill_sem):
    tm = hpk_ref.shape[0]

    @pl.when(pl.program_id(0) == 0)
    def _():
        zero_buf[...] = jnp.zeros_like(zero_buf)
        fill_rows = zero_buf.shape[0]
        starts = [pad_start_ref[e] // SUBLANES * SUBLANES for e in range(N_EXPERTS)]
        last = xs_out.shape[0] - fill_rows
        n_tail = -(-(N_EXPERTS + 1) * ROW_TILE // fill_rows)
        starts += [jnp.minimum(pad_start_ref[N_EXPERTS] + k * fill_rows, last) for k in range(n_tail)]
        for s in starts:
            fill = pltpu.make_async_copy(zero_buf, xs_out.at[pl.ds(pl.multiple_of(s, SUBLANES), fill_rows)], fill_sem)
            fill.start()
            fill.wait()

    idx_copy = pltpu.make_async_copy(dest_hbm.at[pl.program_id(0)], idx_smem, idx_sem)
    idx_copy.start()
    idx_copy.wait()

    def issue(t, carry):
        for k in range(TOP_K):
            row = idx_smem[k * tm + t]
            pltpu.make_async_copy(hpk_ref.at[pl.ds(t, 1)], xs_out.at[pl.ds(row, 1)], row_sem).start(priority=k)
        return carry

    lax.fori_loop(0, tm, issue, 0, unroll=8)
    for _ in range(TOP_K):
        _row_copy_all_wait(hpk_ref, xs_out.at[pl.ds(0, tm)], row_sem)


def _dispatch(pad_start, dest_tiles, hpk, n_rows):
    t, w = hpk.shape
    tm = dest_tiles.shape[1] // TOP_K
    any_spec = pl.BlockSpec(memory_space=pl.ANY)
    return pl.pallas_call(
        _dispatch_kernel,
        grid_spec=pltpu.PrefetchScalarGridSpec(
            num_scalar_prefetch=1,
            grid=(t // tm,),
            in_specs=[any_spec, pl.BlockSpec((tm, w), lambda i, ps: (i, 0))],
            out_specs=any_spec,
            scratch_shapes=[pltpu.SMEM((TOP_K * tm,), jnp.int32), pltpu.VMEM((ROW_TILE + SUBLANES, w), jnp.uint32),
                            pltpu.SemaphoreType.DMA, pltpu.SemaphoreType.DMA, pltpu.SemaphoreType.DMA]),
        out_shape=jax.ShapeDtypeStruct((n_rows, w), jnp.uint32),
        compiler_params=_params(1),
        name="dispatch",
    )(pad_start, dest_tiles, hpk)


def _experts_kernel(tile_expert_ref, n_used_ref, xs_ref, wg_ref, wu_ref, wd_ref, y_ref, act_scr):
    del tile_expert_ref

    @pl.when(pl.program_id(0) < n_used_ref[0])
    def _():
        h = _unpack_bf16_pairs(xs_ref[...])
        for c0 in range(0, D_FF, FF_CHUNK):
            sl = slice(c0, c0 + FF_CHUNK)
            act_scr[:, sl] = (_silu(_dot(h, wg_ref[:, sl])) * _dot(h, wu_ref[:, sl])).astype(BF16)
        y_ref[...] = _dot(act_scr[...], wd_ref[...])

    @pl.when(pl.program_id(0) >= n_used_ref[0])
    def _():
        y_ref[...] = jnp.zeros_like(y_ref)


def _experts(tile_expert, n_used, xs, wg, wu, wd, layer):
    n_rows, w = xs.shape
    d = 2 * w
    tm = ROW_TILE
    wspec = lambda shape: pl.BlockSpec((None, None) + shape, lambda i, te, nu: (layer, te[i], 0, 0))
    return pl.pallas_call(
        _experts_kernel,
        grid_spec=pltpu.PrefetchScalarGridSpec(
            num_scalar_prefetch=2,
            grid=(n_rows // tm,),
            in_specs=[pl.BlockSpec((tm, w), lambda i, te, nu: (jnp.minimum(i, nu[0] - 1), 0)),
                      wspec((d, D_FF)), wspec((d, D_FF)), wspec((D_FF, d))],
            out_specs=pl.BlockSpec((tm, d), lambda i, te, nu: (i, 0)),
            scratch_shapes=[pltpu.VMEM((tm, D_FF), BF16)]),
        out_shape=jax.ShapeDtypeStruct((n_rows, d), F32),
        compiler_params=_params(1),
        name="experts",
    )(tile_expert, n_used, xs, wg, wu, wd)


def _combine_kernel(dest_hbm, x_ref, meta_ref, y_hbm, gf_ref, o_ref, idx_smem, y_buf, idx_sem, row_sem, *,
                    final_norm):
    tm = x_ref.shape[0]
    idx_copy = pltpu.make_async_copy(dest_hbm.at[pl.program_id(0)], idx_smem, idx_sem)
    idx_copy.start()
    idx_copy.wait()

    def issue(t, carry):
        for k in range(TOP_K):
            row = idx_smem[k * tm + t]
            pltpu.make_async_copy(y_hbm.at[pl.ds(row, 1)], y_buf.at[k, pl.ds(t, 1)], row_sem).start(priority=k)
        return carry

    lax.fori_loop(0, tm, issue, 0, unroll=8)
    for k in range(TOP_K):
        _row_copy_all_wait(y_hbm.at[pl.ds(0, tm)], y_buf.at[k], row_sem)
    meta = meta_ref[...]
    out = x_ref[...]
    for k in range(TOP_K):
        out = out + meta[:, META_GATE + k:META_GATE + k + 1] * y_buf[k]
    o_ref[...] = _rms(out, gf_ref[...]) if final_norm else out


def _combine(dest_tiles, x, meta, y, g_final):
    t, d = x.shape
    tm = dest_tiles.shape[1] // TOP_K
    any_spec = pl.BlockSpec(memory_space=pl.ANY)
    final_norm = g_final is not None
    gf = (g_final if final_norm else jnp.ones((d,), F32)).reshape(1, d)
    return pl.pallas_call(
        functools.partial(_combine_kernel, final_norm=final_norm),
        grid=(t // tm,),
        in_specs=[any_spec, pl.BlockSpec((tm, d), lambda i: (i, 0)),
                  pl.BlockSpec((tm, LANES), lambda i: (i, 0)), any_spec, _resident((1, d))],
        out_specs=pl.BlockSpec((tm, d), lambda i: (i, 0)),
        out_shape=jax.ShapeDtypeStruct((t, d), F32),
        scratch_shapes=[pltpu.SMEM((TOP_K * tm,), jnp.int32), pltpu.VMEM((TOP_K, tm, d), F32),
                        pltpu.SemaphoreType.DMA, pltpu.SemaphoreType.DMA],
        compiler_params=_params(1),
        name="combine",
    )(dest_tiles, x, meta, y, gf)


def _moe(x, meta, meta_t, counts, hpk, wg, wu, wd, layer, g_final=None):
    t, d = x.shape
    cnt = counts[:, 0].astype(jnp.int32)
    padded = (cnt + ROW_TILE - 1) // ROW_TILE * ROW_TILE
    ends = jnp.cumsum(padded)
    starts = ends - padded
    expert = meta_t[META_EXPERT:META_EXPERT + TOP_K].astype(jnp.int32)
    dest = meta_t[META_POS:META_POS + TOP_K].astype(jnp.int32)
    for e in range(N_EXPERTS):
        dest = dest + jnp.where(expert == e, starts[e], 0)

    def dest_tiles(tile):
        return dest.reshape(TOP_K, t // tile, tile).transpose(1, 0, 2).reshape(-1, TOP_K * tile)

    n_rows = TOP_K * t + (N_EXPERTS + 1) * ROW_TILE
    n_tiles = n_rows // ROW_TILE
    n_used = ends[-1] // ROW_TILE
    tile_start = jnp.minimum(jnp.arange(n_tiles), n_used - 1) * ROW_TILE
    tile_expert = jnp.sum(tile_start[:, None] >= ends[None, :], axis=1).astype(jnp.int32)
    pad_start = jnp.concatenate([starts + cnt, ends[-1:]])
    xs = _dispatch(pad_start, dest_tiles(DISPATCH_TILE), hpk, n_rows)
    y = _experts(tile_expert, n_used.reshape(1).astype(jnp.int32), xs, wg, wu, wd, layer)
    return _combine(dest_tiles(COMBINE_TILE), x, meta, y, g_final)


CONV_TAIL = 16


def _conv_shift_matrix():
    m = np.zeros(((CONV_WIDTH - 1) * CHUNK, CONV_TAIL + CHUNK), np.float32)
    for k in range(CONV_WIDTH - 1):
        for t in range(CHUNK):
            m[k * CHUNK + t, CONV_TAIL + t - (CONV_WIDTH - 1) + k] = 1.0
    return m


def _split_bf16(v, parts):
    out, r = [], v
    for _ in range(parts):
        p = r.astype(BF16)
        out.append(p)
        r = r - p.astype(F32)
    return out


def _ssd_kernel(xbc_ref, z_ref, dt_ref, cw_ref, cb_ref, dtb_ref, alog_ref, dsk_ref, gn_ref, exp_ref, shift_ref,
                o_ref, state_scr, *chunk_scr):
    n_sub = xbc_ref.shape[0] // CHUNK
    ext_scrs, y_scrs = chunk_scr[:n_sub], chunk_scr[n_sub:]

    @pl.when(pl.program_id(1) == 0)
    def _():
        state_scr[...] = jnp.zeros_like(state_scr)
        ext_scrs[n_sub - 1][CHUNK:, :] = jnp.zeros((CONV_TAIL, CONV_CH), BF16)

    for c in range(n_sub):
        rows = pl.ds(c * CHUNK, CHUNK)
        _ssd_chunk(xbc_ref.at[rows], z_ref.at[rows], dt_ref.at[rows], cw_ref, cb_ref, dtb_ref, alog_ref, dsk_ref,
                   gn_ref, exp_ref, shift_ref, o_ref.at[rows], state_scr, ext_scrs[c],
                   ext_scrs[(c - 1) % n_sub], y_scrs[c])


def _ssd_chunk(xbc_ref, z_ref, dt_ref, cw_ref, cb_ref, dtb_ref, alog_ref, dsk_ref, gn_ref, exp_ref, shift_ref,
               o_ref, state_scr, ext_scr, prev_ext_scr, y_scr):
    ext_scr[0:CONV_TAIL, :] = prev_ext_scr[CHUNK:, :]
    ext_scr[CONV_TAIL:, :] = xbc_ref[...]

    def conv_silu(c0, c1):
        shifted = _dot(shift_ref[...], ext_scr[:, c0:c1])
        acc = cb_ref[:, c0:c1] + cw_ref[CONV_WIDTH - 1:CONV_WIDTH, c0:c1] * ext_scr[CONV_TAIL:, c0:c1].astype(F32)
        for k in range(CONV_WIDTH - 1):
            acc = acc + cw_ref[k:k + 1, c0:c1] * shifted[k * CHUNK:(k + 1) * CHUNK, :]
        return _silu(acc)

    dt = dt_ref[...] + dtb_ref[...]
    dt = jnp.maximum(dt, 0.0) + jnp.log(1.0 + jnp.exp(-jnp.abs(dt)))
    da = dt * -jnp.exp(alog_ref[...])
    row = lax.broadcasted_iota(jnp.int32, (CHUNK, CHUNK), 0)
    col = lax.broadcasted_iota(jnp.int32, (CHUNK, CHUNK), 1)
    causal = col <= row
    cs = _dot(causal.astype(BF16), jnp.concatenate(_split_bf16(da, 3), axis=1))
    acum = cs[:, 0:LANES] + cs[:, LANES:2 * LANES] + cs[:, 2 * LANES:]
    acum_t = acum.T
    ea = jnp.exp(acum)
    dtte = dt * jnp.exp(acum[CHUNK - 1:CHUNK, :] - acum)

    per_head = jnp.concatenate(
        [jnp.concatenate(_split_bf16(v, 2), axis=1) for v in (ea, dt, dtte)], axis=0)

    low = lax.broadcasted_iota(jnp.int32, (CHUNK, LANES), 1) < SSM_HEAD_DIM
    for g in range(N_SSM_GROUPS):
        gsl = slice(g * GROUP_W, (g + 1) * GROUP_W)
        xg = conv_silu(g * GROUP_W, (g + 1) * GROUP_W)
        bg = conv_silu(D_INNER + g * D_STATE, D_INNER + (g + 1) * D_STATE)
        cg = conv_silu(D_INNER + BC_W + g * D_STATE, D_INNER + BC_W + (g + 1) * D_STATE).astype(BF16)
        expanded = _dot(per_head, exp_ref[:, gsl])
        ea_x, dt_x, dtte_x = (expanded[k * CHUNK:(k + 1) * CHUNK, :] for k in range(3))
        xdt = (xg * dt_x).astype(BF16)
        cb = lax.dot_general(cg, bg.astype(BF16), NT_DIMS, preferred_element_type=F32)
        state = state_scr[g]
        y = _dot(cg, state.astype(BF16)) * ea_x + xg * dsk_ref[:, gsl]
        diag = []
        for pair in range(HEADS_PER_GROUP // 2):
            xp = xdt[:, pair * LANES:(pair + 1) * LANES]
            acc = jnp.zeros((CHUNK, LANES), F32)
            for half in range(2):
                h = g * HEADS_PER_GROUP + 2 * pair + half
                seg = acum[:, h:h + 1] - acum_t[h:h + 1, :]
                m = (cb * jnp.exp(jnp.where(causal, seg, NEG_INF))).astype(BF16)
                xm = jnp.where(low if half == 0 else jnp.logical_not(low), xp, jnp.zeros_like(xp))
                acc = acc + _dot(m, xm)
            diag.append(acc)
        y_scr[:, gsl] = y + jnp.concatenate(diag, axis=1)
        w = (xg * dtte_x).astype(BF16)
        state_scr[g] = state * ea_x[CHUNK - 1:CHUNK, :] + _dot(bg.T.astype(BF16), w)

    z = z_ref[...].astype(F32)
    o_ref[...] = _rms(y_scr[...] * _silu(z), gn_ref[...]).astype(o_ref.dtype)


SSD_SUBCHUNKS = 2


def _ssd(xbc, z, dt_raw, conv_w, conv_b, dt_bias, a_log, d_skip, g_norm, batch):
    t = xbc.shape[0]
    step = SSD_SUBCHUNKS * CHUNK
    nc = t // batch // step

    def lane_pad(v):
        return jnp.zeros((1, LANES), F32).at[0, :N_SSM_HEADS].set(v)

    expand = np.zeros((LANES, D_INNER), np.float32)
    for h in range(N_SSM_HEADS):
        expand[h, h * SSM_HEAD_DIM:(h + 1) * SSM_HEAD_DIM] = 1.0
    expand2 = jnp.asarray(np.concatenate([expand, expand], axis=0), BF16)
    tok = lambda b, c: (b * nc + c, 0)
    return pl.pallas_call(
        _ssd_kernel,
        grid=(batch, nc),
        in_specs=[pl.BlockSpec((step, CONV_CH), tok), pl.BlockSpec((step, D_INNER), tok),
                  pl.BlockSpec((step, LANES), tok),
                  _resident((CONV_WIDTH, CONV_CH)), _resident((1, CONV_CH)),
                  _resident((1, LANES)), _resident((1, LANES)),
                  _resident((1, D_INNER)), _resident((1, D_INNER)), _resident((2 * LANES, D_INNER)),
                  _resident(((CONV_WIDTH - 1) * CHUNK, CONV_TAIL + CHUNK))],
        out_specs=pl.BlockSpec((step, D_INNER), tok),
        out_shape=jax.ShapeDtypeStruct((t, D_INNER), BF16),
        scratch_shapes=[pltpu.VMEM((N_SSM_GROUPS, D_STATE, GROUP_W), F32)]
        + [pltpu.VMEM((CONV_TAIL + CHUNK, CONV_CH), BF16)] * SSD_SUBCHUNKS
        + [pltpu.VMEM((CHUNK, D_INNER), F32)] * SSD_SUBCHUNKS,
        compiler_params=_params(2),
        name="ssd",
    )(xbc, z, dt_raw, conv_w, conv_b.reshape(1, CONV_CH), lane_pad(dt_bias), lane_pad(a_log),
      jnp.repeat(d_skip, SSM_HEAD_DIM).reshape(1, D_INNER), g_norm.reshape(1, D_INNER), expand2,
      jnp.asarray(_conv_shift_matrix(), BF16))


def kernel(x, mem, g_mix, g_ffn, g_mem, w_mem_kv, rel_bias, swa_w_in, swa_sinks, swa_w_out, ssm_w_in, ssm_conv_w, ssm_conv_b, ssm_dt_bias, ssm_A_log, ssm_D, ssm_g_norm, ssm_w_out, ffn_w_gate, ffn_w_up, ffn_w_down, moe_w_router, moe_w_gate, moe_w_up, moe_w_down, g_final):
    batch, seq, d = x.shape
    xf = x.reshape(batch * seq, d)
    memf = mem.reshape(batch * MEM_LEN, d)
    moe_wg, moe_wu, moe_wd = moe_w_gate.astype(BF16), moe_w_up.astype(BF16), moe_w_down.astype(BF16)
    for i in range(DEPTH):
        j = i // 2
        (mem_kv,) = _norm_proj(memf, g_mem[i], [w_mem_kv[i].astype(BF16)], [BF16])
        if i % 2 == 0:
            w_in = swa_w_in[j]
            w_q = (w_in[:, :Q_W] * HEAD_DIM ** -0.5).astype(BF16)
            q, kv, xq = _norm_proj(
                xf, g_mix[i], [w_q, w_in[:, Q_W:Q_W + 2 * KV_W].astype(BF16), w_in[:, Q_W + 2 * KV_W:].astype(BF16)],
                [BF16, BF16, BF16])
            attn = _swa(q, kv, rel_bias, swa_sinks[j], batch)
            w_out = swa_w_out[j].astype(BF16)
            xf = _attn_tail(xf, attn, xq, mem_kv, w_out[:Q_W], w_out[Q_W:], g_ffn[i],
                            ffn_w_gate[j].astype(BF16), ffn_w_up[j].astype(BF16), ffn_w_down[j].astype(BF16),
                            batch)
        else:
            w_in = ssm_w_in[j].astype(BF16)
            o_dt = D_INNER + CONV_CH
            w_dt = jnp.zeros((d, LANES), BF16).at[:, :N_SSM_HEADS].set(w_in[:, o_dt:o_dt + N_SSM_HEADS])
            z, xbc, dt_raw, xq = _norm_proj(
                xf, g_mix[i], [w_in[:, :D_INNER], w_in[:, D_INNER:o_dt], w_dt, w_in[:, o_dt + N_SSM_HEADS:]],
                [BF16, BF16, F32, BF16])
            y = _ssd(xbc, z, dt_raw, ssm_conv_w[j], ssm_conv_b[j], ssm_dt_bias[j], ssm_A_log[j],
                     ssm_D[j], ssm_g_norm[j], batch)
            w_out = ssm_w_out[j].astype(BF16)
            xf, meta, meta_t, counts, hpk = _ssd_tail(xf, y, xq, mem_kv, w_out[:D_INNER], w_out[D_INNER:],
                                                      g_ffn[i], moe_w_router[j], batch)
            xf = _moe(xf, meta, meta_t, counts, hpk, moe_wg, moe_wu, moe_wd, j,
                      g_final if i == DEPTH - 1 else None)
    assert DEPTH % 2 == 0
    return xf.reshape(batch, seq, d)
```

```python
import functools
import math

import numpy as np
import jax
import jax.numpy as jnp
from jax import lax
from jax.experimental import pallas as pl
from jax.experimental.pallas import tpu as pltpu

F32 = jnp.float32
BF16 = jnp.bfloat16

D_MODEL = 1024
DEPTH = 4
MEM_LEN = 256
EPS = 1e-6
N_Q_HEADS = 16
N_KV_HEADS = 2
HEAD_DIM = 64
BLOCK = 128
N_BUCKETS = 32
MAX_DISTANCE = 128
N_X_HEADS = 4
X_HEAD_DIM = 256
D_INNER = 2048
SSM_HEAD_DIM = 64
N_SSM_HEADS = 32
N_SSM_GROUPS = 4
HEADS_PER_GROUP = 8
D_STATE = 128
CONV_WIDTH = 4
CHUNK = 128
D_FF = 2816
N_EXPERTS = 8
Q_W = N_Q_HEADS * HEAD_DIM
KV_W = N_KV_HEADS * HEAD_DIM
XQ_W = N_X_HEADS * X_HEAD_DIM
BC_W = N_SSM_GROUPS * D_STATE
CONV_CH = D_INNER + 2 * BC_W
GROUP_W = HEADS_PER_GROUP * SSM_HEAD_DIM

LANES = 128
SUBLANES = 8
VMEM_LIMIT = 56 << 20
NEG_INF = float("-inf")
NT_DIMS = (((1,), (1,)), ((), ()))


def _params(n_axes, vmem=VMEM_LIMIT):
    return pltpu.CompilerParams(dimension_semantics=("arbitrary",) * n_axes, vmem_limit_bytes=vmem)


def _resident(shape):
    nd = len(shape)
    return pl.BlockSpec(shape, lambda *_: (0,) * nd, pipeline_mode=pl.Buffered(1))


def _dot(a, b):
    return jnp.dot(a, b, preferred_element_type=F32)


def _rms(x, g):
    return x * lax.rsqrt(jnp.mean(x * x, axis=-1, keepdims=True) + EPS) * g


def _silu(v):
    return v / (1.0 + jnp.exp(-v))


def _norm_proj_kernel(x_ref, g_ref, *refs, n_out, col_chunk):
    w_refs, o_refs = refs[:n_out], refs[n_out:]
    h = _rms(x_ref[...], g_ref[...]).astype(BF16)
    for w_ref, o_ref in zip(w_refs, o_refs):
        n = w_ref.shape[1]
        for c0 in range(0, n, col_chunk):
            c1 = min(c0 + col_chunk, n)
            o_ref[:, c0:c1] = _dot(h, w_ref[:, c0:c1]).astype(o_ref.dtype)


def _norm_proj(x, g, ws, out_dtypes, tm=512):
    t, d = x.shape
    n_out = len(ws)
    return pl.pallas_call(
        functools.partial(_norm_proj_kernel, n_out=n_out, col_chunk=512),
        grid=(t // tm,),
        in_specs=[pl.BlockSpec((tm, d), lambda i: (i, 0)), _resident((1, d))]
        + [_resident(w.shape) for w in ws],
        out_specs=[pl.BlockSpec((tm, w.shape[1]), lambda i: (i, 0)) for w in ws],
        out_shape=[jax.ShapeDtypeStruct((t, w.shape[1]), dt) for w, dt in zip(ws, out_dtypes)],
        compiler_params=_params(1),
        name="norm_proj",
    )(x, g.reshape(1, d), *ws)


def _bucket_table():
    qi = np.arange(BLOCK)[:, None]
    kj = np.arange(2 * BLOCK)[None, :]
    dist = BLOCK + qi - kj
    max_exact = N_BUCKETS // 2
    d = np.maximum(dist, 0)
    df = np.maximum(d, 1).astype(np.float32)
    far = max_exact + (
        np.log(df / np.float32(max_exact)) / np.float32(math.log(MAX_DISTANCE / max_exact))
        * np.float32(N_BUCKETS - max_exact)
    ).astype(np.int32)
    bucket = np.where(d < max_exact, d, np.minimum(far, N_BUCKETS - 1))
    r = np.arange(BLOCK)[:, None]
    c = np.arange(BLOCK)[None, :]
    return np.where(c <= r, bucket[:, BLOCK:], bucket[:, :BLOCK]).astype(np.int32)


PAIRS_PER_KV = N_Q_HEADS // N_KV_HEADS // 2
SLOT_ROWS = PAIRS_PER_KV * BLOCK


def _swa_slot_head(slot):
    kvh, rest = divmod(slot, 2 * PAIRS_PER_KV)
    half, pair = divmod(rest, PAIRS_PER_KV)
    return 2 * (kvh * PAIRS_PER_KV + pair) + half


def _swa_kernel(relb_ref, bucket_ref, sink_ref, q_ref, kvp_ref, kvc_ref, o_ref, bias_scr, s_scr, p_scr):
    first = (pl.program_id(0) == 0) & (pl.program_id(1) == 0)

    def from_current(n_rows):
        r = lax.broadcasted_iota(jnp.int32, (n_rows, BLOCK), 0) & (BLOCK - 1)
        return lax.broadcasted_iota(jnp.int32, (n_rows, BLOCK), 1) <= r

    @pl.when(first)
    def _():
        bucket = bucket_ref[...]
        for slot in range(N_Q_HEADS):
            h = _swa_slot_head(slot)
            acc = jnp.zeros((BLOCK, BLOCK), F32)
            for n in range(N_BUCKETS):
                acc = jnp.where(bucket == n, relb_ref[n, h], acc)
            rows = slice(slot * BLOCK, (slot + 1) * BLOCK)
            bias_scr[1, rows, :] = acc
            bias_scr[0, rows, :] = jnp.where(from_current(BLOCK), acc, NEG_INF)

    kk = jnp.concatenate([kvp_ref[:, 0:KV_W], kvc_ref[:, 0:KV_W]], axis=0).astype(F32)
    vv = jnp.concatenate([kvp_ref[:, KV_W:], kvc_ref[:, KV_W:]], axis=0).astype(F32)
    low = lax.broadcasted_iota(jnp.int32, kk.shape, 1) < HEAD_DIM

    def placed(t):
        r = pltpu.roll(t, HEAD_DIM, 1)
        return [[jnp.where(low, t, 0.0).astype(BF16), jnp.where(low, 0.0, r).astype(BF16)],
                [jnp.where(low, r, 0.0).astype(BF16), jnp.where(low, 0.0, t).astype(BF16)]]

    k_var, v_var = placed(kk), placed(vv)

    cur_slot = from_current(SLOT_ROWS)
    for v in range(2 * N_KV_HEADS):
        kvh, half = divmod(v, 2)
        q_stack = jnp.concatenate(
            [q_ref[:, (kvh * PAIRS_PER_KV + j) * LANES:(kvh * PAIRS_PER_KV + j + 1) * LANES]
             for j in range(PAIRS_PER_KV)], axis=0)
        s = lax.dot_general(q_stack, k_var[kvh][half], NT_DIMS, preferred_element_type=F32)
        s_scr[v * SLOT_ROWS:(v + 1) * SLOT_ROWS, :] = jnp.where(cur_slot, s[:, BLOCK:], s[:, :BLOCK])

    s = s_scr[...] + bias_scr[jnp.minimum(pl.program_id(1), 1)]
    sink = sink_ref[...]
    m = jnp.maximum(jnp.max(s, axis=-1, keepdims=True), sink)
    p = jnp.exp(s - m)
    p = p / (jnp.sum(p, axis=-1, keepdims=True) + jnp.exp(sink - m))
    p_scr[...] = p.astype(BF16)

    outs = []
    for v in range(2 * N_KV_HEADS):
        kvh, half = divmod(v, 2)
        p = p_scr[v * SLOT_ROWS:(v + 1) * SLOT_ROWS, :]
        zero = jnp.zeros_like(p)
        p_band = jnp.concatenate([jnp.where(cur_slot, zero, p), jnp.where(cur_slot, p, zero)], axis=1)
        outs.append(_dot(p_band, v_var[kvh][half]))
    for kvh in range(N_KV_HEADS):
        for j in range(PAIRS_PER_KV):
            pair = kvh * PAIRS_PER_KV + j
            out = outs[2 * kvh][j * BLOCK:(j + 1) * BLOCK, :] + outs[2 * kvh + 1][j * BLOCK:(j + 1) * BLOCK, :]
            o_ref[:, pair * LANES:(pair + 1) * LANES] = out.astype(o_ref.dtype)


def _swa(q, kv, rel_bias, sinks, batch):
    t = q.shape[0]
    nb = t // batch // BLOCK
    rows = N_Q_HEADS * BLOCK
    bucket = jnp.asarray(_bucket_table())
    slot_sinks = sinks.astype(F32)[np.array([_swa_slot_head(s) for s in range(N_Q_HEADS)])]
    sink_rows = jnp.broadcast_to(jnp.repeat(slot_sinks, BLOCK)[:, None], (rows, LANES))
    smem = pl.BlockSpec(memory_space=pltpu.SMEM)
    return pl.pallas_call(
        _swa_kernel,
        grid=(batch, nb),
        in_specs=[smem, _resident((BLOCK, BLOCK)), _resident((rows, LANES)),
                  pl.BlockSpec((BLOCK, Q_W), lambda b, i: (b * nb + i, 0)),
                  pl.BlockSpec((BLOCK, 2 * KV_W), lambda b, i: (b * nb + jnp.maximum(i - 1, 0), 0)),
                  pl.BlockSpec((BLOCK, 2 * KV_W), lambda b, i: (b * nb + i, 0))],
        out_specs=pl.BlockSpec((BLOCK, Q_W), lambda b, i: (b * nb + i, 0)),
        out_shape=jax.ShapeDtypeStruct((t, Q_W), BF16),
        scratch_shapes=[pltpu.VMEM((2, rows, BLOCK), F32), pltpu.VMEM((rows, BLOCK), F32),
                        pltpu.VMEM((rows, BLOCK), BF16)],
        compiler_params=_params(2),
        name="swa",
    )(rel_bias, bucket, sink_rows, q, kv, kv)


TAIL_TILE = 512
FF_CHUNK = 256


def _cross_attention(xq_ref, mk_ref, mv_ref, cross_scr):
    for h in range(N_X_HEADS):
        sl = slice(h * X_HEAD_DIM, (h + 1) * X_HEAD_DIM)
        s = lax.dot_general(xq_ref[:, sl], mk_ref[:, sl], NT_DIMS, preferred_element_type=F32)
        s = s * (X_HEAD_DIM ** -0.5)
        p = jnp.exp(s - jnp.max(s, axis=-1, keepdims=True))
        p = p / jnp.sum(p, axis=-1, keepdims=True)
        cross_scr[:, sl] = _dot(p.astype(BF16), mv_ref[:, sl]).astype(cross_scr.dtype)


def _swiglu_residual(x1, g_ref, wg_ref, wu_ref, wd_ref, act_scr):
    h = _rms(x1, g_ref[...]).astype(BF16)
    for c0 in range(0, D_FF, FF_CHUNK):
        sl = slice(c0, c0 + FF_CHUNK)
        act_scr[:, sl] = (_silu(_dot(h, wg_ref[:, sl])) * _dot(h, wu_ref[:, sl])).astype(BF16)
    return x1 + _dot(act_scr[...], wd_ref[...])


def _tail_specs(batch, t, a_width, tile=TAIL_TILE):
    nt = t // batch // tile
    tok = lambda w: pl.BlockSpec((tile, w), lambda b, i: (b * nt + i, 0))
    mem = lambda half: pl.BlockSpec((MEM_LEN, XQ_W), lambda b, i: (b, half))
    return nt, tok, [tok(D_MODEL), tok(a_width), tok(XQ_W), mem(0), mem(1),
                     _resident((a_width, D_MODEL)), _resident((XQ_W, D_MODEL))]


def _attn_tail_kernel(x_ref, a_ref, xq_ref, mk_ref, mv_ref, wa_ref, wc_ref, g_ref, wg_ref, wu_ref, wd_ref,
                      o_ref, cross_scr, act_scr):
    _cross_attention(xq_ref, mk_ref, mv_ref, cross_scr)
    x1 = x_ref[...] + _dot(a_ref[...], wa_ref[...]) + _dot(cross_scr[...], wc_ref[...])
    o_ref[...] = _swiglu_residual(x1, g_ref, wg_ref, wu_ref, wd_ref, act_scr)


def _attn_tail(x, attn, xq, mem_kv, wa, wc, g, wg, wu, wd, batch):
    t, d = x.shape
    nt, tok, specs = _tail_specs(batch, t, attn.shape[1])
    return pl.pallas_call(
        _attn_tail_kernel,
        grid=(batch, nt),
        in_specs=specs + [_resident((1, d)), _resident(wg.shape), _resident(wu.shape), _resident(wd.shape)],
        out_specs=tok(d),
        out_shape=jax.ShapeDtypeStruct((t, d), F32),
        scratch_shapes=[pltpu.VMEM((TAIL_TILE, XQ_W), BF16), pltpu.VMEM((TAIL_TILE, D_FF), BF16)],
        compiler_params=_params(2),
        name="attn_tail",
    )(x, attn, xq, mem_kv, mem_kv, wa, wc, g.reshape(1, d), wg, wu, wd)


TOP_K = 2
HI16 = 0xFFFF0000
META_EXPERT, META_POS, META_GATE = 0, 2, 4


def _pack_bf16_pairs(h):
    n = h.shape[1] // 2
    u = pltpu.bitcast(h.astype(BF16).astype(F32), jnp.uint32)
    return (u[:, :n] >> 16) | (u[:, n:] & jnp.uint32(HI16))


def _unpack_bf16_pairs(w):
    lo = pltpu.bitcast(w << 16, F32).astype(BF16)
    hi = pltpu.bitcast(w & jnp.uint32(HI16), F32).astype(BF16)
    return jnp.concatenate([lo, hi], axis=1)


def _route(x1, g_ref, wr_ref, meta_ref, meta_t_ref, cnt_ref, hpk_ref, run_scr):
    h = _rms(x1, g_ref[...])
    hpk_ref[...] = _pack_bf16_pairs(h)
    tm = h.shape[0]
    prod = _dot(jnp.concatenate(_split_bf16(h, 2), axis=0), wr_ref[...])
    prod = prod[:tm, :] + prod[tm:, :]
    logits = prod[:, :LANES] + prod[:, LANES:]
    lt = logits.T[:N_EXPERTS, :]
    sub = lax.broadcasted_iota(jnp.int32, lt.shape, 0)
    m1 = jnp.max(lt, axis=0, keepdims=True)
    i1 = jnp.min(jnp.where(lt == m1, sub, N_EXPERTS), axis=0, keepdims=True)
    lt2 = jnp.where(sub == i1, NEG_INF, lt)
    m2 = jnp.max(lt2, axis=0, keepdims=True)
    i2 = jnp.min(jnp.where(lt2 == m2, sub, N_EXPERTS), axis=0, keepdims=True)
    e = jnp.exp(m2 - m1)
    g1, g2 = 1.0 / (1.0 + e), e / (1.0 + e)

    onehot = jnp.where(jnp.logical_or(sub == i1, sub == i2), 1.0, 0.0)
    onehot16 = jnp.concatenate([onehot, jnp.zeros_like(onehot)], axis=0).astype(BF16)
    earlier = lax.broadcasted_iota(jnp.int32, (tm, tm), 0) < lax.broadcasted_iota(jnp.int32, (tm, tm), 1)
    before = _dot(onehot16, earlier.astype(BF16))[:N_EXPERTS, :] + run_scr[:, 0:1]
    pos1 = jnp.sum(jnp.where(sub == i1, before, 0.0), axis=0, keepdims=True)
    pos2 = jnp.sum(jnp.where(sub == i2, before, 0.0), axis=0, keepdims=True)
    run_scr[...] += jnp.sum(onehot, axis=1, keepdims=True)
    cnt_ref[...] = run_scr[...]

    def put(k, v):
        return jnp.where(sub == k, v, 0.0)

    meta_t = (put(META_EXPERT, i1.astype(F32)) + put(META_EXPERT + 1, i2.astype(F32))
              + put(META_POS, pos1) + put(META_POS + 1, pos2)
              + put(META_GATE, g1) + put(META_GATE + 1, g2))
    meta_t_ref[...] = meta_t
    meta_ref[...] = jnp.concatenate([meta_t, jnp.zeros((LANES - N_EXPERTS, tm), F32)], axis=0).T


def _ssd_tail_kernel(x_ref, y_ref, xq_ref, mk_ref, mv_ref, wy_ref, wc_ref, g_ref, wr_ref,
                     x1_ref, meta_ref, meta_t_ref, cnt_ref, hpk_ref, run_scr, *cross_scrs):
    @pl.when((pl.program_id(0) == 0) & (pl.program_id(1) == 0))
    def _():
        run_scr[...] = jnp.zeros_like(run_scr)

    for s in range(x_ref.shape[0] // TAIL_TILE):
        rows = pl.ds(s * TAIL_TILE, TAIL_TILE)
        _cross_attention(xq_ref.at[rows], mk_ref, mv_ref, cross_scrs[s])
        x1_ref[rows, :] = x_ref[rows, :] + _dot(y_ref[rows, :], wy_ref[...]) + _dot(cross_scrs[s][...], wc_ref[...])
    _route(x1_ref[...], g_ref, wr_ref, meta_ref, meta_t_ref, cnt_ref, hpk_ref, run_scr)


SSD_TAIL_SUBTILES = 2


def _ssd_tail(x, y, xq, mem_kv, wy, wc, g, w_router, batch):
    t, d = x.shape
    tile = SSD_TAIL_SUBTILES * TAIL_TILE
    nt, tok, specs = _tail_specs(batch, t, y.shape[1], tile)
    w1, w2 = _split_bf16(w_router.astype(F32), 2)
    wr = jnp.zeros((d, 2 * LANES), BF16).at[:, :N_EXPERTS].set(w1).at[:, LANES:LANES + N_EXPERTS].set(w2)
    return pl.pallas_call(
        _ssd_tail_kernel,
        grid=(batch, nt),
        in_specs=specs + [_resident((1, d)), _resident((d, 2 * LANES))],
        out_specs=[tok(d), tok(LANES), pl.BlockSpec((N_EXPERTS, tile), lambda b, i: (0, b * nt + i)),
                   pl.BlockSpec((N_EXPERTS, LANES), lambda b, i: (0, 0)), tok(d // 2)],
        out_shape=[jax.ShapeDtypeStruct((t, d), F32), jax.ShapeDtypeStruct((t, LANES), F32),
                   jax.ShapeDtypeStruct((N_EXPERTS, t), F32), jax.ShapeDtypeStruct((N_EXPERTS, LANES), F32),
                   jax.ShapeDtypeStruct((t, d // 2), jnp.uint32)],
        scratch_shapes=[pltpu.VMEM((N_EXPERTS, LANES), F32)]
        + [pltpu.VMEM((TAIL_TILE, XQ_W), BF16)] * SSD_TAIL_SUBTILES,
        compiler_params=_params(2),
        name="ssd_tail",
    )(x, y, xq, mem_kv, mem_kv, wy, wc, g.reshape(1, d), wr)


ROW_TILE = 512
DISPATCH_TILE = 2048
COMBINE_TILE = 1024


def _row_copy_all_wait(src_like, dst_like, sem):
    pltpu.make_async_copy(src_like, dst_like, sem).wait()


def _dispatch_kernel(pad_start_ref, dest_hbm, hpk_ref, xs_out, idx_smem, zero_buf, idx_sem, row_sem, fill_sem):
    tm = hpk_ref.shape[0]

    @pl.when(pl.program_id(0) == 0)
    def _():
        zero_buf[...] = jnp.zeros_like(zero_buf)
        fill_rows = zero_buf.shape[0]
        starts = [pad_start_ref[e] // SUBLANES * SUBLANES for e in range(N_EXPERTS)]
        last = xs_out.shape[0] - fill_rows
        n_tail = -(-(N_EXPERTS + 1) * ROW_TILE // fill_rows)
        starts += [jnp.minimum(pad_start_ref[N_EXPERTS] + k * fill_rows, last) for k in range(n_tail)]
        for s in starts:
            fill = pltpu.make_async_copy(zero_buf, xs_out.at[pl.ds(pl.multiple_of(s, SUBLANES), fill_rows)], fill_sem)
            fill.start()
            fill.wait()

    idx_copy = pltpu.make_async_copy(dest_hbm.at[pl.program_id(0)], idx_smem, idx_sem)
    idx_copy.start()
    idx_copy.wait()

    def issue(t, carry):
        for k in range(TOP_K):
            row = idx_smem[k * tm + t]
            pltpu.make_async_copy(hpk_ref.at[pl.ds(t, 1)], xs_out.at[pl.ds(row, 1)], row_sem).start()
        return carry

    lax.fori_loop(0, tm, issue, 0, unroll=8)
    for _ in range(TOP_K):
        _row_copy_all_wait(hpk_ref, xs_out.at[pl.ds(0, tm)], row_sem)


def _dispatch(pad_start, dest_tiles, hpk, n_rows):
    t, w = hpk.shape
    tm = dest_tiles.shape[1] // TOP_K
    any_spec = pl.BlockSpec(memory_space=pl.ANY)
    return pl.pallas_call(
        _dispatch_kernel,
        grid_spec=pltpu.PrefetchScalarGridSpec(
            num_scalar_prefetch=1,
            grid=(t // tm,),
            in_specs=[any_spec, pl.BlockSpec((tm, w), lambda i, ps: (i, 0))],
            out_specs=any_spec,
            scratch_shapes=[pltpu.SMEM((TOP_K * tm,), jnp.int32), pltpu.VMEM((ROW_TILE + SUBLANES, w), jnp.uint32),
                            pltpu.SemaphoreType.DMA, pltpu.SemaphoreType.DMA, pltpu.SemaphoreType.DMA]),
        out_shape=jax.ShapeDtypeStruct((n_rows, w), jnp.uint32),
        compiler_params=_params(1),
        name="dispatch",
    )(pad_start, dest_tiles, hpk)


def _experts_kernel(tile_expert_ref, n_used_ref, xs_ref, wg_ref, wu_ref, wd_ref, y_ref, act_scr):
    del tile_expert_ref

    @pl.when(pl.program_id(0) < n_used_ref[0])
    def _():
        h = _unpack_bf16_pairs(xs_ref[...])
        for c0 in range(0, D_FF, FF_CHUNK):
            sl = slice(c0, c0 + FF_CHUNK)
            act_scr[:, sl] = (_silu(_dot(h, wg_ref[:, sl])) * _dot(h, wu_ref[:, sl])).astype(BF16)
        y_ref[...] = _dot(act_scr[...], wd_ref[...])

    @pl.when(pl.program_id(0) >= n_used_ref[0])
    def _():
        y_ref[...] = jnp.zeros_like(y_ref)


def _experts(tile_expert, n_used, xs, wg, wu, wd, layer):
    n_rows, w = xs.shape
    d = 2 * w
    tm = ROW_TILE
    wspec = lambda shape: pl.BlockSpec((None, None) + shape, lambda i, te, nu: (layer, te[i], 0, 0))
    return pl.pallas_call(
        _experts_kernel,
        grid_spec=pltpu.PrefetchScalarGridSpec(
            num_scalar_prefetch=2,
            grid=(n_rows // tm,),
            in_specs=[pl.BlockSpec((tm, w), lambda i, te, nu: (jnp.minimum(i, nu[0] - 1), 0)),
                      wspec((d, D_FF)), wspec((d, D_FF)), wspec((D_FF, d))],
            out_specs=pl.BlockSpec((tm, d), lambda i, te, nu: (i, 0)),
            scratch_shapes=[pltpu.VMEM((tm, D_FF), BF16)]),
        out_shape=jax.ShapeDtypeStruct((n_rows, d), F32),
        compiler_params=_params(1),
        name="experts",
    )(tile_expert, n_used, xs, wg, wu, wd)


def _combine_kernel(dest_hbm, x_ref, meta_ref, y_hbm, gf_ref, o_ref, idx_smem, y_buf, idx_sem, row_sem, *,
                    final_norm):
    tm = x_ref.shape[0]
    idx_copy = pltpu.make_async_copy(dest_hbm.at[pl.program_id(0)], idx_smem, idx_sem)
    idx_copy.start()
    idx_copy.wait()

    def issue(t, carry):
        for k in range(TOP_K):
            row = idx_smem[k * tm + t]
            pltpu.make_async_copy(y_hbm.at[pl.ds(row, 1)], y_buf.at[k, pl.ds(t, 1)], row_sem).start()
        return carry

    lax.fori_loop(0, tm, issue, 0, unroll=8)
    for k in range(TOP_K):
        _row_copy_all_wait(y_hbm.at[pl.ds(0, tm)], y_buf.at[k], row_sem)
    meta = meta_ref[...]
    out = x_ref[...]
    for k in range(TOP_K):
        out = out + meta[:, META_GATE + k:META_GATE + k + 1] * y_buf[k]
    o_ref[...] = _rms(out, gf_ref[...]) if final_norm else out


def _combine(dest_tiles, x, meta, y, g_final):
    t, d = x.shape
    tm = dest_tiles.shape[1] // TOP_K
    any_spec = pl.BlockSpec(memory_space=pl.ANY)
    final_norm = g_final is not None
    gf = (g_final if final_norm else jnp.ones((d,), F32)).reshape(1, d)
    return pl.pallas_call(
        functools.partial(_combine_kernel, final_norm=final_norm),
        grid=(t // tm,),
        in_specs=[any_spec, pl.BlockSpec((tm, d), lambda i: (i, 0)),
                  pl.BlockSpec((tm, LANES), lambda i: (i, 0)), any_spec, _resident((1, d))],
        out_specs=pl.BlockSpec((tm, d), lambda i: (i, 0)),
        out_shape=jax.ShapeDtypeStruct((t, d), F32),
        scratch_shapes=[pltpu.SMEM((TOP_K * tm,), jnp.int32), pltpu.VMEM((TOP_K, tm, d), F32),
                        pltpu.SemaphoreType.DMA, pltpu.SemaphoreType.DMA],
        compiler_params=_params(1),
        name="combine",
    )(dest_tiles, x, meta, y, gf)


def _moe(x, meta, meta_t, counts, hpk, wg, wu, wd, layer, g_final=None):
    t, d = x.shape
    cnt = counts[:, 0].astype(jnp.int32)
    padded = (cnt + ROW_TILE - 1) // ROW_TILE * ROW_TILE
    ends = jnp.cumsum(padded)
    starts = ends - padded
    expert = meta_t[META_EXPERT:META_EXPERT + TOP_K].astype(jnp.int32)
    dest = meta_t[META_POS:META_POS + TOP_K].astype(jnp.int32)
    for e in range(N_EXPERTS):
        dest = dest + jnp.where(expert == e, starts[e], 0)

    def dest_tiles(tile):
        return dest.reshape(TOP_K, t // tile, tile).transpose(1, 0, 2).reshape(-1, TOP_K * tile)

    n_rows = TOP_K * t + (N_EXPERTS + 1) * ROW_TILE
    n_tiles = n_rows // ROW_TILE
    n_used = ends[-1] // ROW_TILE
    tile_start = jnp.minimum(jnp.arange(n_tiles), n_used - 1) * ROW_TILE
    tile_expert = jnp.sum(tile_start[:, None] >= ends[None, :], axis=1).astype(jnp.int32)
    pad_start = jnp.concatenate([starts + cnt, ends[-1:]])
    xs = _dispatch(pad_start, dest_tiles(DISPATCH_TILE), hpk, n_rows)
    y = _experts(tile_expert, n_used.reshape(1).astype(jnp.int32), xs, wg, wu, wd, layer)
    return _combine(dest_tiles(COMBINE_TILE), x, meta, y, g_final)


CONV_TAIL = 16


def _conv_shift_matrix():
    m = np.zeros(((CONV_WIDTH - 1) * CHUNK, CONV_TAIL + CHUNK), np.float32)
    for k in range(CONV_WIDTH - 1):
        for t in range(CHUNK):
            m[k * CHUNK + t, CONV_TAIL + t - (CONV_WIDTH - 1) + k] = 1.0
    return m


def _split_bf16(v, parts):
    out, r = [], v
    for _ in range(parts):
        p = r.astype(BF16)
        out.append(p)
        r = r - p.astype(F32)
    return out


def _ssd_kernel(xbc_ref, z_ref, dt_ref, cw_ref, cb_ref, dtb_ref, alog_ref, dsk_ref, gn_ref, exp_ref, shift_ref,
                o_ref, state_scr, *chunk_scr):
    n_sub = xbc_ref.shape[0] // CHUNK
    ext_scrs, y_scrs = chunk_scr[:n_sub], chunk_scr[n_sub:]

    @pl.when(pl.program_id(1) == 0)
    def _():
        state_scr[...] = jnp.zeros_like(state_scr)
        ext_scrs[n_sub - 1][CHUNK:, :] = jnp.zeros((CONV_TAIL, CONV_CH), BF16)

    for c in range(n_sub):
        rows = pl.ds(c * CHUNK, CHUNK)
        _ssd_chunk(xbc_ref.at[rows], z_ref.at[rows], dt_ref.at[rows], cw_ref, cb_ref, dtb_ref, alog_ref, dsk_ref,
                   gn_ref, exp_ref, shift_ref, o_ref.at[rows], state_scr, ext_scrs[c],
                   ext_scrs[(c - 1) % n_sub], y_scrs[c])


def _ssd_chunk(xbc_ref, z_ref, dt_ref, cw_ref, cb_ref, dtb_ref, alog_ref, dsk_ref, gn_ref, exp_ref, shift_ref,
               o_ref, state_scr, ext_scr, prev_ext_scr, y_scr):
    ext_scr[0:CONV_TAIL, :] = prev_ext_scr[CHUNK:, :]
    ext_scr[CONV_TAIL:, :] = xbc_ref[...]

    def conv_silu(c0, c1):
        shifted = _dot(shift_ref[...], ext_scr[:, c0:c1])
        acc = cb_ref[:, c0:c1] + cw_ref[CONV_WIDTH - 1:CONV_WIDTH, c0:c1] * ext_scr[CONV_TAIL:, c0:c1].astype(F32)
        for k in range(CONV_WIDTH - 1):
            acc = acc + cw_ref[k:k + 1, c0:c1] * shifted[k * CHUNK:(k + 1) * CHUNK, :]
        return _silu(acc)

    dt = dt_ref[...] + dtb_ref[...]
    dt = jnp.maximum(dt, 0.0) + jnp.log(1.0 + jnp.exp(-jnp.abs(dt)))
    da = dt * -jnp.exp(alog_ref[...])
    row = lax.broadcasted_iota(jnp.int32, (CHUNK, CHUNK), 0)
    col = lax.broadcasted_iota(jnp.int32, (CHUNK, CHUNK), 1)
    causal = col <= row
    cs = _dot(causal.astype(BF16), jnp.concatenate(_split_bf16(da, 3), axis=1))
    acum = cs[:, 0:LANES] + cs[:, LANES:2 * LANES] + cs[:, 2 * LANES:]
    acum_t = acum.T
    ea = jnp.exp(acum)
    dtte = dt * jnp.exp(acum[CHUNK - 1:CHUNK, :] - acum)

    per_head = jnp.concatenate(
        [jnp.concatenate(_split_bf16(v, 2), axis=1) for v in (ea, dt, dtte)], axis=0)

    low = lax.broadcasted_iota(jnp.int32, (CHUNK, LANES), 1) < SSM_HEAD_DIM
    for g in range(N_SSM_GROUPS):
        gsl = slice(g * GROUP_W, (g + 1) * GROUP_W)
        xg = conv_silu(g * GROUP_W, (g + 1) * GROUP_W)
        bg = conv_silu(D_INNER + g * D_STATE, D_INNER + (g + 1) * D_STATE)
        cg = conv_silu(D_INNER + BC_W + g * D_STATE, D_INNER + BC_W + (g + 1) * D_STATE).astype(BF16)
        expanded = _dot(per_head, exp_ref[:, gsl])
        ea_x, dt_x, dtte_x = (expanded[k * CHUNK:(k + 1) * CHUNK, :] for k in range(3))
        xdt = (xg * dt_x).astype(BF16)
        cb = lax.dot_general(cg, bg.astype(BF16), NT_DIMS, preferred_element_type=F32)
        state = state_scr[g]
        y = _dot(cg, state.astype(BF16)) * ea_x + xg * dsk_ref[:, gsl]
        diag = []
        for pair in range(HEADS_PER_GROUP // 2):
            xp = xdt[:, pair * LANES:(pair + 1) * LANES]
            acc = jnp.zeros((CHUNK, LANES), F32)
            for half in range(2):
                h = g * HEADS_PER_GROUP + 2 * pair + half
                seg = acum[:, h:h + 1] - acum_t[h:h + 1, :]
                m = (cb * jnp.exp(jnp.where(causal, seg, NEG_INF))).astype(BF16)
                xm = jnp.where(low if half == 0 else jnp.logical_not(low), xp, jnp.zeros_like(xp))
                acc = acc + _dot(m, xm)
            diag.append(acc)
        y_scr[:, gsl] = y + jnp.concatenate(diag, axis=1)
        w = (xg * dtte_x).astype(BF16)
        state_scr[g] = state * ea_x[CHUNK - 1:CHUNK, :] + _dot(bg.T.astype(BF16), w)

    z = z_ref[...].astype(F32)
    o_ref[...] = _rms(y_scr[...] * _silu(z), gn_ref[...]).astype(o_ref.dtype)


SSD_SUBCHUNKS = 4


def _ssd(xbc, z, dt_raw, conv_w, conv_b, dt_bias, a_log, d_skip, g_norm, batch):
    t = xbc.shape[0]
    step = SSD_SUBCHUNKS * CHUNK
    nc = t // batch // step

    def lane_pad(v):
        return jnp.zeros((1, LANES), F32).at[0, :N_SSM_HEADS].set(v)

    expand = np.zeros((LANES, D_INNER), np.float32)
    for h in range(N_SSM_HEADS):
        expand[h, h * SSM_HEAD_DIM:(h + 1) * SSM_HEAD_DIM] = 1.0
    expand2 = jnp.asarray(np.concatenate([expand, expand], axis=0), BF16)
    tok = lambda b, c: (b * nc + c, 0)
    return pl.pallas_call(
        _ssd_kernel,
        grid=(batch, nc),
        in_specs=[pl.BlockSpec((step, CONV_CH), tok), pl.BlockSpec((step, D_INNER), tok),
                  pl.BlockSpec((step, LANES), tok),
                  _resident((CONV_WIDTH, CONV_CH)), _resident((1, CONV_CH)),
                  _resident((1, LANES)), _resident((1, LANES)),
                  _resident((1, D_INNER)), _resident((1, D_INNER)), _resident((2 * LANES, D_INNER)),
                  _resident(((CONV_WIDTH - 1) * CHUNK, CONV_TAIL + CHUNK))],
        out_specs=pl.BlockSpec((step, D_INNER), tok),
        out_shape=jax.ShapeDtypeStruct((t, D_INNER), BF16),
        scratch_shapes=[pltpu.VMEM((N_SSM_GROUPS, D_STATE, GROUP_W), F32)]
        + [pltpu.VMEM((CONV_TAIL + CHUNK, CONV_CH), BF16)] * SSD_SUBCHUNKS
        + [pltpu.VMEM((CHUNK, D_INNER), F32)] * SSD_SUBCHUNKS,
        compiler_params=_params(2),
        name="ssd",
    )(xbc, z, dt_raw, conv_w, conv_b.reshape(1, CONV_CH), lane_pad(dt_bias), lane_pad(a_log),
      jnp.repeat(d_skip, SSM_HEAD_DIM).reshape(1, D_INNER), g_norm.reshape(1, D_INNER), expand2,
      jnp.asarray(_conv_shift_matrix(), BF16))


def kernel(x, mem, g_mix, g_ffn, g_mem, w_mem_kv, rel_bias, swa_w_in, swa_sinks, swa_w_out, ssm_w_in, ssm_conv_w, ssm_conv_b, ssm_dt_bias, ssm_A_log, ssm_D, ssm_g_norm, ssm_w_out, ffn_w_gate, ffn_w_up, ffn_w_down, moe_w_router, moe_w_gate, moe_w_up, moe_w_down, g_final):
    batch, seq, d = x.shape
    xf = x.reshape(batch * seq, d)
    memf = mem.reshape(batch * MEM_LEN, d)
    moe_wg, moe_wu, moe_wd = moe_w_gate.astype(BF16), moe_w_up.astype(BF16), moe_w_down.astype(BF16)
    for i in range(DEPTH):
        j = i // 2
        (mem_kv,) = _norm_proj(memf, g_mem[i], [w_mem_kv[i].astype(BF16)], [BF16])
        if i % 2 == 0:
            w_in = swa_w_in[j]
            w_q = (w_in[:, :Q_W] * HEAD_DIM ** -0.5).astype(BF16)
            q, kv, xq = _norm_proj(
                xf, g_mix[i], [w_q, w_in[:, Q_W:Q_W + 2 * KV_W].astype(BF16), w_in[:, Q_W + 2 * KV_W:].astype(BF16)],
                [BF16, BF16, BF16])
            attn = _swa(q, kv, rel_bias, swa_sinks[j], batch)
            w_out = swa_w_out[j].astype(BF16)
            xf = _attn_tail(xf, attn, xq, mem_kv, w_out[:Q_W], w_out[Q_W:], g_ffn[i],
                            ffn_w_gate[j].astype(BF16), ffn_w_up[j].astype(BF16), ffn_w_down[j].astype(BF16),
                            batch)
        else:
            w_in = ssm_w_in[j].astype(BF16)
            o_dt = D_INNER + CONV_CH
            w_dt = jnp.zeros((d, LANES), BF16).at[:, :N_SSM_HEADS].set(w_in[:, o_dt:o_dt + N_SSM_HEADS])
            z, xbc, dt_raw, xq = _norm_proj(
                xf, g_mix[i], [w_in[:, :D_INNER], w_in[:, D_INNER:o_dt], w_dt, w_in[:, o_dt + N_SSM_HEADS:]],
                [BF16, BF16, F32, BF16])
            y = _ssd(xbc, z, dt_raw, ssm_conv_w[j], ssm_conv_b[j], ssm_dt_bias[j], ssm_A_log[j],
                     ssm_D[j], ssm_g_norm[j], batch)
            w_out = ssm_w_out[j].astype(BF16)
            xf, meta, meta_t, counts, hpk = _ssd_tail(xf, y, xq, mem_kv, w_out[:D_INNER], w_out[D_INNER:],
                                                      g_ffn[i], moe_w_router[j], batch)
            xf = _moe(xf, meta, meta_t, counts, hpk, moe_wg, moe_wu, moe_wd, j,
                      g_final if i == DEPTH - 1 else None)
    assert DEPTH % 2 == 0
    return xf.reshape(batch, seq, d)
```

```python
import functools
import math

import numpy as np
import jax
import jax.numpy as jnp
from jax import lax
from jax.experimental import pallas as pl
from jax.experimental.pallas import tpu as pltpu

F32 = jnp.float32
BF16 = jnp.bfloat16

D_MODEL = 1024
DEPTH = 4
MEM_LEN = 256
EPS = 1e-6
N_Q_HEADS = 16
N_KV_HEADS = 2
HEAD_DIM = 64
BLOCK = 128
N_BUCKETS = 32
MAX_DISTANCE = 128
N_X_HEADS = 4
X_HEAD_DIM = 256
D_INNER = 2048
SSM_HEAD_DIM = 64
N_SSM_HEADS = 32
N_SSM_GROUPS = 4
HEADS_PER_GROUP = 8
D_STATE = 128
CONV_WIDTH = 4
CHUNK = 128
D_FF = 2816
N_EXPERTS = 8
Q_W = N_Q_HEADS * HEAD_DIM
KV_W = N_KV_HEADS * HEAD_DIM
XQ_W = N_X_HEADS * X_HEAD_DIM
BC_W = N_SSM_GROUPS * D_STATE
CONV_CH = D_INNER + 2 * BC_W
GROUP_W = HEADS_PER_GROUP * SSM_HEAD_DIM

LANES = 128
SUBLANES = 8
VMEM_LIMIT = 56 << 20
NEG_INF = float("-inf")
NT_DIMS = (((1,), (1,)), ((), ()))


def _params(n_axes, vmem=VMEM_LIMIT):
    return pltpu.CompilerParams(dimension_semantics=("arbitrary",) * n_axes, vmem_limit_bytes=vmem)


def _resident(shape):
    nd = len(shape)
    return pl.BlockSpec(shape, lambda *_: (0,) * nd, pipeline_mode=pl.Buffered(1))


def _dot(a, b):
    return jnp.dot(a, b, preferred_element_type=F32)


def _rms(x, g):
    return x * lax.rsqrt(jnp.mean(x * x, axis=-1, keepdims=True) + EPS) * g


def _silu(v):
    return v / (1.0 + jnp.exp(-v))


def _norm_proj_kernel(x_ref, g_ref, *refs, n_out, col_chunk):
    w_refs, o_refs = refs[:n_out], refs[n_out:]
    h = _rms(x_ref[...], g_ref[...]).astype(BF16)
    for w_ref, o_ref in zip(w_refs, o_refs):
        n = w_ref.shape[1]
        for c0 in range(0, n, col_chunk):
            c1 = min(c0 + col_chunk, n)
            o_ref[:, c0:c1] = _dot(h, w_ref[:, c0:c1]).astype(o_ref.dtype)


def _norm_proj(x, g, ws, out_dtypes, tm=512):
    t, d = x.shape
    n_out = len(ws)
    return pl.pallas_call(
        functools.partial(_norm_proj_kernel, n_out=n_out, col_chunk=512),
        grid=(t // tm,),
        in_specs=[pl.BlockSpec((tm, d), lambda i: (i, 0)), _resident((1, d))]
        + [_resident(w.shape) for w in ws],
        out_specs=[pl.BlockSpec((tm, w.shape[1]), lambda i: (i, 0)) for w in ws],
        out_shape=[jax.ShapeDtypeStruct((t, w.shape[1]), dt) for w, dt in zip(ws, out_dtypes)],
        compiler_params=_params(1),
        name="norm_proj",
    )(x, g.reshape(1, d), *ws)


def _bucket_table():
    qi = np.arange(BLOCK)[:, None]
    kj = np.arange(2 * BLOCK)[None, :]
    dist = BLOCK + qi - kj
    max_exact = N_BUCKETS // 2
    d = np.maximum(dist, 0)
    df = np.maximum(d, 1).astype(np.float32)
    far = max_exact + (
        np.log(df / np.float32(max_exact)) / np.float32(math.log(MAX_DISTANCE / max_exact))
        * np.float32(N_BUCKETS - max_exact)
    ).astype(np.int32)
    bucket = np.where(d < max_exact, d, np.minimum(far, N_BUCKETS - 1))
    r = np.arange(BLOCK)[:, None]
    c = np.arange(BLOCK)[None, :]
    return np.where(c <= r, bucket[:, BLOCK:], bucket[:, :BLOCK]).astype(np.int32)


PAIRS_PER_KV = N_Q_HEADS // N_KV_HEADS // 2
SLOT_ROWS = PAIRS_PER_KV * BLOCK


def _swa_slot_head(slot):
    kvh, rest = divmod(slot, 2 * PAIRS_PER_KV)
    half, pair = divmod(rest, PAIRS_PER_KV)
    return 2 * (kvh * PAIRS_PER_KV + pair) + half


def _swa_kernel(relb_ref, bucket_ref, sink_ref, q_ref, kvp_ref, kvc_ref, o_ref, bias_scr, s_scr, p_scr):
    first = (pl.program_id(0) == 0) & (pl.program_id(1) == 0)

    def from_current(n_rows):
        r = lax.broadcasted_iota(jnp.int32, (n_rows, BLOCK), 0) & (BLOCK - 1)
        return lax.broadcasted_iota(jnp.int32, (n_rows, BLOCK), 1) <= r

    @pl.when(first)
    def _():
        bucket = bucket_ref[...]
        for slot in range(N_Q_HEADS):
            h = _swa_slot_head(slot)
            acc = jnp.zeros((BLOCK, BLOCK), F32)
            for n in range(N_BUCKETS):
                acc = jnp.where(bucket == n, relb_ref[n, h], acc)
            rows = slice(slot * BLOCK, (slot + 1) * BLOCK)
            bias_scr[1, rows, :] = acc
            bias_scr[0, rows, :] = jnp.where(from_current(BLOCK), acc, NEG_INF)

    kk = jnp.concatenate([kvp_ref[:, 0:KV_W], kvc_ref[:, 0:KV_W]], axis=0).astype(F32)
    vv = jnp.concatenate([kvp_ref[:, KV_W:], kvc_ref[:, KV_W:]], axis=0).astype(F32)
    low = lax.broadcasted_iota(jnp.int32, kk.shape, 1) < HEAD_DIM

    def placed(t):
        r = pltpu.roll(t, HEAD_DIM, 1)
        return [[jnp.where(low, t, 0.0).astype(BF16), jnp.where(low, 0.0, r).astype(BF16)],
                [jnp.where(low, r, 0.0).astype(BF16), jnp.where(low, 0.0, t).astype(BF16)]]

    k_var, v_var = placed(kk), placed(vv)

    cur_slot = from_current(SLOT_ROWS)
    for v in range(2 * N_KV_HEADS):
        kvh, half = divmod(v, 2)
        q_stack = jnp.concatenate(
            [q_ref[:, (kvh * PAIRS_PER_KV + j) * LANES:(kvh * PAIRS_PER_KV + j + 1) * LANES]
             for j in range(PAIRS_PER_KV)], axis=0)
        s = lax.dot_general(q_stack, k_var[kvh][half], NT_DIMS, preferred_element_type=F32)
        s_scr[v * SLOT_ROWS:(v + 1) * SLOT_ROWS, :] = jnp.where(cur_slot, s[:, BLOCK:], s[:, :BLOCK])

    s = s_scr[...] + bias_scr[jnp.minimum(pl.program_id(1), 1)]
    sink = sink_ref[...]
    m = jnp.maximum(jnp.max(s, axis=-1, keepdims=True), sink)
    p = jnp.exp(s - m)
    p = p / (jnp.sum(p, axis=-1, keepdims=True) + jnp.exp(sink - m))
    p_scr[...] = p.astype(BF16)

    outs = []
    for v in range(2 * N_KV_HEADS):
        kvh, half = divmod(v, 2)
        p = p_scr[v * SLOT_ROWS:(v + 1) * SLOT_ROWS, :]
        zero = jnp.zeros_like(p)
        p_band = jnp.concatenate([jnp.where(cur_slot, zero, p), jnp.where(cur_slot, p, zero)], axis=1)
        outs.append(_dot(p_band, v_var[kvh][half]))
    for kvh in range(N_KV_HEADS):
        for j in range(PAIRS_PER_KV):
            pair = kvh * PAIRS_PER_KV + j
            out = outs[2 * kvh][j * BLOCK:(j + 1) * BLOCK, :] + outs[2 * kvh + 1][j * BLOCK:(j + 1) * BLOCK, :]
            o_ref[:, pair * LANES:(pair + 1) * LANES] = out.astype(o_ref.dtype)


def _swa(q, kv, rel_bias, sinks, batch):
    t = q.shape[0]
    nb = t // batch // BLOCK
    rows = N_Q_HEADS * BLOCK
    bucket = jnp.asarray(_bucket_table())
    slot_sinks = sinks.astype(F32)[np.array([_swa_slot_head(s) for s in range(N_Q_HEADS)])]
    sink_rows = jnp.broadcast_to(jnp.repeat(slot_sinks, BLOCK)[:, None], (rows, LANES))
    smem = pl.BlockSpec(memory_space=pltpu.SMEM)
    return pl.pallas_call(
        _swa_kernel,
        grid=(batch, nb),
        in_specs=[smem, _resident((BLOCK, BLOCK)), _resident((rows, LANES)),
                  pl.BlockSpec((BLOCK, Q_W), lambda b, i: (b * nb + i, 0)),
                  pl.BlockSpec((BLOCK, 2 * KV_W), lambda b, i: (b * nb + jnp.maximum(i - 1, 0), 0)),
                  pl.BlockSpec((BLOCK, 2 * KV_W), lambda b, i: (b * nb + i, 0))],
        out_specs=pl.BlockSpec((BLOCK, Q_W), lambda b, i: (b * nb + i, 0)),
        out_shape=jax.ShapeDtypeStruct((t, Q_W), BF16),
        scratch_shapes=[pltpu.VMEM((2, rows, BLOCK), F32), pltpu.VMEM((rows, BLOCK), F32),
                        pltpu.VMEM((rows, BLOCK), BF16)],
        compiler_params=_params(2),
        name="swa",
    )(rel_bias, bucket, sink_rows, q, kv, kv)


TAIL_TILE = 512
FF_CHUNK = 256


def _cross_attention(xq_ref, mk_ref, mv_ref, cross_scr):
    for h in range(N_X_HEADS):
        sl = slice(h * X_HEAD_DIM, (h + 1) * X_HEAD_DIM)
        s = lax.dot_general(xq_ref[:, sl], mk_ref[:, sl], NT_DIMS, preferred_element_type=F32)
        s = s * (X_HEAD_DIM ** -0.5)
        p = jnp.exp(s - jnp.max(s, axis=-1, keepdims=True))
        p = p / jnp.sum(p, axis=-1, keepdims=True)
        cross_scr[:, sl] = _dot(p.astype(BF16), mv_ref[:, sl]).astype(cross_scr.dtype)


def _swiglu_residual(x1, g_ref, wg_ref, wu_ref, wd_ref, act_scr):
    h = _rms(x1, g_ref[...]).astype(BF16)
    for c0 in range(0, D_FF, FF_CHUNK):
        sl = slice(c0, c0 + FF_CHUNK)
        act_scr[:, sl] = (_silu(_dot(h, wg_ref[:, sl])) * _dot(h, wu_ref[:, sl])).astype(BF16)
    return x1 + _dot(act_scr[...], wd_ref[...])


def _tail_specs(batch, t, a_width, tile=TAIL_TILE):
    nt = t // batch // tile
    tok = lambda w: pl.BlockSpec((tile, w), lambda b, i: (b * nt + i, 0))
    mem = lambda half: pl.BlockSpec((MEM_LEN, XQ_W), lambda b, i: (b, half))
    return nt, tok, [tok(D_MODEL), tok(a_width), tok(XQ_W), mem(0), mem(1),
                     _resident((a_width, D_MODEL)), _resident((XQ_W, D_MODEL))]


def _attn_tail_kernel(x_ref, a_ref, xq_ref, mk_ref, mv_ref, wa_ref, wc_ref, g_ref, wg_ref, wu_ref, wd_ref,
                      o_ref, cross_scr, act_scr):
    _cross_attention(xq_ref, mk_ref, mv_ref, cross_scr)
    x1 = x_ref[...] + _dot(a_ref[...], wa_ref[...]) + _dot(cross_scr[...], wc_ref[...])
    o_ref[...] = _swiglu_residual(x1, g_ref, wg_ref, wu_ref, wd_ref, act_scr)


def _attn_tail(x, attn, xq, mem_kv, wa, wc, g, wg, wu, wd, batch):
    t, d = x.shape
    nt, tok, specs = _tail_specs(batch, t, attn.shape[1])
    return pl.pallas_call(
        _attn_tail_kernel,
        grid=(batch, nt),
        in_specs=specs + [_resident((1, d)), _resident(wg.shape), _resident(wu.shape), _resident(wd.shape)],
        out_specs=tok(d),
        out_shape=jax.ShapeDtypeStruct((t, d), F32),
        scratch_shapes=[pltpu.VMEM((TAIL_TILE, XQ_W), BF16), pltpu.VMEM((TAIL_TILE, D_FF), BF16)],
        compiler_params=_params(2),
        name="attn_tail",
    )(x, attn, xq, mem_kv, mem_kv, wa, wc, g.reshape(1, d), wg, wu, wd)


TOP_K = 2
HI16 = 0xFFFF0000
META_EXPERT, META_POS, META_GATE = 0, 2, 4


def _pack_bf16_pairs(h):
    n = h.shape[1] // 2
    u = pltpu.bitcast(h.astype(BF16).astype(F32), jnp.uint32)
    return (u[:, :n] >> 16) | (u[:, n:] & jnp.uint32(HI16))


def _unpack_bf16_pairs(w):
    lo = pltpu.bitcast(w << 16, F32).astype(BF16)
    hi = pltpu.bitcast(w & jnp.uint32(HI16), F32).astype(BF16)
    return jnp.concatenate([lo, hi], axis=1)


def _route(x1, g_ref, wr_ref, meta_ref, meta_t_ref, cnt_ref, hpk_ref, run_scr):
    h = _rms(x1, g_ref[...])
    hpk_ref[...] = _pack_bf16_pairs(h)
    tm = h.shape[0]
    prod = _dot(jnp.concatenate(_split_bf16(h, 2), axis=0), wr_ref[...])
    prod = prod[:tm, :] + prod[tm:, :]
    logits = prod[:, :LANES] + prod[:, LANES:]
    lt = logits.T[:N_EXPERTS, :]
    sub = lax.broadcasted_iota(jnp.int32, lt.shape, 0)
    m1 = jnp.max(lt, axis=0, keepdims=True)
    i1 = jnp.min(jnp.where(lt == m1, sub, N_EXPERTS), axis=0, keepdims=True)
    lt2 = jnp.where(sub == i1, NEG_INF, lt)
    m2 = jnp.max(lt2, axis=0, keepdims=True)
    i2 = jnp.min(jnp.where(lt2 == m2, sub, N_EXPERTS), axis=0, keepdims=True)
    e = jnp.exp(m2 - m1)
    g1, g2 = 1.0 / (1.0 + e), e / (1.0 + e)

    onehot = jnp.where(jnp.logical_or(sub == i1, sub == i2), 1.0, 0.0)
    onehot16 = jnp.concatenate([onehot, jnp.zeros_like(onehot)], axis=0).astype(BF16)
    earlier = lax.broadcasted_iota(jnp.int32, (tm, tm), 0) < lax.broadcasted_iota(jnp.int32, (tm, tm), 1)
    before = _dot(onehot16, earlier.astype(BF16))[:N_EXPERTS, :] + run_scr[:, 0:1]
    pos1 = jnp.sum(jnp.where(sub == i1, before, 0.0), axis=0, keepdims=True)
    pos2 = jnp.sum(jnp.where(sub == i2, before, 0.0), axis=0, keepdims=True)
    run_scr[...] += jnp.sum(onehot, axis=1, keepdims=True)
    cnt_ref[...] = run_scr[...]

    def put(k, v):
        return jnp.where(sub == k, v, 0.0)

    meta_t = (put(META_EXPERT, i1.astype(F32)) + put(META_EXPERT + 1, i2.astype(F32))
              + put(META_POS, pos1) + put(META_POS + 1, pos2)
              + put(META_GATE, g1) + put(META_GATE + 1, g2))
    meta_t_ref[...] = meta_t
    meta_ref[...] = jnp.concatenate([meta_t, jnp.zeros((LANES - N_EXPERTS, tm), F32)], axis=0).T


def _ssd_tail_kernel(x_ref, y_ref, xq_ref, mk_ref, mv_ref, wy_ref, wc_ref, g_ref, wr_ref,
                     x1_ref, meta_ref, meta_t_ref, cnt_ref, hpk_ref, run_scr, *cross_scrs):
    @pl.when((pl.program_id(0) == 0) & (pl.program_id(1) == 0))
    def _():
        run_scr[...] = jnp.zeros_like(run_scr)

    for s in range(x_ref.shape[0] // TAIL_TILE):
        rows = pl.ds(s * TAIL_TILE, TAIL_TILE)
        _cross_attention(xq_ref.at[rows], mk_ref, mv_ref, cross_scrs[s])
        x1_ref[rows, :] = x_ref[rows, :] + _dot(y_ref[rows, :], wy_ref[...]) + _dot(cross_scrs[s][...], wc_ref[...])
    _route(x1_ref[...], g_ref, wr_ref, meta_ref, meta_t_ref, cnt_ref, hpk_ref, run_scr)


SSD_TAIL_SUBTILES = 2


def _ssd_tail(x, y, xq, mem_kv, wy, wc, g, w_router, batch):
    t, d = x.shape
    tile = SSD_TAIL_SUBTILES * TAIL_TILE
    nt, tok, specs = _tail_specs(batch, t, y.shape[1], tile)
    w1, w2 = _split_bf16(w_router.astype(F32), 2)
    wr = jnp.zeros((d, 2 * LANES), BF16).at[:, :N_EXPERTS].set(w1).at[:, LANES:LANES + N_EXPERTS].set(w2)
    return pl.pallas_call(
        _ssd_tail_kernel,
        grid=(batch, nt),
        in_specs=specs + [_resident((1, d)), _resident((d, 2 * LANES))],
        out_specs=[tok(d), tok(LANES), pl.BlockSpec((N_EXPERTS, tile), lambda b, i: (0, b * nt + i)),
                   pl.BlockSpec((N_EXPERTS, LANES), lambda b, i: (0, 0)), tok(d // 2)],
        out_shape=[jax.ShapeDtypeStruct((t, d), F32), jax.ShapeDtypeStruct((t, LANES), F32),
                   jax.ShapeDtypeStruct((N_EXPERTS, t), F32), jax.ShapeDtypeStruct((N_EXPERTS, LANES), F32),
                   jax.ShapeDtypeStruct((t, d // 2), jnp.uint32)],
        scratch_shapes=[pltpu.VMEM((N_EXPERTS, LANES), F32)]
        + [pltpu.VMEM((TAIL_TILE, XQ_W), BF16)] * SSD_TAIL_SUBTILES,
        compiler_params=_params(2),
        name="ssd_tail",
    )(x, y, xq, mem_kv, mem_kv, wy, wc, g.reshape(1, d), wr)


ROW_TILE = 512
DISPATCH_TILE = 4096
COMBINE_TILE = 1024


def _row_copy_all_wait(src_like, dst_like, sem):
    pltpu.make_async_copy(src_like, dst_like, sem).wait()


def _dispatch_kernel(pad_start_ref, dest_hbm, hpk_ref, xs_out, idx_smem, zero_buf, idx_sem, row_sem, fill_sem):
    tm = hpk_ref.shape[0]

    @pl.when(pl.program_id(0) == 0)
    def _():
        zero_buf[...] = jnp.zeros_like(zero_buf)
        fill_rows = zero_buf.shape[0]
        starts = [pad_start_ref[e] // SUBLANES * SUBLANES for e in range(N_EXPERTS)]
        last = xs_out.shape[0] - fill_rows
        n_tail = -(-(N_EXPERTS + 1) * ROW_TILE // fill_rows)
        starts += [jnp.minimum(pad_start_ref[N_EXPERTS] + k * fill_rows, last) for k in range(n_tail)]
        for s in starts:
            fill = pltpu.make_async_copy(zero_buf, xs_out.at[pl.ds(pl.multiple_of(s, SUBLANES), fill_rows)], fill_sem)
            fill.start()
            fill.wait()

    idx_copy = pltpu.make_async_copy(dest_hbm.at[pl.program_id(0)], idx_smem, idx_sem)
    idx_copy.start()
    idx_copy.wait()

    def issue(t, carry):
        for k in range(TOP_K):
            row = idx_smem[k * tm + t]
            pltpu.make_async_copy(hpk_ref.at[pl.ds(t, 1)], xs_out.at[pl.ds(row, 1)], row_sem).start()
        return carry

    lax.fori_loop(0, tm, issue, 0, unroll=8)
    for _ in range(TOP_K):
        _row_copy_all_wait(hpk_ref, xs_out.at[pl.ds(0, tm)], row_sem)


def _dispatch(pad_start, dest_tiles, hpk, n_rows):
    t, w = hpk.shape
    tm = dest_tiles.shape[1] // TOP_K
    any_spec = pl.BlockSpec(memory_space=pl.ANY)
    return pl.pallas_call(
        _dispatch_kernel,
        grid_spec=pltpu.PrefetchScalarGridSpec(
            num_scalar_prefetch=1,
            grid=(t // tm,),
            in_specs=[any_spec, pl.BlockSpec((tm, w), lambda i, ps: (i, 0))],
            out_specs=any_spec,
            scratch_shapes=[pltpu.SMEM((TOP_K * tm,), jnp.int32), pltpu.VMEM((ROW_TILE + SUBLANES, w), jnp.uint32),
                            pltpu.SemaphoreType.DMA, pltpu.SemaphoreType.DMA, pltpu.SemaphoreType.DMA]),
        out_shape=jax.ShapeDtypeStruct((n_rows, w), jnp.uint32),
        compiler_params=_params(1),
        name="dispatch",
    )(pad_start, dest_tiles, hpk)


def _experts_kernel(tile_expert_ref, n_used_ref, xs_ref, wg_ref, wu_ref, wd_ref, y_ref, act_scr):
    del tile_expert_ref

    @pl.when(pl.program_id(0) < n_used_ref[0])
    def _():
        h = _unpack_bf16_pairs(xs_ref[...])
        for c0 in range(0, D_FF, FF_CHUNK):
            sl = slice(c0, c0 + FF_CHUNK)
            act_scr[:, sl] = (_silu(_dot(h, wg_ref[:, sl])) * _dot(h, wu_ref[:, sl])).astype(BF16)
        y_ref[...] = _dot(act_scr[...], wd_ref[...])

    @pl.when(pl.program_id(0) >= n_used_ref[0])
    def _():
        y_ref[...] = jnp.zeros_like(y_ref)


def _experts(tile_expert, n_used, xs, wg, wu, wd, layer):
    n_rows, w = xs.shape
    d = 2 * w
    tm = ROW_TILE
    wspec = lambda shape: pl.BlockSpec((None, None) + shape, lambda i, te, nu: (layer, te[i], 0, 0))
    return pl.pallas_call(
        _experts_kernel,
        grid_spec=pltpu.PrefetchScalarGridSpec(
            num_scalar_prefetch=2,
            grid=(n_rows // tm,),
            in_specs=[pl.BlockSpec((tm, w), lambda i, te, nu: (jnp.minimum(i, nu[0] - 1), 0)),
                      wspec((d, D_FF)), wspec((d, D_FF)), wspec((D_FF, d))],
            out_specs=pl.BlockSpec((tm, d), lambda i, te, nu: (i, 0)),
            scratch_shapes=[pltpu.VMEM((tm, D_FF), BF16)]),
        out_shape=jax.ShapeDtypeStruct((n_rows, d), F32),
        compiler_params=_params(1),
        name="experts",
    )(tile_expert, n_used, xs, wg, wu, wd)


def _combine_kernel(dest_hbm, x_ref, meta_ref, y_hbm, gf_ref, o_ref, idx_smem, y_buf, idx_sem, row_sem, *,
                    final_norm):
    tm = x_ref.shape[0]
    idx_copy = pltpu.make_async_copy(dest_hbm.at[pl.program_id(0)], idx_smem, idx_sem)
    idx_copy.start()
    idx_copy.wait()

    def issue(t, carry):
        for k in range(TOP_K):
            row = idx_smem[k * tm + t]
            pltpu.make_async_copy(y_hbm.at[pl.ds(row, 1)], y_buf.at[k, pl.ds(t, 1)], row_sem).start()
        return carry

    lax.fori_loop(0, tm, issue, 0, unroll=8)
    for k in range(TOP_K):
        _row_copy_all_wait(y_hbm.at[pl.ds(0, tm)], y_buf.at[k], row_sem)
    meta = meta_ref[...]
    out = x_ref[...]
    for k in range(TOP_K):
        out = out + meta[:, META_GATE + k:META_GATE + k + 1] * y_buf[k]
    o_ref[...] = _rms(out, gf_ref[...]) if final_norm else out


def _combine(dest_tiles, x, meta, y, g_final):
    t, d = x.shape
    tm = dest_tiles.shape[1] // TOP_K
    any_spec = pl.BlockSpec(memory_space=pl.ANY)
    final_norm = g_final is not None
    gf = (g_final if final_norm else jnp.ones((d,), F32)).reshape(1, d)
    return pl.pallas_call(
        functools.partial(_combine_kernel, final_norm=final_norm),
        grid=(t // tm,),
        in_specs=[any_spec, pl.BlockSpec((tm, d), lambda i: (i, 0)),
                  pl.BlockSpec((tm, LANES), lambda i: (i, 0)), any_spec, _resident((1, d))],
        out_specs=pl.BlockSpec((tm, d), lambda i: (i, 0)),
        out_shape=jax.ShapeDtypeStruct((t, d), F32),
        scratch_shapes=[pltpu.SMEM((TOP_K * tm,), jnp.int32), pltpu.VMEM((TOP_K, tm, d), F32),
                        pltpu.SemaphoreType.DMA, pltpu.SemaphoreType.DMA],
        compiler_params=_params(1),
        name="combine",
    )(dest_tiles, x, meta, y, gf)


def _moe(x, meta, meta_t, counts, hpk, wg, wu, wd, layer, g_final=None):
    t, d = x.shape
    cnt = counts[:, 0].astype(jnp.int32)
    padded = (cnt + ROW_TILE - 1) // ROW_TILE * ROW_TILE
    ends = jnp.cumsum(padded)
    starts = ends - padded
    expert = meta_t[META_EXPERT:META_EXPERT + TOP_K].astype(jnp.int32)
    dest = meta_t[META_POS:META_POS + TOP_K].astype(jnp.int32)
    for e in range(N_EXPERTS):
        dest = dest + jnp.where(expert == e, starts[e], 0)

    def dest_tiles(tile):
        return dest.reshape(TOP_K, t // tile, tile).transpose(1, 0, 2).reshape(-1, TOP_K * tile)

    n_rows = TOP_K * t + (N_EXPERTS + 1) * ROW_TILE
    n_tiles = n_rows // ROW_TILE
    n_used = ends[-1] // ROW_TILE
    tile_start = jnp.minimum(jnp.arange(n_tiles), n_used - 1) * ROW_TILE
    tile_expert = jnp.sum(tile_start[:, None] >= ends[None, :], axis=1).astype(jnp.int32)
    pad_start = jnp.concatenate([starts + cnt, ends[-1:]])
    xs = _dispatch(pad_start, dest_tiles(DISPATCH_TILE), hpk, n_rows)
    y = _experts(tile_expert, n_used.reshape(1).astype(jnp.int32), xs, wg, wu, wd, layer)
    return _combine(dest_tiles(COMBINE_TILE), x, meta, y, g_final)


CONV_TAIL = 16


def _conv_shift_matrix():
    m = np.zeros(((CONV_WIDTH - 1) * CHUNK, CONV_TAIL + CHUNK), np.float32)
    for k in range(CONV_WIDTH - 1):
        for t in range(CHUNK):
            m[k * CHUNK + t, CONV_TAIL + t - (CONV_WIDTH - 1) + k] = 1.0
    return m


def _split_bf16(v, parts):
    out, r = [], v
    for _ in range(parts):
        p = r.astype(BF16)
        out.append(p)
        r = r - p.astype(F32)
    return out


def _ssd_kernel(xbc_ref, z_ref, dt_ref, cw_ref, cb_ref, dtb_ref, alog_ref, dsk_ref, gn_ref, exp_ref, shift_ref,
                o_ref, state_scr, *chunk_scr):
    n_sub = xbc_ref.shape[0] // CHUNK
    ext_scrs, y_scrs = chunk_scr[:n_sub], chunk_scr[n_sub:]

    @pl.when(pl.program_id(1) == 0)
    def _():
        state_scr[...] = jnp.zeros_like(state_scr)
        ext_scrs[n_sub - 1][CHUNK:, :] = jnp.zeros((CONV_TAIL, CONV_CH), BF16)

    for c in range(n_sub):
        rows = pl.ds(c * CHUNK, CHUNK)
        _ssd_chunk(xbc_ref.at[rows], z_ref.at[rows], dt_ref.at[rows], cw_ref, cb_ref, dtb_ref, alog_ref, dsk_ref,
                   gn_ref, exp_ref, shift_ref, o_ref.at[rows], state_scr, ext_scrs[c],
                   ext_scrs[(c - 1) % n_sub], y_scrs[c])


def _ssd_chunk(xbc_ref, z_ref, dt_ref, cw_ref, cb_ref, dtb_ref, alog_ref, dsk_ref, gn_ref, exp_ref, shift_ref,
               o_ref, state_scr, ext_scr, prev_ext_scr, y_scr):
    ext_scr[0:CONV_TAIL, :] = prev_ext_scr[CHUNK:, :]
    ext_scr[CONV_TAIL:, :] = xbc_ref[...]

    def conv_silu(c0, c1):
        shifted = _dot(shift_ref[...], ext_scr[:, c0:c1])
        acc = cb_ref[:, c0:c1] + cw_ref[CONV_WIDTH - 1:CONV_WIDTH, c0:c1] * ext_scr[CONV_TAIL:, c0:c1].astype(F32)
        for k in range(CONV_WIDTH - 1):
            acc = acc + cw_ref[k:k + 1, c0:c1] * shifted[k * CHUNK:(k + 1) * CHUNK, :]
        return _silu(acc)

    dt = dt_ref[...] + dtb_ref[...]
    dt = jnp.maximum(dt, 0.0) + jnp.log(1.0 + jnp.exp(-jnp.abs(dt)))
    da = dt * -jnp.exp(alog_ref[...])
    row = lax.broadcasted_iota(jnp.int32, (CHUNK, CHUNK), 0)
    col = lax.broadcasted_iota(jnp.int32, (CHUNK, CHUNK), 1)
    causal = col <= row
    cs = _dot(causal.astype(BF16), jnp.concatenate(_split_bf16(da, 3), axis=1))
    acum = cs[:, 0:LANES] + cs[:, LANES:2 * LANES] + cs[:, 2 * LANES:]
    acum_t = acum.T
    ea = jnp.exp(acum)
    dtte = dt * jnp.exp(acum[CHUNK - 1:CHUNK, :] - acum)

    per_head = jnp.concatenate(
        [jnp.concatenate(_split_bf16(v, 2), axis=1) for v in (ea, dt, dtte)], axis=0)

    low = lax.broadcasted_iota(jnp.int32, (CHUNK, LANES), 1) < SSM_HEAD_DIM
    for g in range(N_SSM_GROUPS):
        gsl = slice(g * GROUP_W, (g + 1) * GROUP_W)
        xg = conv_silu(g * GROUP_W, (g + 1) * GROUP_W)
        bg = conv_silu(D_INNER + g * D_STATE, D_INNER + (g + 1) * D_STATE)
        cg = conv_silu(D_INNER + BC_W + g * D_STATE, D_INNER + BC_W + (g + 1) * D_STATE).astype(BF16)
        expanded = _dot(per_head, exp_ref[:, gsl])
        ea_x, dt_x, dtte_x = (expanded[k * CHUNK:(k + 1) * CHUNK, :] for k in range(3))
        xdt = (xg * dt_x).astype(BF16)
        cb = lax.dot_general(cg, bg.astype(BF16), NT_DIMS, preferred_element_type=F32)
        state = state_scr[g]
        y = _dot(cg, state.astype(BF16)) * ea_x + xg * dsk_ref[:, gsl]
        diag = []
        for pair in range(HEADS_PER_GROUP // 2):
            xp = xdt[:, pair * LANES:(pair + 1) * LANES]
            acc = jnp.zeros((CHUNK, LANES), F32)
            for half in range(2):
                h = g * HEADS_PER_GROUP + 2 * pair + half
                seg = acum[:, h:h + 1] - acum_t[h:h + 1, :]
                m = (cb * jnp.exp(jnp.where(causal, seg, NEG_INF))).astype(BF16)
                xm = jnp.where(low if half == 0 else jnp.logical_not(low), xp, jnp.zeros_like(xp))
                acc = acc + _dot(m, xm)
            diag.append(acc)
        y_scr[:, gsl] = y + jnp.concatenate(diag, axis=1)
        w = (xg * dtte_x).astype(BF16)
        state_scr[g] = state * ea_x[CHUNK - 1:CHUNK, :] + _dot(bg.T.astype(BF16), w)

    z = z_ref[...].astype(F32)
    o_ref[...] = _rms(y_scr[...] * _silu(z), gn_ref[...]).astype(o_ref.dtype)


SSD_SUBCHUNKS = 4


def _ssd(xbc, z, dt_raw, conv_w, conv_b, dt_bias, a_log, d_skip, g_norm, batch):
    t = xbc.shape[0]
    step = SSD_SUBCHUNKS * CHUNK
    nc = t // batch // step

    def lane_pad(v):
        return jnp.zeros((1, LANES), F32).at[0, :N_SSM_HEADS].set(v)

    expand = np.zeros((LANES, D_INNER), np.float32)
    for h in range(N_SSM_HEADS):
        expand[h, h * SSM_HEAD_DIM:(h + 1) * SSM_HEAD_DIM] = 1.0
    expand2 = jnp.asarray(np.concatenate([expand, expand], axis=0), BF16)
    tok = lambda b, c: (b * nc + c, 0)
    return pl.pallas_call(
        _ssd_kernel,
        grid=(batch, nc),
        in_specs=[pl.BlockSpec((step, CONV_CH), tok), pl.BlockSpec((step, D_INNER), tok),
                  pl.BlockSpec((step, LANES), tok),
                  _resident((CONV_WIDTH, CONV_CH)), _resident((1, CONV_CH)),
                  _resident((1, LANES)), _resident((1, LANES)),
                  _resident((1, D_INNER)), _resident((1, D_INNER)), _resident((2 * LANES, D_INNER)),
                  _resident(((CONV_WIDTH - 1) * CHUNK, CONV_TAIL + CHUNK))],
        out_specs=pl.BlockSpec((step, D_INNER), tok),
        out_shape=jax.ShapeDtypeStruct((t, D_INNER), BF16),
        scratch_shapes=[pltpu.VMEM((N_SSM_GROUPS, D_STATE, GROUP_W), F32)]
        + [pltpu.VMEM((CONV_TAIL + CHUNK, CONV_CH), BF16)] * SSD_SUBCHUNKS
        + [pltpu.VMEM((CHUNK, D_INNER), F32)] * SSD_SUBCHUNKS,
        compiler_params=_params(2),
        name="ssd",
    )(xbc, z, dt_raw, conv_w, conv_b.reshape(1, CONV_CH), lane_pad(dt_bias), lane_pad(a_log),
      jnp.repeat(d_skip, SSM_HEAD_DIM).reshape(1, D_INNER), g_norm.reshape(1, D_INNER), expand2,
      jnp.asarray(_conv_shift_matrix(), BF16))


def kernel(x, mem, g_mix, g_ffn, g_mem, w_mem_kv, rel_bias, swa_w_in, swa_sinks, swa_w_out, ssm_w_in, ssm_conv_w, ssm_conv_b, ssm_dt_bias, ssm_A_log, ssm_D, ssm_g_norm, ssm_w_out, ffn_w_gate, ffn_w_up, ffn_w_down, moe_w_router, moe_w_gate, moe_w_up, moe_w_down, g_final):
    batch, seq, d = x.shape
    xf = x.reshape(batch * seq, d)
    memf = mem.reshape(batch * MEM_LEN, d)
    moe_wg, moe_wu, moe_wd = moe_w_gate.astype(BF16), moe_w_up.astype(BF16), moe_w_down.astype(BF16)
    for i in range(DEPTH):
        j = i // 2
        (mem_kv,) = _norm_proj(memf, g_mem[i], [w_mem_kv[i].astype(BF16)], [BF16])
        if i % 2 == 0:
            w_in = swa_w_in[j]
            w_q = (w_in[:, :Q_W] * HEAD_DIM ** -0.5).astype(BF16)
            q, kv, xq = _norm_proj(
                xf, g_mix[i], [w_q, w_in[:, Q_W:Q_W + 2 * KV_W].astype(BF16), w_in[:, Q_W + 2 * KV_W:].astype(BF16)],
                [BF16, BF16, BF16])
            attn = _swa(q, kv, rel_bias, swa_sinks[j], batch)
            w_out = swa_w_out[j].astype(BF16)
            xf = _attn_tail(xf, attn, xq, mem_kv, w_out[:Q_W], w_out[Q_W:], g_ffn[i],
                            ffn_w_gate[j].astype(BF16), ffn_w_up[j].astype(BF16), ffn_w_down[j].astype(BF16),
                            batch)
        else:
            w_in = ssm_w_in[j].astype(BF16)
            o_dt = D_INNER + CONV_CH
            w_dt = jnp.zeros((d, LANES), BF16).at[:, :N_SSM_HEADS].set(w_in[:, o_dt:o_dt + N_SSM_HEADS])
            z, xbc, dt_raw, xq = _norm_proj(
                xf, g_mix[i], [w_in[:, :D_INNER], w_in[:, D_INNER:o_dt], w_dt, w_in[:, o_dt + N_SSM_HEADS:]],
                [BF16, BF16, F32, BF16])
            y = _ssd(xbc, z, dt_raw, ssm_conv_w[j], ssm_conv_b[j], ssm_dt_bias[j], ssm_A_log[j],
                     ssm_D[j], ssm_g_norm[j], batch)
            w_out = ssm_w_out[j].astype(BF16)
            xf, meta, meta_t, counts, hpk = _ssd_tail(xf, y, xq, mem_kv, w_out[:D_INNER], w_out[D_INNER:],
                                                      g_ffn[i], moe_w_router[j], batch)
            xf = _moe(xf, meta, meta_t, counts, hpk, moe_wg, moe_wu, moe_wd, j,
                      g_final if i == DEPTH - 1 else None)
    assert DEPTH % 2 == 0
    return xf.reshape(batch, seq, d)
```

```python
import functools
import math

import numpy as np
import jax
import jax.numpy as jnp
from jax import lax
from jax.experimental import pallas as pl
from jax.experimental.pallas import tpu as pltpu

F32 = jnp.float32
BF16 = jnp.bfloat16

D_MODEL = 1024
DEPTH = 4
MEM_LEN = 256
EPS = 1e-6
N_Q_HEADS = 16
N_KV_HEADS = 2
HEAD_DIM = 64
BLOCK = 128
N_BUCKETS = 32
MAX_DISTANCE = 128
N_X_HEADS = 4
X_HEAD_DIM = 256
D_INNER = 2048
SSM_HEAD_DIM = 64
N_SSM_HEADS = 32
N_SSM_GROUPS = 4
HEADS_PER_GROUP = 8
D_STATE = 128
CONV_WIDTH = 4
CHUNK = 128
D_FF = 2816
N_EXPERTS = 8
Q_W = N_Q_HEADS * HEAD_DIM
KV_W = N_KV_HEADS * HEAD_DIM
XQ_W = N_X_HEADS * X_HEAD_DIM
BC_W = N_SSM_GROUPS * D_STATE
CONV_CH = D_INNER + 2 * BC_W
GROUP_W = HEADS_PER_GROUP * SSM_HEAD_DIM

LANES = 128
SUBLANES = 8
VMEM_LIMIT = 56 << 20
NEG_INF = float("-inf")
NT_DIMS = (((1,), (1,)), ((), ()))


def _params(n_axes, vmem=VMEM_LIMIT):
    return pltpu.CompilerParams(dimension_semantics=("arbitrary",) * n_axes, vmem_limit_bytes=vmem)


def _resident(shape):
    nd = len(shape)
    return pl.BlockSpec(shape, lambda *_: (0,) * nd, pipeline_mode=pl.Buffered(1))


def _dot(a, b):
    return jnp.dot(a, b, preferred_element_type=F32)


def _rms(x, g):
    return x * lax.rsqrt(jnp.mean(x * x, axis=-1, keepdims=True) + EPS) * g


def _silu(v):
    return v / (1.0 + jnp.exp(-v))


def _norm_proj_kernel(x_ref, g_ref, *refs, n_out, col_chunk):
    w_refs, o_refs = refs[:n_out], refs[n_out:]
    h = _rms(x_ref[...], g_ref[...]).astype(BF16)
    for w_ref, o_ref in zip(w_refs, o_refs):
        n = w_ref.shape[1]
        for c0 in range(0, n, col_chunk):
            c1 = min(c0 + col_chunk, n)
            o_ref[:, c0:c1] = _dot(h, w_ref[:, c0:c1]).astype(o_ref.dtype)


def _norm_proj(x, g, ws, out_dtypes, tm=512):
    t, d = x.shape
    n_out = len(ws)
    return pl.pallas_call(
        functools.partial(_norm_proj_kernel, n_out=n_out, col_chunk=512),
        grid=(t // tm,),
        in_specs=[pl.BlockSpec((tm, d), lambda i: (i, 0)), _resident((1, d))]
        + [_resident(w.shape) for w in ws],
        out_specs=[pl.BlockSpec((tm, w.shape[1]), lambda i: (i, 0)) for w in ws],
        out_shape=[jax.ShapeDtypeStruct((t, w.shape[1]), dt) for w, dt in zip(ws, out_dtypes)],
        compiler_params=_params(1),
        name="norm_proj",
    )(x, g.reshape(1, d), *ws)


def _bucket_table():
    qi = np.arange(BLOCK)[:, None]
    kj = np.arange(2 * BLOCK)[None, :]
    dist = BLOCK + qi - kj
    max_exact = N_BUCKETS // 2
    d = np.maximum(dist, 0)
    df = np.maximum(d, 1).astype(np.float32)
    far = max_exact + (
        np.log(df / np.float32(max_exact)) / np.float32(math.log(MAX_DISTANCE / max_exact))
        * np.float32(N_BUCKETS - max_exact)
    ).astype(np.int32)
    bucket = np.where(d < max_exact, d, np.minimum(far, N_BUCKETS - 1))
    r = np.arange(BLOCK)[:, None]
    c = np.arange(BLOCK)[None, :]
    return np.where(c <= r, bucket[:, BLOCK:], bucket[:, :BLOCK]).astype(np.int32)


PAIRS_PER_KV = N_Q_HEADS // N_KV_HEADS // 2
SLOT_ROWS = PAIRS_PER_KV * BLOCK


def _swa_slot_head(slot):
    kvh, rest = divmod(slot, 2 * PAIRS_PER_KV)
    half, pair = divmod(rest, PAIRS_PER_KV)
    return 2 * (kvh * PAIRS_PER_KV + pair) + half


def _swa_kernel(relb_ref, bucket_ref, sink_ref, q_ref, kvp_ref, kvc_ref, o_ref, bias_scr, s_scr, p_scr):
    first = (pl.program_id(0) == 0) & (pl.program_id(1) == 0)

    def from_current(n_rows):
        r = lax.broadcasted_iota(jnp.int32, (n_rows, BLOCK), 0) & (BLOCK - 1)
        return lax.broadcasted_iota(jnp.int32, (n_rows, BLOCK), 1) <= r

    @pl.when(first)
    def _():
        bucket = bucket_ref[...]
        for slot in range(N_Q_HEADS):
            h = _swa_slot_head(slot)
            acc = jnp.zeros((BLOCK, BLOCK), F32)
            for n in range(N_BUCKETS):
                acc = jnp.where(bucket == n, relb_ref[n, h], acc)
            rows = slice(slot * BLOCK, (slot + 1) * BLOCK)
            bias_scr[1, rows, :] = acc
            bias_scr[0, rows, :] = jnp.where(from_current(BLOCK), acc, NEG_INF)

    kk = jnp.concatenate([kvp_ref[:, 0:KV_W], kvc_ref[:, 0:KV_W]], axis=0).astype(F32)
    vv = jnp.concatenate([kvp_ref[:, KV_W:], kvc_ref[:, KV_W:]], axis=0).astype(F32)
    low = lax.broadcasted_iota(jnp.int32, kk.shape, 1) < HEAD_DIM

    def placed(t):
        r = pltpu.roll(t, HEAD_DIM, 1)
        return [[jnp.where(low, t, 0.0).astype(BF16), jnp.where(low, 0.0, r).astype(BF16)],
                [jnp.where(low, r, 0.0).astype(BF16), jnp.where(low, 0.0, t).astype(BF16)]]

    k_var, v_var = placed(kk), placed(vv)

    cur_slot = from_current(SLOT_ROWS)
    for v in range(2 * N_KV_HEADS):
        kvh, half = divmod(v, 2)
        q_stack = jnp.concatenate(
            [q_ref[:, (kvh * PAIRS_PER_KV + j) * LANES:(kvh * PAIRS_PER_KV + j + 1) * LANES]
             for j in range(PAIRS_PER_KV)], axis=0)
        s = lax.dot_general(q_stack, k_var[kvh][half], NT_DIMS, preferred_element_type=F32)
        s_scr[v * SLOT_ROWS:(v + 1) * SLOT_ROWS, :] = jnp.where(cur_slot, s[:, BLOCK:], s[:, :BLOCK])

    s = s_scr[...] + bias_scr[jnp.minimum(pl.program_id(1), 1)]
    sink = sink_ref[...]
    m = jnp.maximum(jnp.max(s, axis=-1, keepdims=True), sink)
    p = jnp.exp(s - m)
    p = p / (jnp.sum(p, axis=-1, keepdims=True) + jnp.exp(sink - m))
    p_scr[...] = p.astype(BF16)

    outs = []
    for v in range(2 * N_KV_HEADS):
        kvh, half = divmod(v, 2)
        p = p_scr[v * SLOT_ROWS:(v + 1) * SLOT_ROWS, :]
        zero = jnp.zeros_like(p)
        p_band = jnp.concatenate([jnp.where(cur_slot, zero, p), jnp.where(cur_slot, p, zero)], axis=1)
        outs.append(_dot(p_band, v_var[kvh][half]))
    for kvh in range(N_KV_HEADS):
        for j in range(PAIRS_PER_KV):
            pair = kvh * PAIRS_PER_KV + j
            out = outs[2 * kvh][j * BLOCK:(j + 1) * BLOCK, :] + outs[2 * kvh + 1][j * BLOCK:(j + 1) * BLOCK, :]
            o_ref[:, pair * LANES:(pair + 1) * LANES] = out.astype(o_ref.dtype)


def _swa(q, kv, rel_bias, sinks, batch):
    t = q.shape[0]
    nb = t // batch // BLOCK
    rows = N_Q_HEADS * BLOCK
    bucket = jnp.asarray(_bucket_table())
    slot_sinks = sinks.astype(F32)[np.array([_swa_slot_head(s) for s in range(N_Q_HEADS)])]
    sink_rows = jnp.broadcast_to(jnp.repeat(slot_sinks, BLOCK)[:, None], (rows, LANES))
    smem = pl.BlockSpec(memory_space=pltpu.SMEM)
    return pl.pallas_call(
        _swa_kernel,
        grid=(batch, nb),
        in_specs=[smem, _resident((BLOCK, BLOCK)), _resident((rows, LANES)),
                  pl.BlockSpec((BLOCK, Q_W), lambda b, i: (b * nb + i, 0)),
                  pl.BlockSpec((BLOCK, 2 * KV_W), lambda b, i: (b * nb + jnp.maximum(i - 1, 0), 0)),
                  pl.BlockSpec((BLOCK, 2 * KV_W), lambda b, i: (b * nb + i, 0))],
        out_specs=pl.BlockSpec((BLOCK, Q_W), lambda b, i: (b * nb + i, 0)),
        out_shape=jax.ShapeDtypeStruct((t, Q_W), BF16),
        scratch_shapes=[pltpu.VMEM((2, rows, BLOCK), F32), pltpu.VMEM((rows, BLOCK), F32),
                        pltpu.VMEM((rows, BLOCK), BF16)],
        compiler_params=_params(2),
        name="swa",
    )(rel_bias, bucket, sink_rows, q, kv, kv)


TAIL_TILE = 512
FF_CHUNK = 256


def _cross_attention(xq_ref, mk_ref, mv_ref, cross_scr):
    for h in range(N_X_HEADS):
        sl = slice(h * X_HEAD_DIM, (h + 1) * X_HEAD_DIM)
        s = lax.dot_general(xq_ref[:, sl], mk_ref[:, sl], NT_DIMS, preferred_element_type=F32)
        s = s * (X_HEAD_DIM ** -0.5)
        p = jnp.exp(s - jnp.max(s, axis=-1, keepdims=True))
        p = p / jnp.sum(p, axis=-1, keepdims=True)
        cross_scr[:, sl] = _dot(p.astype(BF16), mv_ref[:, sl]).astype(cross_scr.dtype)


def _swiglu_residual(x1, g_ref, wg_ref, wu_ref, wd_ref, act_scr):
    h = _rms(x1, g_ref[...]).astype(BF16)
    for c0 in range(0, D_FF, FF_CHUNK):
        sl = slice(c0, c0 + FF_CHUNK)
        act_scr[:, sl] = (_silu(_dot(h, wg_ref[:, sl])) * _dot(h, wu_ref[:, sl])).astype(BF16)
    return x1 + _dot(act_scr[...], wd_ref[...])


def _tail_specs(batch, t, a_width, tile=TAIL_TILE):
    nt = t // batch // tile
    tok = lambda w: pl.BlockSpec((tile, w), lambda b, i: (b * nt + i, 0))
    mem = lambda half: pl.BlockSpec((MEM_LEN, XQ_W), lambda b, i: (b, half))
    return nt, tok, [tok(D_MODEL), tok(a_width), tok(XQ_W), mem(0), mem(1),
                     _resident((a_width, D_MODEL)), _resident((XQ_W, D_MODEL))]


def _attn_tail_kernel(x_ref, a_ref, xq_ref, mk_ref, mv_ref, wa_ref, wc_ref, g_ref, wg_ref, wu_ref, wd_ref,
                      o_ref, cross_scr, act_scr):
    _cross_attention(xq_ref, mk_ref, mv_ref, cross_scr)
    x1 = x_ref[...] + _dot(a_ref[...], wa_ref[...]) + _dot(cross_scr[...], wc_ref[...])
    o_ref[...] = _swiglu_residual(x1, g_ref, wg_ref, wu_ref, wd_ref, act_scr)


def _attn_tail(x, attn, xq, mem_kv, wa, wc, g, wg, wu, wd, batch):
    t, d = x.shape
    nt, tok, specs = _tail_specs(batch, t, attn.shape[1])
    return pl.pallas_call(
        _attn_tail_kernel,
        grid=(batch, nt),
        in_specs=specs + [_resident((1, d)), _resident(wg.shape), _resident(wu.shape), _resident(wd.shape)],
        out_specs=tok(d),
        out_shape=jax.ShapeDtypeStruct((t, d), F32),
        scratch_shapes=[pltpu.VMEM((TAIL_TILE, XQ_W), BF16), pltpu.VMEM((TAIL_TILE, D_FF), BF16)],
        compiler_params=_params(2),
        name="attn_tail",
    )(x, attn, xq, mem_kv, mem_kv, wa, wc, g.reshape(1, d), wg, wu, wd)


TOP_K = 2
HI16 = 0xFFFF0000
META_EXPERT, META_POS, META_GATE = 0, 2, 4


def _pack_bf16_pairs(h):
    n = h.shape[1] // 2
    u = pltpu.bitcast(h.astype(BF16).astype(F32), jnp.uint32)
    return (u[:, :n] >> 16) | (u[:, n:] & jnp.uint32(HI16))


def _unpack_bf16_pairs(w):
    lo = pltpu.bitcast(w << 16, F32).astype(BF16)
    hi = pltpu.bitcast(w & jnp.uint32(HI16), F32).astype(BF16)
    return jnp.concatenate([lo, hi], axis=1)


def _route(x1, g_ref, wr_ref, meta_ref, meta_t_ref, cnt_ref, hpk_ref, run_scr):
    h = _rms(x1, g_ref[...])
    hpk_ref[...] = _pack_bf16_pairs(h)
    tm = h.shape[0]
    prod = _dot(jnp.concatenate(_split_bf16(h, 2), axis=0), wr_ref[...])
    prod = prod[:tm, :] + prod[tm:, :]
    logits = prod[:, :LANES] + prod[:, LANES:]
    lt = logits.T[:N_EXPERTS, :]
    sub = lax.broadcasted_iota(jnp.int32, lt.shape, 0)
    m1 = jnp.max(lt, axis=0, keepdims=True)
    i1 = jnp.min(jnp.where(lt == m1, sub, N_EXPERTS), axis=0, keepdims=True)
    lt2 = jnp.where(sub == i1, NEG_INF, lt)
    m2 = jnp.max(lt2, axis=0, keepdims=True)
    i2 = jnp.min(jnp.where(lt2 == m2, sub, N_EXPERTS), axis=0, keepdims=True)
    e = jnp.exp(m2 - m1)
    g1, g2 = 1.0 / (1.0 + e), e / (1.0 + e)

    onehot = jnp.where(jnp.logical_or(sub == i1, sub == i2), 1.0, 0.0)
    onehot16 = jnp.concatenate([onehot, jnp.zeros_like(onehot)], axis=0).astype(BF16)
    earlier = lax.broadcasted_iota(jnp.int32, (tm, tm), 0) < lax.broadcasted_iota(jnp.int32, (tm, tm), 1)
    before = _dot(onehot16, earlier.astype(BF16))[:N_EXPERTS, :] + run_scr[:, 0:1]
    pos1 = jnp.sum(jnp.where(sub == i1, before, 0.0), axis=0, keepdims=True)
    pos2 = jnp.sum(jnp.where(sub == i2, before, 0.0), axis=0, keepdims=True)
    run_scr[...] += jnp.sum(onehot, axis=1, keepdims=True)
    cnt_ref[...] = run_scr[...]

    def put(k, v):
        return jnp.where(sub == k, v, 0.0)

    meta_t = (put(META_EXPERT, i1.astype(F32)) + put(META_EXPERT + 1, i2.astype(F32))
              + put(META_POS, pos1) + put(META_POS + 1, pos2)
              + put(META_GATE, g1) + put(META_GATE + 1, g2))
    meta_t_ref[...] = meta_t
    meta_ref[...] = jnp.concatenate([meta_t, jnp.zeros((LANES - N_EXPERTS, tm), F32)], axis=0).T


def _ssd_tail_kernel(x_ref, y_ref, xq_ref, mk_ref, mv_ref, wy_ref, wc_ref, g_ref, wr_ref,
                     x1_ref, meta_ref, meta_t_ref, cnt_ref, hpk_ref, run_scr, *cross_scrs):
    @pl.when((pl.program_id(0) == 0) & (pl.program_id(1) == 0))
    def _():
        run_scr[...] = jnp.zeros_like(run_scr)

    for s in range(x_ref.shape[0] // TAIL_TILE):
        rows = pl.ds(s * TAIL_TILE, TAIL_TILE)
        _cross_attention(xq_ref.at[rows], mk_ref, mv_ref, cross_scrs[s])
        x1_ref[rows, :] = x_ref[rows, :] + _dot(y_ref[rows, :], wy_ref[...]) + _dot(cross_scrs[s][...], wc_ref[...])
    _route(x1_ref[...], g_ref, wr_ref, meta_ref, meta_t_ref, cnt_ref, hpk_ref, run_scr)


SSD_TAIL_SUBTILES = 2


def _ssd_tail(x, y, xq, mem_kv, wy, wc, g, w_router, batch):
    t, d = x.shape
    tile = SSD_TAIL_SUBTILES * TAIL_TILE
    nt, tok, specs = _tail_specs(batch, t, y.shape[1], tile)
    w1, w2 = _split_bf16(w_router.astype(F32), 2)
    wr = jnp.zeros((d, 2 * LANES), BF16).at[:, :N_EXPERTS].set(w1).at[:, LANES:LANES + N_EXPERTS].set(w2)
    return pl.pallas_call(
        _ssd_tail_kernel,
        grid=(batch, nt),
        in_specs=specs + [_resident((1, d)), _resident((d, 2 * LANES))],
        out_specs=[tok(d), tok(LANES), pl.BlockSpec((N_EXPERTS, tile), lambda b, i: (0, b * nt + i)),
                   pl.BlockSpec((N_EXPERTS, LANES), lambda b, i: (0, 0)), tok(d // 2)],
        out_shape=[jax.ShapeDtypeStruct((t, d), F32), jax.ShapeDtypeStruct((t, LANES), F32),
                   jax.ShapeDtypeStruct((N_EXPERTS, t), F32), jax.ShapeDtypeStruct((N_EXPERTS, LANES), F32),
                   jax.ShapeDtypeStruct((t, d // 2), jnp.uint32)],
        scratch_shapes=[pltpu.VMEM((N_EXPERTS, LANES), F32)]
        + [pltpu.VMEM((TAIL_TILE, XQ_W), BF16)] * SSD_TAIL_SUBTILES,
        compiler_params=_params(2),
        name="ssd_tail",
    )(x, y, xq, mem_kv, mem_kv, wy, wc, g.reshape(1, d), wr)


ROW_TILE = 512
DISPATCH_TILE = 4096
COMBINE_TILE = 1024


def _row_copy_all_wait(src_like, dst_like, sem):
    pltpu.make_async_copy(src_like, dst_like, sem).wait()


def _dispatch_kernel(pad_start_ref, dest_hbm, hpk_ref, xs_out, idx_smem, zero_buf, idx_sem, row_sem, fill_sem):
    tm = hpk_ref.shape[0]

    @pl.when(pl.program_id(0) == 0)
    def _():
        zero_buf[...] = jnp.zeros_like(zero_buf)
        fill_rows = zero_buf.shape[0]
        starts = [pad_start_ref[e] // SUBLANES * SUBLANES for e in range(N_EXPERTS)]
        last = xs_out.shape[0] - fill_rows
        n_tail = -(-(N_EXPERTS + 1) * ROW_TILE // fill_rows)
        starts += [jnp.minimum(pad_start_ref[N_EXPERTS] + k * fill_rows, last) for k in range(n_tail)]
        for s in starts:
            fill = pltpu.make_async_copy(zero_buf, xs_out.at[pl.ds(pl.multiple_of(s, SUBLANES), fill_rows)], fill_sem)
            fill.start()
            fill.wait()

    idx_copy = pltpu.make_async_copy(dest_hbm.at[pl.program_id(0)], idx_smem, idx_sem)
    idx_copy.start()
    idx_copy.wait()

    def issue(t, carry):
        for k in range(TOP_K):
            row = idx_smem[k * tm + t]
            pltpu.make_async_copy(hpk_ref.at[pl.ds(t, 1)], xs_out.at[pl.ds(row, 1)], row_sem).start()
        return carry

    lax.fori_loop(0, tm, issue, 0, unroll=8)
    for _ in range(TOP_K):
        _row_copy_all_wait(hpk_ref, xs_out.at[pl.ds(0, tm)], row_sem)


def _dispatch(pad_start, dest_tiles, hpk, n_rows):
    t, w = hpk.shape
    tm = dest_tiles.shape[1] // TOP_K
    any_spec = pl.BlockSpec(memory_space=pl.ANY)
    return pl.pallas_call(
        _dispatch_kernel,
        grid_spec=pltpu.PrefetchScalarGridSpec(
            num_scalar_prefetch=1,
            grid=(t // tm,),
            in_specs=[any_spec, pl.BlockSpec((tm, w), lambda i, ps: (i, 0))],
            out_specs=any_spec,
            scratch_shapes=[pltpu.SMEM((TOP_K * tm,), jnp.int32), pltpu.VMEM((ROW_TILE + SUBLANES, w), jnp.uint32),
                            pltpu.SemaphoreType.DMA, pltpu.SemaphoreType.DMA, pltpu.SemaphoreType.DMA]),
        out_shape=jax.ShapeDtypeStruct((n_rows, w), jnp.uint32),
        compiler_params=_params(1),
        name="dispatch",
    )(pad_start, dest_tiles, hpk)


def _experts_kernel(tile_expert_ref, n_used_ref, xs_ref, wg_ref, wu_ref, wd_ref, y_ref, act_scr):
    del tile_expert_ref

    @pl.when(pl.program_id(0) < n_used_ref[0])
    def _():
        h = _unpack_bf16_pairs(xs_ref[...])
        for c0 in range(0, D_FF, FF_CHUNK):
            sl = slice(c0, c0 + FF_CHUNK)
            act_scr[:, sl] = (_silu(_dot(h, wg_ref[:, sl])) * _dot(h, wu_ref[:, sl])).astype(BF16)
        y_ref[...] = _dot(act_scr[...], wd_ref[...])

    @pl.when(pl.program_id(0) >= n_used_ref[0])
    def _():
        y_ref[...] = jnp.zeros_like(y_ref)


def _experts(tile_expert, n_used, xs, wg, wu, wd, layer):
    n_rows, w = xs.shape
    d = 2 * w
    tm = ROW_TILE
    wspec = lambda shape: pl.BlockSpec((None, None) + shape, lambda i, te, nu: (layer, te[i], 0, 0))
    return pl.pallas_call(
        _experts_kernel,
        grid_spec=pltpu.PrefetchScalarGridSpec(
            num_scalar_prefetch=2,
            grid=(n_rows // tm,),
            in_specs=[pl.BlockSpec((tm, w), lambda i, te, nu: (jnp.minimum(i, nu[0] - 1), 0)),
                      wspec((d, D_FF)), wspec((d, D_FF)), wspec((D_FF, d))],
            out_specs=pl.BlockSpec((tm, d), lambda i, te, nu: (i, 0)),
            scratch_shapes=[pltpu.VMEM((tm, D_FF), BF16)]),
        out_shape=jax.ShapeDtypeStruct((n_rows, d), F32),
        compiler_params=_params(1),
        name="experts",
    )(tile_expert, n_used, xs, wg, wu, wd)


def _combine_kernel(dest_hbm, x_ref, meta_ref, y_hbm, gf_ref, o_ref, idx_smem, y_buf, idx_sem, row_sem, *,
                    final_norm):
    tm = x_ref.shape[0]
    idx_copy = pltpu.make_async_copy(dest_hbm.at[pl.program_id(0)], idx_smem, idx_sem)
    idx_copy.start()
    idx_copy.wait()

    def issue(t, carry):
        for k in range(TOP_K):
            row = idx_smem[k * tm + t]
            pltpu.make_async_copy(y_hbm.at[pl.ds(row, 1)], y_buf.at[k, pl.ds(t, 1)], row_sem).start()
        return carry

    lax.fori_loop(0, tm, issue, 0, unroll=8)
    for k in range(TOP_K):
        _row_copy_all_wait(y_hbm.at[pl.ds(0, tm)], y_buf.at[k], row_sem)
    meta = meta_ref[...]
    out = x_ref[...]
    for k in range(TOP_K):
        out = out + meta[:, META_GATE + k:META_GATE + k + 1] * y_buf[k]
    o_ref[...] = _rms(out, gf_ref[...]) if final_norm else out


def _combine(dest_tiles, x, meta, y, g_final):
    t, d = x.shape
    tm = dest_tiles.shape[1] // TOP_K
    any_spec = pl.BlockSpec(memory_space=pl.ANY)
    final_norm = g_final is not None
    gf = (g_final if final_norm else jnp.ones((d,), F32)).reshape(1, d)
    return pl.pallas_call(
        functools.partial(_combine_kernel, final_norm=final_norm),
        grid=(t // tm,),
        in_specs=[any_spec, pl.BlockSpec((tm, d), lambda i: (i, 0)),
                  pl.BlockSpec((tm, LANES), lambda i: (i, 0)), any_spec, _resident((1, d))],
        out_specs=pl.BlockSpec((tm, d), lambda i: (i, 0)),
        out_shape=jax.ShapeDtypeStruct((t, d), F32),
        scratch_shapes=[pltpu.SMEM((TOP_K * tm,), jnp.int32), pltpu.VMEM((TOP_K, tm, d), F32),
                        pltpu.SemaphoreType.DMA, pltpu.SemaphoreType.DMA],
        compiler_params=_params(1),
        name="combine",
    )(dest_tiles, x, meta, y, gf)


def _moe(x, meta, meta_t, counts, hpk, wg, wu, wd, layer, g_final=None):
    t, d = x.shape
    cnt = counts[:, 0].astype(jnp.int32)
    padded = (cnt + ROW_TILE - 1) // ROW_TILE * ROW_TILE
    ends = jnp.cumsum(padded)
    starts = ends - padded
    expert = meta_t[META_EXPERT:META_EXPERT + TOP_K].astype(jnp.int32)
    dest = meta_t[META_POS:META_POS + TOP_K].astype(jnp.int32)
    for e in range(N_EXPERTS):
        dest = dest + jnp.where(expert == e, starts[e], 0)

    def dest_tiles(tile):
        return dest.reshape(TOP_K, t // tile, tile).transpose(1, 0, 2).reshape(-1, TOP_K * tile)

    n_rows = TOP_K * t + (N_EXPERTS + 1) * ROW_TILE
    n_tiles = n_rows // ROW_TILE
    n_used = ends[-1] // ROW_TILE
    tile_start = jnp.minimum(jnp.arange(n_tiles), n_used - 1) * ROW_TILE
    tile_expert = jnp.sum(tile_start[:, None] >= ends[None, :], axis=1).astype(jnp.int32)
    pad_start = jnp.concatenate([starts + cnt, ends[-1:]])
    xs = _dispatch(pad_start, dest_tiles(DISPATCH_TILE), hpk, n_rows)
    y = _experts(tile_expert, n_used.reshape(1).astype(jnp.int32), xs, wg, wu, wd, layer)
    return _combine(dest_tiles(COMBINE_TILE), x, meta, y, g_final)


CONV_TAIL = 16


def _conv_shift_matrix():
    m = np.zeros(((CONV_WIDTH - 1) * CHUNK, CONV_TAIL + CHUNK), np.float32)
    for k in range(CONV_WIDTH - 1):
        for t in range(CHUNK):
            m[k * CHUNK + t, CONV_TAIL + t - (CONV_WIDTH - 1) + k] = 1.0
    return m


def _split_bf16(v, parts):
    out, r = [], v
    for _ in range(parts):
        p = r.astype(BF16)
        out.append(p)
        r = r - p.astype(F32)
    return out


def _ssd_kernel(xbc_ref, z_ref, dt_ref, cw_ref, cb_ref, dtb_ref, alog_ref, dsk_ref, gn_ref, exp_ref, shift_ref,
                o_ref, state_scr, *chunk_scr):
    n_sub = xbc_ref.shape[0] // CHUNK
    ext_scrs, y_scrs = chunk_scr[:n_sub], chunk_scr[n_sub:]

    @pl.when(pl.program_id(1) == 0)
    def _():
        state_scr[...] = jnp.zeros_like(state_scr)
        ext_scrs[n_sub - 1][CHUNK:, :] = jnp.zeros((CONV_TAIL, CONV_CH), BF16)

    for c in range(n_sub):
        rows = pl.ds(c * CHUNK, CHUNK)
        _ssd_chunk(xbc_ref.at[rows], z_ref.at[rows], dt_ref.at[rows], cw_ref, cb_ref, dtb_ref, alog_ref, dsk_ref,
                   gn_ref, exp_ref, shift_ref, o_ref.at[rows], state_scr, ext_scrs[c],
                   ext_scrs[(c - 1) % n_sub], y_scrs[c])


def _ssd_chunk(xbc_ref, z_ref, dt_ref, cw_ref, cb_ref, dtb_ref, alog_ref, dsk_ref, gn_ref, exp_ref, shift_ref,
               o_ref, state_scr, ext_scr, prev_ext_scr, y_scr):
    ext_scr[0:CONV_TAIL, :] = prev_ext_scr[CHUNK:, :]
    ext_scr[CONV_TAIL:, :] = xbc_ref[...]

    def conv_silu(c0, c1):
        shifted = _dot(shift_ref[...], ext_scr[:, c0:c1])
        acc = cb_ref[:, c0:c1] + cw_ref[CONV_WIDTH - 1:CONV_WIDTH, c0:c1] * ext_scr[CONV_TAIL:, c0:c1].astype(F32)
        for k in range(CONV_WIDTH - 1):
            acc = acc + cw_ref[k:k + 1, c0:c1] * shifted[k * CHUNK:(k + 1) * CHUNK, :]
        return _silu(acc)

    dt = dt_ref[...] + dtb_ref[...]
    dt = jnp.maximum(dt, 0.0) + jnp.log(1.0 + jnp.exp(-jnp.abs(dt)))
    da = dt * -jnp.exp(alog_ref[...])
    row = lax.broadcasted_iota(jnp.int32, (CHUNK, CHUNK), 0)
    col = lax.broadcasted_iota(jnp.int32, (CHUNK, CHUNK), 1)
    causal = col <= row
    cs = _dot(causal.astype(BF16), jnp.concatenate(_split_bf16(da, 3), axis=1))
    acum = cs[:, 0:LANES] + cs[:, LANES:2 * LANES] + cs[:, 2 * LANES:]
    acum_t = acum.T
    ea = jnp.exp(acum)
    dtte = dt * jnp.exp(acum[CHUNK - 1:CHUNK, :] - acum)

    per_head = jnp.concatenate(
        [jnp.concatenate(_split_bf16(v, 2), axis=1) for v in (ea, dt, dtte)], axis=0)

    low = lax.broadcasted_iota(jnp.int32, (CHUNK, LANES), 1) < SSM_HEAD_DIM
    def group_inputs(g):
        return (conv_silu(g * GROUP_W, (g + 1) * GROUP_W),
                conv_silu(D_INNER + g * D_STATE, D_INNER + (g + 1) * D_STATE),
                conv_silu(D_INNER + BC_W + g * D_STATE, D_INNER + BC_W + (g + 1) * D_STATE).astype(BF16))

    nxt = group_inputs(0)
    for g in range(N_SSM_GROUPS):
        gsl = slice(g * GROUP_W, (g + 1) * GROUP_W)
        xg, bg, cg = nxt
        if g + 1 < N_SSM_GROUPS:
            nxt = group_inputs(g + 1)
        expanded = _dot(per_head, exp_ref[:, gsl])
        ea_x, dt_x, dtte_x = (expanded[k * CHUNK:(k + 1) * CHUNK, :] for k in range(3))
        xdt = (xg * dt_x).astype(BF16)
        cb = lax.dot_general(cg, bg.astype(BF16), NT_DIMS, preferred_element_type=F32)
        state = state_scr[g]
        y = _dot(cg, state.astype(BF16)) * ea_x + xg * dsk_ref[:, gsl]
        diag = []
        for pair in range(HEADS_PER_GROUP // 2):
            xp = xdt[:, pair * LANES:(pair + 1) * LANES]
            acc = jnp.zeros((CHUNK, LANES), F32)
            for half in range(2):
                h = g * HEADS_PER_GROUP + 2 * pair + half
                seg = acum[:, h:h + 1] - acum_t[h:h + 1, :]
                m = (cb * jnp.exp(jnp.where(causal, seg, NEG_INF))).astype(BF16)
                xm = jnp.where(low if half == 0 else jnp.logical_not(low), xp, jnp.zeros_like(xp))
                acc = acc + _dot(m, xm)
            diag.append(acc)
        y_scr[:, gsl] = y + jnp.concatenate(diag, axis=1)
        w = (xg * dtte_x).astype(BF16)
        state_scr[g] = state * ea_x[CHUNK - 1:CHUNK, :] + _dot(bg.T.astype(BF16), w)

    z = z_ref[...].astype(F32)
    o_ref[...] = _rms(y_scr[...] * _silu(z), gn_ref[...]).astype(o_ref.dtype)


SSD_SUBCHUNKS = 4


def _ssd(xbc, z, dt_raw, conv_w, conv_b, dt_bias, a_log, d_skip, g_norm, batch):
    t = xbc.shape[0]
    step = SSD_SUBCHUNKS * CHUNK
    nc = t // batch // step

    def lane_pad(v):
        return jnp.zeros((1, LANES), F32).at[0, :N_SSM_HEADS].set(v)

    expand = np.zeros((LANES, D_INNER), np.float32)
    for h in range(N_SSM_HEADS):
        expand[h, h * SSM_HEAD_DIM:(h + 1) * SSM_HEAD_DIM] = 1.0
    expand2 = jnp.asarray(np.concatenate([expand, expand], axis=0), BF16)
    tok = lambda b, c: (b * nc + c, 0)
    return pl.pallas_call(
        _ssd_kernel,
        grid=(batch, nc),
        in_specs=[pl.BlockSpec((step, CONV_CH), tok), pl.BlockSpec((step, D_INNER), tok),
                  pl.BlockSpec((step, LANES), tok),
                  _resident((CONV_WIDTH, CONV_CH)), _resident((1, CONV_CH)),
                  _resident((1, LANES)), _resident((1, LANES)),
                  _resident((1, D_INNER)), _resident((1, D_INNER)), _resident((2 * LANES, D_INNER)),
                  _resident(((CONV_WIDTH - 1) * CHUNK, CONV_TAIL + CHUNK))],
        out_specs=pl.BlockSpec((step, D_INNER), tok),
        out_shape=jax.ShapeDtypeStruct((t, D_INNER), BF16),
        scratch_shapes=[pltpu.VMEM((N_SSM_GROUPS, D_STATE, GROUP_W), F32)]
        + [pltpu.VMEM((CONV_TAIL + CHUNK, CONV_CH), BF16)] * SSD_SUBCHUNKS
        + [pltpu.VMEM((CHUNK, D_INNER), F32)] * SSD_SUBCHUNKS,
        compiler_params=_params(2),
        name="ssd",
    )(xbc, z, dt_raw, conv_w, conv_b.reshape(1, CONV_CH), lane_pad(dt_bias), lane_pad(a_log),
      jnp.repeat(d_skip, SSM_HEAD_DIM).reshape(1, D_INNER), g_norm.reshape(1, D_INNER), expand2,
      jnp.asarray(_conv_shift_matrix(), BF16))


def kernel(x, mem, g_mix, g_ffn, g_mem, w_mem_kv, rel_bias, swa_w_in, swa_sinks, swa_w_out, ssm_w_in, ssm_conv_w, ssm_conv_b, ssm_dt_bias, ssm_A_log, ssm_D, ssm_g_norm, ssm_w_out, ffn_w_gate, ffn_w_up, ffn_w_down, moe_w_router, moe_w_gate, moe_w_up, moe_w_down, g_final):
    batch, seq, d = x.shape
    xf = x.reshape(batch * seq, d)
    memf = mem.reshape(batch * MEM_LEN, d)
    moe_wg, moe_wu, moe_wd = moe_w_gate.astype(BF16), moe_w_up.astype(BF16), moe_w_down.astype(BF16)
    for i in range(DEPTH):
        j = i // 2
        (mem_kv,) = _norm_proj(memf, g_mem[i], [w_mem_kv[i].astype(BF16)], [BF16])
        if i % 2 == 0:
            w_in = swa_w_in[j]
            w_q = (w_in[:, :Q_W] * HEAD_DIM ** -0.5).astype(BF16)
            q, kv, xq = _norm_proj(
                xf, g_mix[i], [w_q, w_in[:, Q_W:Q_W + 2 * KV_W].astype(BF16), w_in[:, Q_W + 2 * KV_W:].astype(BF16)],
                [BF16, BF16, BF16])
            attn = _swa(q, kv, rel_bias, swa_sinks[j], batch)
            w_out = swa_w_out[j].astype(BF16)
            xf = _attn_tail(xf, attn, xq, mem_kv, w_out[:Q_W], w_out[Q_W:], g_ffn[i],
                            ffn_w_gate[j].astype(BF16), ffn_w_up[j].astype(BF16), ffn_w_down[j].astype(BF16),
                            batch)
        else:
            w_in = ssm_w_in[j].astype(BF16)
            o_dt = D_INNER + CONV_CH
            w_dt = jnp.zeros((d, LANES), BF16).at[:, :N_SSM_HEADS].set(w_in[:, o_dt:o_dt + N_SSM_HEADS])
            z, xbc, dt_raw, xq = _norm_proj(
                xf, g_mix[i], [w_in[:, :D_INNER], w_in[:, D_INNER:o_dt], w_dt, w_in[:, o_dt + N_SSM_HEADS:]],
                [BF16, BF16, F32, BF16])
            y = _ssd(xbc, z, dt_raw, ssm_conv_w[j], ssm_conv_b[j], ssm_dt_bias[j], ssm_A_log[j],
                     ssm_D[j], ssm_g_norm[j], batch)
            w_out = ssm_w_out[j].astype(BF16)
            xf, meta, meta_t, counts, hpk = _ssd_tail(xf, y, xq, mem_kv, w_out[:D_INNER], w_out[D_INNER:],
                                                      g_ffn[i], moe_w_router[j], batch)
            xf = _moe(xf, meta, meta_t, counts, hpk, moe_wg, moe_wu, moe_wd, j,
                      g_final if i == DEPTH - 1 else None)
    assert DEPTH % 2 == 0
    return xf.reshape(batch, seq, d)
```

```python
import functools
import math

import numpy as np
import jax
import jax.numpy as jnp
from jax import lax
from jax.experimental import pallas as pl
from jax.experimental.pallas import tpu as pltpu

F32 = jnp.float32
BF16 = jnp.bfloat16

D_MODEL = 1024
DEPTH = 4
MEM_LEN = 256
EPS = 1e-6
N_Q_HEADS = 16
N_KV_HEADS = 2
HEAD_DIM = 64
BLOCK = 128
N_BUCKETS = 32
MAX_DISTANCE = 128
N_X_HEADS = 4
X_HEAD_DIM = 256
D_INNER = 2048
SSM_HEAD_DIM = 64
N_SSM_HEADS = 32
N_SSM_GROUPS = 4
HEADS_PER_GROUP = 8
D_STATE = 128
CONV_WIDTH = 4
CHUNK = 128
D_FF = 2816
N_EXPERTS = 8
Q_W = N_Q_HEADS * HEAD_DIM
KV_W = N_KV_HEADS * HEAD_DIM
XQ_W = N_X_HEADS * X_HEAD_DIM
BC_W = N_SSM_GROUPS * D_STATE
CONV_CH = D_INNER + 2 * BC_W
GROUP_W = HEADS_PER_GROUP * SSM_HEAD_DIM

LANES = 128
SUBLANES = 8
VMEM_LIMIT = 56 << 20
NEG_INF = float("-inf")
NT_DIMS = (((1,), (1,)), ((), ()))


def _params(n_axes, vmem=VMEM_LIMIT):
    return pltpu.CompilerParams(dimension_semantics=("arbitrary",) * n_axes, vmem_limit_bytes=vmem)


def _resident(shape):
    nd = len(shape)
    return pl.BlockSpec(shape, lambda *_: (0,) * nd, pipeline_mode=pl.Buffered(1))


def _dot(a, b):
    return jnp.dot(a, b, preferred_element_type=F32)


def _rms(x, g):
    return x * lax.rsqrt(jnp.mean(x * x, axis=-1, keepdims=True) + EPS) * g


def _silu(v):
    return v / (1.0 + jnp.exp(-v))


def _norm_proj_kernel(x_ref, g_ref, *refs, n_out, col_chunk):
    w_refs, o_refs = refs[:n_out], refs[n_out:]
    h = _rms(x_ref[...], g_ref[...]).astype(BF16)
    for w_ref, o_ref in zip(w_refs, o_refs):
        n = w_ref.shape[1]
        for c0 in range(0, n, col_chunk):
            c1 = min(c0 + col_chunk, n)
            o_ref[:, c0:c1] = _dot(h, w_ref[:, c0:c1]).astype(o_ref.dtype)


def _norm_proj(x, g, ws, out_dtypes, tm=512):
    t, d = x.shape
    n_out = len(ws)
    return pl.pallas_call(
        functools.partial(_norm_proj_kernel, n_out=n_out, col_chunk=512),
        grid=(t // tm,),
        in_specs=[pl.BlockSpec((tm, d), lambda i: (i, 0)), _resident((1, d))]
        + [_resident(w.shape) for w in ws],
        out_specs=[pl.BlockSpec((tm, w.shape[1]), lambda i: (i, 0)) for w in ws],
        out_shape=[jax.ShapeDtypeStruct((t, w.shape[1]), dt) for w, dt in zip(ws, out_dtypes)],
        compiler_params=_params(1),
        name="norm_proj",
    )(x, g.reshape(1, d), *ws)


def _bucket_table():
    qi = np.arange(BLOCK)[:, None]
    kj = np.arange(2 * BLOCK)[None, :]
    dist = BLOCK + qi - kj
    max_exact = N_BUCKETS // 2
    d = np.maximum(dist, 0)
    df = np.maximum(d, 1).astype(np.float32)
    far = max_exact + (
        np.log(df / np.float32(max_exact)) / np.float32(math.log(MAX_DISTANCE / max_exact))
        * np.float32(N_BUCKETS - max_exact)
    ).astype(np.int32)
    bucket = np.where(d < max_exact, d, np.minimum(far, N_BUCKETS - 1))
    r = np.arange(BLOCK)[:, None]
    c = np.arange(BLOCK)[None, :]
    return np.where(c <= r, bucket[:, BLOCK:], bucket[:, :BLOCK]).astype(np.int32)


PAIRS_PER_KV = N_Q_HEADS // N_KV_HEADS // 2
SLOT_ROWS = PAIRS_PER_KV * BLOCK


def _swa_slot_head(slot):
    kvh, rest = divmod(slot, 2 * PAIRS_PER_KV)
    half, pair = divmod(rest, PAIRS_PER_KV)
    return 2 * (kvh * PAIRS_PER_KV + pair) + half


def _swa_kernel(relb_ref, bucket_ref, sink_ref, q_ref, kvp_ref, kvc_ref, o_ref, bias_scr, s_scr, p_scr):
    first = (pl.program_id(0) == 0) & (pl.program_id(1) == 0)

    def from_current(n_rows):
        r = lax.broadcasted_iota(jnp.int32, (n_rows, BLOCK), 0) & (BLOCK - 1)
        return lax.broadcasted_iota(jnp.int32, (n_rows, BLOCK), 1) <= r

    @pl.when(first)
    def _():
        bucket = bucket_ref[...]
        for slot in range(N_Q_HEADS):
            h = _swa_slot_head(slot)
            acc = jnp.zeros((BLOCK, BLOCK), F32)
            for n in range(N_BUCKETS):
                acc = jnp.where(bucket == n, relb_ref[n, h], acc)
            rows = slice(slot * BLOCK, (slot + 1) * BLOCK)
            bias_scr[1, rows, :] = acc
            bias_scr[0, rows, :] = jnp.where(from_current(BLOCK), acc, NEG_INF)

    kk = jnp.concatenate([kvp_ref[:, 0:KV_W], kvc_ref[:, 0:KV_W]], axis=0).astype(F32)
    vv = jnp.concatenate([kvp_ref[:, KV_W:], kvc_ref[:, KV_W:]], axis=0).astype(F32)
    low = lax.broadcasted_iota(jnp.int32, kk.shape, 1) < HEAD_DIM

    def placed(t):
        r = pltpu.roll(t, HEAD_DIM, 1)
        return [[jnp.where(low, t, 0.0).astype(BF16), jnp.where(low, 0.0, r).astype(BF16)],
                [jnp.where(low, r, 0.0).astype(BF16), jnp.where(low, 0.0, t).astype(BF16)]]

    k_var, v_var = placed(kk), placed(vv)

    cur_slot = from_current(SLOT_ROWS)
    for v in range(2 * N_KV_HEADS):
        kvh, half = divmod(v, 2)
        q_stack = jnp.concatenate(
            [q_ref[:, (kvh * PAIRS_PER_KV + j) * LANES:(kvh * PAIRS_PER_KV + j + 1) * LANES]
             for j in range(PAIRS_PER_KV)], axis=0)
        s = lax.dot_general(q_stack, k_var[kvh][half], NT_DIMS, preferred_element_type=F32)
        s_scr[v * SLOT_ROWS:(v + 1) * SLOT_ROWS, :] = jnp.where(cur_slot, s[:, BLOCK:], s[:, :BLOCK])

    s = s_scr[...] + bias_scr[jnp.minimum(pl.program_id(1), 1)]
    sink = sink_ref[...]
    m = jnp.maximum(jnp.max(s, axis=-1, keepdims=True), sink)
    p = jnp.exp(s - m)
    p = p / (jnp.sum(p, axis=-1, keepdims=True) + jnp.exp(sink - m))
    p_scr[...] = p.astype(BF16)

    outs = []
    for v in range(2 * N_KV_HEADS):
        kvh, half = divmod(v, 2)
        p = p_scr[v * SLOT_ROWS:(v + 1) * SLOT_ROWS, :]
        zero = jnp.zeros_like(p)
        p_band = jnp.concatenate([jnp.where(cur_slot, zero, p), jnp.where(cur_slot, p, zero)], axis=1)
        outs.append(_dot(p_band, v_var[kvh][half]))
    for kvh in range(N_KV_HEADS):
        for j in range(PAIRS_PER_KV):
            pair = kvh * PAIRS_PER_KV + j
            out = outs[2 * kvh][j * BLOCK:(j + 1) * BLOCK, :] + outs[2 * kvh + 1][j * BLOCK:(j + 1) * BLOCK, :]
            o_ref[:, pair * LANES:(pair + 1) * LANES] = out.astype(o_ref.dtype)


def _swa(q, kv, rel_bias, sinks, batch):
    t = q.shape[0]
    nb = t // batch // BLOCK
    rows = N_Q_HEADS * BLOCK
    bucket = jnp.asarray(_bucket_table())
    slot_sinks = sinks.astype(F32)[np.array([_swa_slot_head(s) for s in range(N_Q_HEADS)])]
    sink_rows = jnp.broadcast_to(jnp.repeat(slot_sinks, BLOCK)[:, None], (rows, LANES))
    smem = pl.BlockSpec(memory_space=pltpu.SMEM)
    return pl.pallas_call(
        _swa_kernel,
        grid=(batch, nb),
        in_specs=[smem, _resident((BLOCK, BLOCK)), _resident((rows, LANES)),
                  pl.BlockSpec((BLOCK, Q_W), lambda b, i: (b * nb + i, 0)),
                  pl.BlockSpec((BLOCK, 2 * KV_W), lambda b, i: (b * nb + jnp.maximum(i - 1, 0), 0)),
                  pl.BlockSpec((BLOCK, 2 * KV_W), lambda b, i: (b * nb + i, 0))],
        out_specs=pl.BlockSpec((BLOCK, Q_W), lambda b, i: (b * nb + i, 0)),
        out_shape=jax.ShapeDtypeStruct((t, Q_W), BF16),
        scratch_shapes=[pltpu.VMEM((2, rows, BLOCK), F32), pltpu.VMEM((rows, BLOCK), F32),
                        pltpu.VMEM((rows, BLOCK), BF16)],
        compiler_params=_params(2),
        name="swa",
    )(rel_bias, bucket, sink_rows, q, kv, kv)


TAIL_TILE = 512
FF_CHUNK = 256


def _cross_attention(xq_ref, mk_ref, mv_ref, cross_scr):
    for h in range(N_X_HEADS):
        sl = slice(h * X_HEAD_DIM, (h + 1) * X_HEAD_DIM)
        s = lax.dot_general(xq_ref[:, sl], mk_ref[:, sl], NT_DIMS, preferred_element_type=F32)
        s = s * (X_HEAD_DIM ** -0.5)
        p = jnp.exp(s - jnp.max(s, axis=-1, keepdims=True))
        p = p / jnp.sum(p, axis=-1, keepdims=True)
        cross_scr[:, sl] = _dot(p.astype(BF16), mv_ref[:, sl]).astype(cross_scr.dtype)


def _swiglu_residual(x1, g_ref, wg_ref, wu_ref, wd_ref, act_scr):
    h = _rms(x1, g_ref[...]).astype(BF16)
    for c0 in range(0, D_FF, FF_CHUNK):
        sl = slice(c0, c0 + FF_CHUNK)
        act_scr[:, sl] = (_silu(_dot(h, wg_ref[:, sl])) * _dot(h, wu_ref[:, sl])).astype(BF16)
    return x1 + _dot(act_scr[...], wd_ref[...])


def _tail_specs(batch, t, a_width, tile=TAIL_TILE):
    nt = t // batch // tile
    tok = lambda w: pl.BlockSpec((tile, w), lambda b, i: (b * nt + i, 0))
    mem = lambda half: pl.BlockSpec((MEM_LEN, XQ_W), lambda b, i: (b, half))
    return nt, tok, [tok(D_MODEL), tok(a_width), tok(XQ_W), mem(0), mem(1),
                     _resident((a_width, D_MODEL)), _resident((XQ_W, D_MODEL))]


def _attn_tail_kernel(x_ref, a_ref, xq_ref, mk_ref, mv_ref, wa_ref, wc_ref, g_ref, wg_ref, wu_ref, wd_ref,
                      o_ref, cross_scr, act_scr):
    _cross_attention(xq_ref, mk_ref, mv_ref, cross_scr)
    x1 = x_ref[...] + _dot(a_ref[...], wa_ref[...]) + _dot(cross_scr[...], wc_ref[...])
    o_ref[...] = _swiglu_residual(x1, g_ref, wg_ref, wu_ref, wd_ref, act_scr)


def _attn_tail(x, attn, xq, mem_kv, wa, wc, g, wg, wu, wd, batch):
    t, d = x.shape
    nt, tok, specs = _tail_specs(batch, t, attn.shape[1])
    return pl.pallas_call(
        _attn_tail_kernel,
        grid=(batch, nt),
        in_specs=specs + [_resident((1, d)), _resident(wg.shape), _resident(wu.shape), _resident(wd.shape)],
        out_specs=tok(d),
        out_shape=jax.ShapeDtypeStruct((t, d), F32),
        scratch_shapes=[pltpu.VMEM((TAIL_TILE, XQ_W), BF16), pltpu.VMEM((TAIL_TILE, D_FF), BF16)],
        compiler_params=_params(2),
        name="attn_tail",
    )(x, attn, xq, mem_kv, mem_kv, wa, wc, g.reshape(1, d), wg, wu, wd)


TOP_K = 2
HI16 = 0xFFFF0000
META_EXPERT, META_POS, META_GATE = 0, 2, 4


def _pack_bf16_pairs(h):
    n = h.shape[1] // 2
    u = pltpu.bitcast(h.astype(BF16).astype(F32), jnp.uint32)
    return (u[:, :n] >> 16) | (u[:, n:] & jnp.uint32(HI16))


def _unpack_bf16_pairs(w):
    lo = pltpu.bitcast(w << 16, F32).astype(BF16)
    hi = pltpu.bitcast(w & jnp.uint32(HI16), F32).astype(BF16)
    return jnp.concatenate([lo, hi], axis=1)


def _route(x1, g_ref, wr_ref, meta_ref, meta_t_ref, cnt_ref, hpk_ref, run_scr):
    h = _rms(x1, g_ref[...])
    hpk_ref[...] = _pack_bf16_pairs(h)
    tm = h.shape[0]
    prod = _dot(jnp.concatenate(_split_bf16(h, 2), axis=0), wr_ref[...])
    prod = prod[:tm, :] + prod[tm:, :]
    logits = prod[:, :LANES] + prod[:, LANES:]
    lt = logits.T[:N_EXPERTS, :]
    sub = lax.broadcasted_iota(jnp.int32, lt.shape, 0)
    m1 = jnp.max(lt, axis=0, keepdims=True)
    i1 = jnp.min(jnp.where(lt == m1, sub, N_EXPERTS), axis=0, keepdims=True)
    lt2 = jnp.where(sub == i1, NEG_INF, lt)
    m2 = jnp.max(lt2, axis=0, keepdims=True)
    i2 = jnp.min(jnp.where(lt2 == m2, sub, N_EXPERTS), axis=0, keepdims=True)
    e = jnp.exp(m2 - m1)
    g1, g2 = 1.0 / (1.0 + e), e / (1.0 + e)

    onehot = jnp.where(jnp.logical_or(sub == i1, sub == i2), 1.0, 0.0)
    onehot16 = jnp.concatenate([onehot, jnp.zeros_like(onehot)], axis=0).astype(BF16)
    earlier = lax.broadcasted_iota(jnp.int32, (tm, tm), 0) < lax.broadcasted_iota(jnp.int32, (tm, tm), 1)
    before = _dot(onehot16, earlier.astype(BF16))[:N_EXPERTS, :] + run_scr[:, 0:1]
    pos1 = jnp.sum(jnp.where(sub == i1, before, 0.0), axis=0, keepdims=True)
    pos2 = jnp.sum(jnp.where(sub == i2, before, 0.0), axis=0, keepdims=True)
    run_scr[...] += jnp.sum(onehot, axis=1, keepdims=True)
    cnt_ref[...] = run_scr[...]

    def put(k, v):
        return jnp.where(sub == k, v, 0.0)

    meta_t = (put(META_EXPERT, i1.astype(F32)) + put(META_EXPERT + 1, i2.astype(F32))
              + put(META_POS, pos1) + put(META_POS + 1, pos2)
              + put(META_GATE, g1) + put(META_GATE + 1, g2))
    meta_t_ref[...] = meta_t
    meta_ref[...] = jnp.concatenate([meta_t, jnp.zeros((LANES - N_EXPERTS, tm), F32)], axis=0).T


def _ssd_tail_kernel(x_ref, y_ref, xq_ref, mk_ref, mv_ref, wy_ref, wc_ref, g_ref, wr_ref,
                     x1_ref, meta_ref, meta_t_ref, cnt_ref, hpk_ref, run_scr, *cross_scrs):
    @pl.when((pl.program_id(0) == 0) & (pl.program_id(1) == 0))
    def _():
        run_scr[...] = jnp.zeros_like(run_scr)

    for s in range(x_ref.shape[0] // TAIL_TILE):
        rows = pl.ds(s * TAIL_TILE, TAIL_TILE)
        _cross_attention(xq_ref.at[rows], mk_ref, mv_ref, cross_scrs[s])
        x1_ref[rows, :] = x_ref[rows, :] + _dot(y_ref[rows, :], wy_ref[...]) + _dot(cross_scrs[s][...], wc_ref[...])
    _route(x1_ref[...], g_ref, wr_ref, meta_ref, meta_t_ref, cnt_ref, hpk_ref, run_scr)


SSD_TAIL_SUBTILES = 2


def _ssd_tail(x, y, xq, mem_kv, wy, wc, g, w_router, batch):
    t, d = x.shape
    tile = SSD_TAIL_SUBTILES * TAIL_TILE
    nt, tok, specs = _tail_specs(batch, t, y.shape[1], tile)
    w1, w2 = _split_bf16(w_router.astype(F32), 2)
    wr = jnp.zeros((d, 2 * LANES), BF16).at[:, :N_EXPERTS].set(w1).at[:, LANES:LANES + N_EXPERTS].set(w2)
    return pl.pallas_call(
        _ssd_tail_kernel,
        grid=(batch, nt),
        in_specs=specs + [_resident((1, d)), _resident((d, 2 * LANES))],
        out_specs=[tok(d), tok(LANES), pl.BlockSpec((N_EXPERTS, tile), lambda b, i: (0, b * nt + i)),
                   pl.BlockSpec((N_EXPERTS, LANES), lambda b, i: (0, 0)), tok(d // 2)],
        out_shape=[jax.ShapeDtypeStruct((t, d), F32), jax.ShapeDtypeStruct((t, LANES), F32),
                   jax.ShapeDtypeStruct((N_EXPERTS, t), F32), jax.ShapeDtypeStruct((N_EXPERTS, LANES), F32),
                   jax.ShapeDtypeStruct((t, d // 2), jnp.uint32)],
        scratch_shapes=[pltpu.VMEM((N_EXPERTS, LANES), F32)]
        + [pltpu.VMEM((TAIL_TILE, XQ_W), BF16)] * SSD_TAIL_SUBTILES,
        compiler_params=_params(2),
        name="ssd_tail",
    )(x, y, xq, mem_kv, mem_kv, wy, wc, g.reshape(1, d), wr)


ROW_TILE = 512
DISPATCH_TILE = 4096
COMBINE_TILE = 1024


def _row_copy_all_wait(src_like, dst_like, sem):
    pltpu.make_async_copy(src_like, dst_like, sem).wait()


def _dispatch_kernel(pad_start_ref, dest_hbm, hpk_ref, xs_out, idx_smem, zero_buf, idx_sem, row_sem, fill_sem):
    tm = hpk_ref.shape[0]

    @pl.when(pl.program_id(0) == 0)
    def _():
        zero_buf[...] = jnp.zeros_like(zero_buf)
        fill_rows = zero_buf.shape[0]
        starts = [pad_start_ref[e] // SUBLANES * SUBLANES for e in range(N_EXPERTS)]
        last = xs_out.shape[0] - fill_rows
        n_tail = -(-(N_EXPERTS + 1) * ROW_TILE // fill_rows)
        starts += [jnp.minimum(pad_start_ref[N_EXPERTS] + k * fill_rows, last) for k in range(n_tail)]
        for s in starts:
            fill = pltpu.make_async_copy(zero_buf, xs_out.at[pl.ds(pl.multiple_of(s, SUBLANES), fill_rows)], fill_sem)
            fill.start()
            fill.wait()

    idx_copy = pltpu.make_async_copy(dest_hbm.at[pl.program_id(0)], idx_smem, idx_sem)
    idx_copy.start()
    idx_copy.wait()

    def issue(t, carry):
        for k in range(TOP_K):
            row = idx_smem[k * tm + t]
            pltpu.make_async_copy(hpk_ref.at[pl.ds(t, 1)], xs_out.at[pl.ds(row, 1)], row_sem).start()
        return carry

    lax.fori_loop(0, tm, issue, 0, unroll=8)
    for _ in range(TOP_K):
        _row_copy_all_wait(hpk_ref, xs_out.at[pl.ds(0, tm)], row_sem)


def _dispatch(pad_start, dest_tiles, hpk, n_rows):
    t, w = hpk.shape
    tm = dest_tiles.shape[1] // TOP_K
    any_spec = pl.BlockSpec(memory_space=pl.ANY)
    return pl.pallas_call(
        _dispatch_kernel,
        grid_spec=pltpu.PrefetchScalarGridSpec(
            num_scalar_prefetch=1,
            grid=(t // tm,),
            in_specs=[any_spec, pl.BlockSpec((tm, w), lambda i, ps: (i, 0))],
            out_specs=any_spec,
            scratch_shapes=[pltpu.SMEM((TOP_K * tm,), jnp.int32), pltpu.VMEM((ROW_TILE + SUBLANES, w), jnp.uint32),
                            pltpu.SemaphoreType.DMA, pltpu.SemaphoreType.DMA, pltpu.SemaphoreType.DMA]),
        out_shape=jax.ShapeDtypeStruct((n_rows, w), jnp.uint32),
        compiler_params=_params(1),
        name="dispatch",
    )(pad_start, dest_tiles, hpk)


def _experts_kernel(tile_expert_ref, n_used_ref, xs_ref, wg_ref, wu_ref, wd_ref, y_ref, act_scr):
    del tile_expert_ref

    @pl.when(pl.program_id(0) < n_used_ref[0])
    def _():
        h = _unpack_bf16_pairs(xs_ref[...])
        for c0 in range(0, D_FF, FF_CHUNK):
            sl = slice(c0, c0 + FF_CHUNK)
            act_scr[:, sl] = (_silu(_dot(h, wg_ref[:, sl])) * _dot(h, wu_ref[:, sl])).astype(BF16)
        y_ref[...] = _dot(act_scr[...], wd_ref[...])

    @pl.when(pl.program_id(0) >= n_used_ref[0])
    def _():
        y_ref[...] = jnp.zeros_like(y_ref)


def _experts(tile_expert, n_used, xs, wg, wu, wd, layer):
    n_rows, w = xs.shape
    d = 2 * w
    tm = ROW_TILE
    wspec = lambda shape: pl.BlockSpec((None, None) + shape, lambda i, te, nu: (layer, te[i], 0, 0))
    return pl.pallas_call(
        _experts_kernel,
        grid_spec=pltpu.PrefetchScalarGridSpec(
            num_scalar_prefetch=2,
            grid=(n_rows // tm,),
            in_specs=[pl.BlockSpec((tm, w), lambda i, te, nu: (jnp.minimum(i, nu[0] - 1), 0)),
                      wspec((d, D_FF)), wspec((d, D_FF)), wspec((D_FF, d))],
            out_specs=pl.BlockSpec((tm, d), lambda i, te, nu: (i, 0)),
            scratch_shapes=[pltpu.VMEM((tm, D_FF), BF16)]),
        out_shape=jax.ShapeDtypeStruct((n_rows, d), F32),
        compiler_params=_params(1),
        name="experts",
    )(tile_expert, n_used, xs, wg, wu, wd)


def _combine_kernel(dest_hbm, x_ref, meta_ref, y_hbm, gf_ref, o_ref, idx_smem, y_buf, idx_sem, row_sem, *,
                    final_norm):
    tm = x_ref.shape[0]
    idx_copy = pltpu.make_async_copy(dest_hbm.at[pl.program_id(0)], idx_smem, idx_sem)
    idx_copy.start()
    idx_copy.wait()

    def issue(t, carry):
        for k in range(TOP_K):
            row = idx_smem[k * tm + t]
            pltpu.make_async_copy(y_hbm.at[pl.ds(row, 1)], y_buf.at[k, pl.ds(t, 1)], row_sem).start()
        return carry

    lax.fori_loop(0, tm, issue, 0, unroll=8)
    for k in range(TOP_K):
        _row_copy_all_wait(y_hbm.at[pl.ds(0, tm)], y_buf.at[k], row_sem)
    meta = meta_ref[...]
    out = x_ref[...]
    for k in range(TOP_K):
        out = out + meta[:, META_GATE + k:META_GATE + k + 1] * y_buf[k]
    o_ref[...] = _rms(out, gf_ref[...]) if final_norm else out


def _combine(dest_tiles, x, meta, y, g_final):
    t, d = x.shape
    tm = dest_tiles.shape[1] // TOP_K
    any_spec = pl.BlockSpec(memory_space=pl.ANY)
    final_norm = g_final is not None
    gf = (g_final if final_norm else jnp.ones((d,), F32)).reshape(1, d)
    return pl.pallas_call(
        functools.partial(_combine_kernel, final_norm=final_norm),
        grid=(t // tm,),
        in_specs=[any_spec, pl.BlockSpec((tm, d), lambda i: (i, 0)),
                  pl.BlockSpec((tm, LANES), lambda i: (i, 0)), any_spec, _resident((1, d))],
        out_specs=pl.BlockSpec((tm, d), lambda i: (i, 0)),
        out_shape=jax.ShapeDtypeStruct((t, d), F32),
        scratch_shapes=[pltpu.SMEM((TOP_K * tm,), jnp.int32), pltpu.VMEM((TOP_K, tm, d), F32),
                        pltpu.SemaphoreType.DMA, pltpu.SemaphoreType.DMA],
        compiler_params=_params(1),
        name="combine",
    )(dest_tiles, x, meta, y, gf)


def _moe(x, meta, meta_t, counts, hpk, wg, wu, wd, layer, g_final=None):
    t, d = x.shape
    cnt = counts[:, 0].astype(jnp.int32)
    padded = (cnt + ROW_TILE - 1) // ROW_TILE * ROW_TILE
    ends = jnp.cumsum(padded)
    starts = ends - padded
    expert = meta_t[META_EXPERT:META_EXPERT + TOP_K].astype(jnp.int32)
    dest = meta_t[META_POS:META_POS + TOP_K].astype(jnp.int32)
    for e in range(N_EXPERTS):
        dest = dest + jnp.where(expert == e, starts[e], 0)

    def dest_tiles(tile):
        return dest.reshape(TOP_K, t // tile, tile).transpose(1, 0, 2).reshape(-1, TOP_K * tile)

    n_rows = TOP_K * t + (N_EXPERTS + 1) * ROW_TILE
    n_tiles = n_rows // ROW_TILE
    n_used = ends[-1] // ROW_TILE
    tile_start = jnp.minimum(jnp.arange(n_tiles), n_used - 1) * ROW_TILE
    tile_expert = jnp.sum(tile_start[:, None] >= ends[None, :], axis=1).astype(jnp.int32)
    pad_start = jnp.concatenate([starts + cnt, ends[-1:]])
    xs = _dispatch(pad_start, dest_tiles(DISPATCH_TILE), hpk, n_rows)
    y = _experts(tile_expert, n_used.reshape(1).astype(jnp.int32), xs, wg, wu, wd, layer)
    return _combine(dest_tiles(COMBINE_TILE), x, meta, y, g_final)


CONV_TAIL = 16


def _conv_shift_matrix():
    m = np.zeros(((CONV_WIDTH - 1) * CHUNK, CONV_TAIL + CHUNK), np.float32)
    for k in range(CONV_WIDTH - 1):
        for t in range(CHUNK):
            m[k * CHUNK + t, CONV_TAIL + t - (CONV_WIDTH - 1) + k] = 1.0
    return m


def _split_bf16(v, parts):
    out, r = [], v
    for _ in range(parts):
        p = r.astype(BF16)
        out.append(p)
        r = r - p.astype(F32)
    return out


def _ssd_kernel(xbc_ref, z_ref, dt_ref, cw_ref, cb_ref, dtb_ref, alog_ref, dsk_ref, gn_ref, exp_ref, shift_ref,
                o_ref, state_scr, *chunk_scr):
    n_sub = xbc_ref.shape[0] // CHUNK
    ext_scrs, y_scrs = chunk_scr[:n_sub], chunk_scr[n_sub:]

    @pl.when(pl.program_id(1) == 0)
    def _():
        state_scr[...] = jnp.zeros_like(state_scr)
        ext_scrs[n_sub - 1][CHUNK:, :] = jnp.zeros((CONV_TAIL, CONV_CH), BF16)

    for c in range(n_sub):
        rows = pl.ds(c * CHUNK, CHUNK)
        _ssd_chunk(xbc_ref.at[rows], z_ref.at[rows], dt_ref.at[rows], cw_ref, cb_ref, dtb_ref, alog_ref, dsk_ref,
                   gn_ref, exp_ref, shift_ref, o_ref.at[rows], state_scr, ext_scrs[c],
                   ext_scrs[(c - 1) % n_sub], y_scrs[c])


def _ssd_chunk(xbc_ref, z_ref, dt_ref, cw_ref, cb_ref, dtb_ref, alog_ref, dsk_ref, gn_ref, exp_ref, shift_ref,
               o_ref, state_scr, ext_scr, prev_ext_scr, y_scr):
    ext_scr[0:CONV_TAIL, :] = prev_ext_scr[CHUNK:, :]
    ext_scr[CONV_TAIL:, :] = xbc_ref[...]

    def conv_silu(c0, c1):
        shifted = _dot(shift_ref[...], ext_scr[:, c0:c1])
        acc = cb_ref[:, c0:c1] + cw_ref[CONV_WIDTH - 1:CONV_WIDTH, c0:c1] * ext_scr[CONV_TAIL:, c0:c1].astype(F32)
        for k in range(CONV_WIDTH - 1):
            acc = acc + cw_ref[k:k + 1, c0:c1] * shifted[k * CHUNK:(k + 1) * CHUNK, :]
        return _silu(acc)

    dt = dt_ref[...] + dtb_ref[...]
    dt = jnp.maximum(dt, 0.0) + jnp.log(1.0 + jnp.exp(-jnp.abs(dt)))
    da = dt * -jnp.exp(alog_ref[...])
    row = lax.broadcasted_iota(jnp.int32, (CHUNK, CHUNK), 0)
    col = lax.broadcasted_iota(jnp.int32, (CHUNK, CHUNK), 1)
    causal = col <= row
    cs = _dot(causal.astype(BF16), jnp.concatenate(_split_bf16(da, 3), axis=1))
    acum = cs[:, 0:LANES] + cs[:, LANES:2 * LANES] + cs[:, 2 * LANES:]
    acum_t = acum.T
    ea = jnp.exp(acum)
    dtte = dt * jnp.exp(acum[CHUNK - 1:CHUNK, :] - acum)

    per_head = jnp.concatenate(
        [jnp.concatenate(_split_bf16(v, 2), axis=1) for v in (ea, dt, dtte)], axis=0)

    low = lax.broadcasted_iota(jnp.int32, (CHUNK, LANES), 1) < SSM_HEAD_DIM
    def group_inputs(g):
        return (conv_silu(g * GROUP_W, (g + 1) * GROUP_W),
                conv_silu(D_INNER + g * D_STATE, D_INNER + (g + 1) * D_STATE),
                conv_silu(D_INNER + BC_W + g * D_STATE, D_INNER + BC_W + (g + 1) * D_STATE).astype(BF16),
                _dot(per_head, exp_ref[:, g * GROUP_W:(g + 1) * GROUP_W]))

    nxt = group_inputs(0)
    for g in range(N_SSM_GROUPS):
        gsl = slice(g * GROUP_W, (g + 1) * GROUP_W)
        xg, bg, cg, expanded = nxt
        if g + 1 < N_SSM_GROUPS:
            nxt = group_inputs(g + 1)
        ea_x, dt_x, dtte_x = (expanded[k * CHUNK:(k + 1) * CHUNK, :] for k in range(3))
        xdt = (xg * dt_x).astype(BF16)
        cb = lax.dot_general(cg, bg.astype(BF16), NT_DIMS, preferred_element_type=F32)
        state = state_scr[g]
        y = _dot(cg, state.astype(BF16)) * ea_x + xg * dsk_ref[:, gsl]
        diag = []
        for pair in range(HEADS_PER_GROUP // 2):
            xp = xdt[:, pair * LANES:(pair + 1) * LANES]
            acc = jnp.zeros((CHUNK, LANES), F32)
            for half in range(2):
                h = g * HEADS_PER_GROUP + 2 * pair + half
                seg = acum[:, h:h + 1] - acum_t[h:h + 1, :]
                m = (cb * jnp.exp(jnp.where(causal, seg, NEG_INF))).astype(BF16)
                xm = jnp.where(low if half == 0 else jnp.logical_not(low), xp, jnp.zeros_like(xp))
                acc = acc + _dot(m, xm)
            diag.append(acc)
        y_scr[:, gsl] = y + jnp.concatenate(diag, axis=1)
        w = (xg * dtte_x).astype(BF16)
        state_scr[g] = state * ea_x[CHUNK - 1:CHUNK, :] + _dot(bg.T.astype(BF16), w)

    z = z_ref[...].astype(F32)
    o_ref[...] = _rms(y_scr[...] * _silu(z), gn_ref[...]).astype(o_ref.dtype)


SSD_SUBCHUNKS = 4


def _ssd(xbc, z, dt_raw, conv_w, conv_b, dt_bias, a_log, d_skip, g_norm, batch):
    t = xbc.shape[0]
    step = SSD_SUBCHUNKS * CHUNK
    nc = t // batch // step

    def lane_pad(v):
        return jnp.zeros((1, LANES), F32).at[0, :N_SSM_HEADS].set(v)

    expand = np.zeros((LANES, D_INNER), np.float32)
    for h in range(N_SSM_HEADS):
        expand[h, h * SSM_HEAD_DIM:(h + 1) * SSM_HEAD_DIM] = 1.0
    expand2 = jnp.asarray(np.concatenate([expand, expand], axis=0), BF16)
    tok = lambda b, c: (b * nc + c, 0)
    return pl.pallas_call(
        _ssd_kernel,
        grid=(batch, nc),
        in_specs=[pl.BlockSpec((step, CONV_CH), tok), pl.BlockSpec((step, D_INNER), tok),
                  pl.BlockSpec((step, LANES), tok),
                  _resident((CONV_WIDTH, CONV_CH)), _resident((1, CONV_CH)),
                  _resident((1, LANES)), _resident((1, LANES)),
                  _resident((1, D_INNER)), _resident((1, D_INNER)), _resident((2 * LANES, D_INNER)),
                  _resident(((CONV_WIDTH - 1) * CHUNK, CONV_TAIL + CHUNK))],
        out_specs=pl.BlockSpec((step, D_INNER), tok),
        out_shape=jax.ShapeDtypeStruct((t, D_INNER), BF16),
        scratch_shapes=[pltpu.VMEM((N_SSM_GROUPS, D_STATE, GROUP_W), F32)]
        + [pltpu.VMEM((CONV_TAIL + CHUNK, CONV_CH), BF16)] * SSD_SUBCHUNKS
        + [pltpu.VMEM((CHUNK, D_INNER), F32)] * SSD_SUBCHUNKS,
        compiler_params=_params(2),
        name="ssd",
    )(xbc, z, dt_raw, conv_w, conv_b.reshape(1, CONV_CH), lane_pad(dt_bias), lane_pad(a_log),
      jnp.repeat(d_skip, SSM_HEAD_DIM).reshape(1, D_INNER), g_norm.reshape(1, D_INNER), expand2,
      jnp.asarray(_conv_shift_matrix(), BF16))


def kernel(x, mem, g_mix, g_ffn, g_mem, w_mem_kv, rel_bias, swa_w_in, swa_sinks, swa_w_out, ssm_w_in, ssm_conv_w, ssm_conv_b, ssm_dt_bias, ssm_A_log, ssm_D, ssm_g_norm, ssm_w_out, ffn_w_gate, ffn_w_up, ffn_w_down, moe_w_router, moe_w_gate, moe_w_up, moe_w_down, g_final):
    batch, seq, d = x.shape
    xf = x.reshape(batch * seq, d)
    memf = mem.reshape(batch * MEM_LEN, d)
    moe_wg, moe_wu, moe_wd = moe_w_gate.astype(BF16), moe_w_up.astype(BF16), moe_w_down.astype(BF16)
    for i in range(DEPTH):
        j = i // 2
        (mem_kv,) = _norm_proj(memf, g_mem[i], [w_mem_kv[i].astype(BF16)], [BF16])
        if i % 2 == 0:
            w_in = swa_w_in[j]
            w_q = (w_in[:, :Q_W] * HEAD_DIM ** -0.5).astype(BF16)
            q, kv, xq = _norm_proj(
                xf, g_mix[i], [w_q, w_in[:, Q_W:Q_W + 2 * KV_W].astype(BF16), w_in[:, Q_W + 2 * KV_W:].astype(BF16)],
                [BF16, BF16, BF16])
            attn = _swa(q, kv, rel_bias, swa_sinks[j], batch)
            w_out = swa_w_out[j].astype(BF16)
            xf = _attn_tail(xf, attn, xq, mem_kv, w_out[:Q_W], w_out[Q_W:], g_ffn[i],
                            ffn_w_gate[j].astype(BF16), ffn_w_up[j].astype(BF16), ffn_w_down[j].astype(BF16),
                            batch)
        else:
            w_in = ssm_w_in[j].astype(BF16)
            o_dt = D_INNER + CONV_CH
            w_dt = jnp.zeros((d, LANES), BF16).at[:, :N_SSM_HEADS].set(w_in[:, o_dt:o_dt + N_SSM_HEADS])
            z, xbc, dt_raw, xq = _norm_proj(
                xf, g_mix[i], [w_in[:, :D_INNER], w_in[:, D_INNER:o_dt], w_dt, w_in[:, o_dt + N_SSM_HEADS:]],
                [BF16, BF16, F32, BF16])
            y = _ssd(xbc, z, dt_raw, ssm_conv_w[j], ssm_conv_b[j], ssm_dt_bias[j], ssm_A_log[j],
                     ssm_D[j], ssm_g_norm[j], batch)
            w_out = ssm_w_out[j].astype(BF16)
            xf, meta, meta_t, counts, hpk = _ssd_tail(xf, y, xq, mem_kv, w_out[:D_INNER], w_out[D_INNER:],
                                                      g_ffn[i], moe_w_router[j], batch)
            xf = _moe(xf, meta, meta_t, counts, hpk, moe_wg, moe_wu, moe_wd, j,
                      g_final if i == DEPTH - 1 else None)
    assert DEPTH % 2 == 0
    return xf.reshape(batch, seq, d)
```
